```python
import math
import jax, jax.numpy as jnp
from jax import lax
import numpy as np

D_MODEL = 1024
BATCH = 4
SEQ = 4096
DEPTH = 2

GRID_W = 64
CTX_LEN = 256
EPS = 1e-6
NEG = -1e30

D_MIX = D_MODEL
FOURIER_GROUPS = 4
FOURIER_CH = 64
D_FOURIER = FOURIER_GROUPS * FOURIER_CH
HEAD_DIM = 64
ATT_HEADS = 8
ATT_KV_HEADS = 2
ATT_GROUP = ATT_HEADS // ATT_KV_HEADS
D_ATT = ATT_HEADS * HEAD_DIM
D_KV = ATT_KV_HEADS * HEAD_DIM
WINDOW = 128
ATT_BLOCK = 128
ROPE_BASE = 10000.0
MLSTM_HEADS = 4
MLSTM_DIM = 64
D_MLSTM = MLSTM_HEADS * MLSTM_DIM
MLSTM_CHUNK = 64
QK_CONV = 3
D_FF = 2816
N_MOD = 9

SPLIT_SIZES = (D_FOURIER, D_ATT, D_KV, D_KV, D_MLSTM, D_MLSTM, D_MLSTM, D_MLSTM, 2 * MLSTM_HEADS, 2 * MLSTM_HEADS)
D_IN = D_FOURIER + D_ATT + 2 * D_KV + 4 * D_MLSTM + 4 * MLSTM_HEADS

kernel_name = "hybrid_fourier_swa_mlstm_macaron_dit"

F32 = jnp.float32


def rmsnorm(t, g):
    tf = t.astype(F32)
    y = tf * lax.rsqrt(jnp.mean(tf * tf, axis=-1, keepdims=True) + EPS)
    return (y * g.astype(F32)).astype(t.dtype)


def modulate(h, shift, scale):
    return h * (1 + scale) + shift


def swiglu(h, w_in, w_out):
    g, u = jnp.split(h @ w_in, 2, axis=-1)
    return (jax.nn.silu(g) * u) @ w_out


def split_columns(u):
    idx = []
    total = 0
    for s in SPLIT_SIZES[:-1]:
        total += s
        idx.append(total)
    return jnp.split(u, idx, axis=-1)


def axial_rope(n):
    rows = n // GRID_W
    row = jnp.broadcast_to(jnp.arange(rows, dtype=F32)[:, None], (rows, GRID_W)).reshape(n)
    col = jnp.broadcast_to(jnp.arange(GRID_W, dtype=F32)[None, :], (rows, GRID_W)).reshape(n)
    nf = HEAD_DIM // 4
    inv = ROPE_BASE ** (-jnp.arange(nf, dtype=F32) / nf)
    ar = row[:, None] * inv
    ac = col[:, None] * inv
    ang = jnp.concatenate([ar, ar, ac, ac], axis=-1)
    return jnp.cos(ang), jnp.sin(ang)


def apply_rope(t, cos, sin):
    t = t.astype(F32)
    t1, t2, t3, t4 = jnp.split(t, 4, axis=-1)
    rot = jnp.concatenate([-t2, t1, -t4, t3], axis=-1)
    return t * cos[:, None, :] + rot * sin[:, None, :]


def fourier_mix(u, w):
    b, n, _ = u.shape
    ug = u.astype(F32).reshape(b, n, FOURIER_GROUPS, FOURIER_CH)
    f = jnp.fft.fftn(ug, axes=(1, 3), norm="ortho").real
    y = jnp.einsum('bngc,gce->bnge', f, w.astype(F32))
    return y.reshape(b, n, D_FOURIER).astype(u.dtype)


def band(t, nb):
    b = t.shape[0]
    pad = jnp.zeros((b, ATT_BLOCK) + t.shape[2:], t.dtype)
    tp = jnp.concatenate([pad, t, pad], axis=1).reshape((b, nb + 2, ATT_BLOCK) + t.shape[2:])
    return jnp.concatenate([tp[:, :-2], tp[:, 1:-1], tp[:, 2:]], axis=2)


def window_attention(q, k, v, kc, vc, sink):
    b, n, _, d = q.shape
    nb = n // ATT_BLOCK
    scale = d ** -0.5
    qb = q.astype(F32).reshape(b, nb, ATT_BLOCK, ATT_KV_HEADS, ATT_GROUP, d)
    kband = band(k.astype(F32), nb)
    vband = band(v.astype(F32), nb)
    s_loc = jnp.einsum('bnqkgd,bnskd->bnkgqs', qb, kband) * scale
    qi = jnp.arange(ATT_BLOCK)[:, None]
    sj = jnp.arange(3 * ATT_BLOCK)[None, :]
    kglob = jnp.arange(nb)[:, None, None] * ATT_BLOCK + sj[None] - ATT_BLOCK
    valid = (jnp.abs(sj - ATT_BLOCK - qi) <= WINDOW)[None] & (kglob >= 0) & (kglob < n)
    s_loc = jnp.where(valid[None, :, None, None], s_loc, NEG)
    s_ctx = jnp.einsum('bnqkgd,bckd->bnkgqc', qb, kc.astype(F32)) * scale
    sink_l = jnp.broadcast_to(sink.astype(F32).reshape(ATT_KV_HEADS, ATT_GROUP)[None, None, :, :, None, None],
                              s_loc.shape[:-1] + (1,))
    p = jax.nn.softmax(jnp.concatenate([s_loc, s_ctx, sink_l], axis=-1), axis=-1)
    nl = 3 * ATT_BLOCK
    nc = kc.shape[1]
    out = (jnp.einsum('bnkgqs,bnskd->bnqkgd', p[..., :nl], vband)
           + jnp.einsum('bnkgqc,bckd->bnqkgd', p[..., nl:nl + nc], vc.astype(F32)))
    return out.reshape(b, n, D_ATT)


def context_attention(qc, kc, vc, sink):
    b, cl, _, d = qc.shape
    qg = qc.astype(F32).reshape(b, cl, ATT_KV_HEADS, ATT_GROUP, d)
    s = jnp.einsum('bqkgd,bskd->bkgqs', qg, kc.astype(F32)) * (d ** -0.5)
    sink_c = jnp.broadcast_to(sink.astype(F32).reshape(ATT_KV_HEADS, ATT_GROUP)[None, :, :, None, None],
                              s.shape[:-1] + (1,))
    p = jax.nn.softmax(jnp.concatenate([s, sink_c], axis=-1), axis=-1)
    out = jnp.einsum('bkgqs,bskd->bqkgd', p[..., :cl], vc.astype(F32))
    return out.reshape(b, cl, D_ATT)


def centred_dwconv(t, w):
    kw = w.shape[0]
    half = kw // 2
    n = t.shape[1]
    tp = jnp.pad(t, ((0, 0), (half, half), (0, 0)))
    out = tp[:, 0:n] * w[0]
    for j in range(1, kw):
        out = out + tp[:, j:j + n] * w[j]
    return out


def mlstm_prepare(mq, mk, mv, mi, mf, conv_w, b_i, b_f):
    b, n, _ = mq.shape
    qk = jax.nn.silu(centred_dwconv(jnp.concatenate([mq, mk], axis=-1), conv_w))
    q, k = jnp.split(qk, 2, axis=-1)

    def heads(t):
        return t.astype(F32).reshape(b, n, MLSTM_HEADS, MLSTM_DIM).transpose(0, 2, 1, 3)

    q = heads(q)
    k = heads(k) * (MLSTM_DIM ** -0.5)
    v = heads(mv)
    li = (mi.astype(F32).reshape(b, n, 2, MLSTM_HEADS) + b_i.astype(F32)).transpose(2, 0, 3, 1)
    lf = jax.nn.log_sigmoid(mf.astype(F32).reshape(b, n, 2, MLSTM_HEADS) + b_f.astype(F32)).transpose(2, 0, 3, 1)
    return q, k, v, li, lf


def mlstm_zero_state(b):
    return (jnp.zeros((b, MLSTM_HEADS, MLSTM_DIM, MLSTM_DIM), F32),
            jnp.zeros((b, MLSTM_HEADS, MLSTM_DIM), F32),
            jnp.zeros((b, MLSTM_HEADS), F32))


def mlstm_chunkwise(q, k, v, li, lf, state):
    b, h, n, d = q.shape
    L = MLSTM_CHUNK
    nc = n // L

    def to_chunks(a):
        return jnp.moveaxis(a.reshape((b, h, nc, L) + a.shape[3:]), 2, 0)

    xs = (to_chunks(q), to_chunks(k), to_chunks(v), to_chunks(li), to_chunks(lf))
    causal = jnp.tril(jnp.ones((L, L), dtype=bool))

    def step(carry, inp):
        C, nv, m = carry
        qc, kc, vc, lic, lfc = inp
        bcum = jnp.cumsum(lfc, axis=-1)
        dmat = bcum[..., :, None] - bcum[..., None, :] + lic[..., None, :]
        dmat = jnp.where(causal, dmat, -jnp.inf)
        g_prev = bcum + m[..., None]
        m_t = jnp.maximum(g_prev, jnp.max(dmat, axis=-1))
        w_intra = jnp.exp(dmat - m_t[..., None])
        w_prev = jnp.exp(g_prev - m_t)
        qk = jnp.einsum('bhtd,bhsd->bhts', qc, kc) * w_intra
        num = (jnp.einsum('bhts,bhsv->bhtv', qk, vc)
               + w_prev[..., None] * jnp.einsum('bhvk,bhtk->bhtv', C, qc))
        den = qk.sum(axis=-1) + w_prev * jnp.einsum('bhtk,bhk->bht', qc, nv)
        hout = num / jnp.maximum(jnp.abs(den), jnp.exp(-m_t))[..., None]
        b_last = bcum[..., -1]
        dlast = b_last[..., None] - bcum + lic
        m_new = jnp.maximum(b_last + m, jnp.max(dlast, axis=-1))
        w_s = jnp.exp(dlast - m_new[..., None])
        decay = jnp.exp(b_last + m - m_new)
        C_new = decay[..., None, None] * C + jnp.einsum('bhs,bhsv,bhsk->bhvk', w_s, vc, kc)
        n_new = decay[..., None] * nv + jnp.einsum('bhs,bhsk->bhk', w_s, kc)
        return (C_new, n_new, m_new), hout

    state, hs = lax.scan(step, state, xs)
    hs = jnp.moveaxis(hs, 0, 2).reshape(b, h, n, d)
    return hs, state


def mlstm_bidir(q, k, v, li, lf, init_fwd, init_bwd):
    h_f, s_f = mlstm_chunkwise(q, k, v, li[0], lf[0], init_fwd)
    h_b, s_b = mlstm_chunkwise(jnp.flip(q, 2), jnp.flip(k, 2), jnp.flip(v, 2),
                               jnp.flip(li[1], -1), jnp.flip(lf[1], -1), init_bwd)
    return h_f + jnp.flip(h_b, 2), s_f, s_b


def mlstm_output(hsum, mo):
    b, h, n, d = hsum.shape
    hh = hsum.transpose(0, 2, 1, 3).reshape(b, n, h * d)
    return (jax.nn.sigmoid(mo.astype(F32)) * hh).astype(mo.dtype)


def trunk_layer(x, xc, mod_x, mod_c, norm_g, w_ffn_in, w_ffn_out, w_in, w_out, w_fourier,
                sink, conv_qk, b_gate_i, b_gate_f, cos, sin, last):
    mx = jnp.split(mod_x, N_MOD, axis=-1)
    mc = jnp.split(mod_c, N_MOD, axis=-1)
    b, n, _ = x.shape
    cl = xc.shape[1]
    dt = x.dtype

    def ffn_half(t, mods, sub, f):
        h = modulate(rmsnorm(t, norm_g[sub]), mods[3 * sub], mods[3 * sub + 1])
        return t + 0.5 * mods[3 * sub + 2] * swiglu(h, w_ffn_in[f], w_ffn_out[f])

    x = ffn_half(x, mx, 0, 0)
    xc = ffn_half(xc, mc, 0, 0)

    hx = modulate(rmsnorm(x, norm_g[1]), mx[3], mx[4])
    hc = modulate(rmsnorm(xc, norm_g[1]), mc[3], mc[4])
    ax, qx, kx, vx, mqx, mkx, mvx, mox, mix_, mfx = split_columns(hx @ w_in)
    ac, qc, kc, vc, mqc, mkc, mvc, moc, mic, mfc = split_columns(hc @ w_in)
    kc = kc.reshape(b, cl, ATT_KV_HEADS, HEAD_DIM)
    vc = vc.reshape(b, cl, ATT_KV_HEADS, HEAD_DIM)

    zero = mlstm_zero_state(b)
    h_ctx, st_f, st_b = mlstm_bidir(*mlstm_prepare(mqc, mkc, mvc, mic, mfc, conv_qk, b_gate_i, b_gate_f), zero, zero)
    h_lat, _, _ = mlstm_bidir(*mlstm_prepare(mqx, mkx, mvx, mix_, mfx, conv_qk, b_gate_i, b_gate_f), st_f, st_b)

    qr = apply_rope(qx.reshape(b, n, ATT_HEADS, HEAD_DIM), cos, sin)
    kr = apply_rope(kx.reshape(b, n, ATT_KV_HEADS, HEAD_DIM), cos, sin)
    att_x = window_attention(qr, kr, vx.reshape(b, n, ATT_KV_HEADS, HEAD_DIM), kc, vc, sink)

    y = jnp.concatenate([fourier_mix(ax, w_fourier), att_x.astype(dt), mlstm_output(h_lat, mox).astype(dt)], axis=-1) @ w_out
    x = x + mx[5] * y
    x = ffn_half(x, mx, 2, 1)

    if not last:
        att_c = context_attention(qc.reshape(b, cl, ATT_HEADS, HEAD_DIM), kc, vc, sink)
        yc = jnp.concatenate([fourier_mix(ac, w_fourier), att_c.astype(dt), mlstm_output(h_ctx, moc).astype(dt)], axis=-1) @ w_out
        xc = xc + mc[5] * yc
        xc = ffn_half(xc, mc, 2, 1)
    return x, xc


def setup_inputs(seed: int = 0) -> dict:
    key = jax.random.key(seed)
    ks = jax.random.split(key, 20)
    D = D_MODEL
    nrm = jax.random.normal
    x = nrm(ks[0], (BATCH, SEQ, D), F32)
    c = nrm(ks[1], (BATCH, D), F32)
    ctx = nrm(ks[2], (BATCH, CTX_LEN, D), F32)
    c_ctx = nrm(ks[3], (D,), F32)
    w_ada = nrm(ks[4], (DEPTH, D, N_MOD * D), F32) * (0.5 * D ** -0.5)
    b_ada = nrm(ks[5], (DEPTH, N_MOD * D), F32) * 0.01
    norm_g = 1.0 + 0.02 * nrm(ks[6], (DEPTH, 3, D), F32)
    w_ffn_in = nrm(ks[7], (DEPTH, 2, D, 2 * D_FF), F32) * (D ** -0.5)
    w_ffn_out = nrm(ks[8], (DEPTH, 2, D_FF, D), F32) * (D_FF ** -0.5)
    w_in = nrm(ks[9], (DEPTH, D, D_IN), F32) * (D ** -0.5)
    w_out = nrm(ks[10], (DEPTH, D_MIX, D), F32) * (D_MIX ** -0.5)
    w_fourier = nrm(ks[11], (DEPTH, FOURIER_GROUPS, FOURIER_CH, FOURIER_CH), F32) * (FOURIER_CH ** -0.5)
    attn_sink = 0.1 * nrm(ks[12], (DEPTH, ATT_HEADS), F32)
    conv_qk = nrm(ks[13], (DEPTH, QK_CONV, 2 * D_MLSTM), F32) * (QK_CONV ** -0.5)
    b_gate_i = 0.1 * nrm(ks[14], (DEPTH, 2, MLSTM_HEADS), F32)
    b_gate_f = jnp.linspace(3.0, 6.0, MLSTM_HEADS, dtype=F32) + 0.1 * nrm(ks[15], (DEPTH, 2, MLSTM_HEADS), F32)
    g_final = 1.0 + 0.02 * nrm(ks[16], (D,), F32)
    return {"x": x, "c": c, "ctx": ctx, "c_ctx": c_ctx, "w_ada": w_ada, "b_ada": b_ada,
            "norm_g": norm_g, "w_ffn_in": w_ffn_in, "w_ffn_out": w_ffn_out, "w_in": w_in,
            "w_out": w_out, "w_fourier": w_fourier, "attn_sink": attn_sink, "conv_qk": conv_qk,
            "b_gate_i": b_gate_i, "b_gate_f": b_gate_f, "g_final": g_final}


def reference(x, c, ctx, c_ctx, w_ada, b_ada, norm_g, w_ffn_in, w_ffn_out, w_in, w_out, w_fourier,
              attn_sink, conv_qk, b_gate_i, b_gate_f, g_final):
    n = x.shape[1]
    cos, sin = axial_rope(n)
    sc = jax.nn.silu(c)
    scc = jax.nn.silu(c_ctx)
    xc = ctx
    for l in range(DEPTH):
        mod_x = (sc @ w_ada[l] + b_ada[l])[:, None, :]
        mod_c = scc @ w_ada[l] + b_ada[l]
        x, xc = trunk_layer(x, xc, mod_x, mod_c, norm_g[l], w_ffn_in[l], w_ffn_out[l], w_in[l], w_out[l],
                            w_fourier[l], attn_sink[l], conv_qk[l], b_gate_i[l], b_gate_f[l], cos, sin,
                            l == DEPTH - 1)
    return rmsnorm(x, g_final)
```

```python
import functools
import math

import jax
import jax.numpy as jnp
from jax import lax
from jax.experimental import pallas as pl
from jax.experimental.pallas import tpu as pltpu

F32 = jnp.float32
BF16 = jnp.bfloat16
HI = lax.Precision.HIGHEST

GRID_W = 64
EPS = 1e-6
NEG = -1e30
FOURIER_GROUPS = 4
FOURIER_CH = 64
D_FOURIER = FOURIER_GROUPS * FOURIER_CH
HEAD_DIM = 64
ATT_HEADS = 8
ATT_KV_HEADS = 2
D_ATT = ATT_HEADS * HEAD_DIM
D_KV = ATT_KV_HEADS * HEAD_DIM
ATT_BLOCK = 128
WINDOW = 128
ROPE_BASE = 10000.0
MLSTM_HEADS = 4
MLSTM_DIM = 64
D_MLSTM = MLSTM_HEADS * MLSTM_DIM
N_MOD = 9
N_GATES = 4 * MLSTM_HEADS

LANES = 128
SUBLANES = 8
VMEM_LIMIT = 56 * 1024 * 1024

MLSTM_L = 128
S_AUG = D_MLSTM + LANES

C_P = 0
C_Q = C_P + 2 * D_FOURIER
C_K = C_Q + D_ATT
C_V = C_K + D_KV
C_MQ = C_V + D_KV
C_MK = C_MQ + D_MLSTM
C_MV = C_MK + D_MLSTM
C_MO = C_MV + D_MLSTM
C_G = C_MO + D_MLSTM
C_END = C_G + LANES


def _cparams(sem):
    return pltpu.CompilerParams(dimension_semantics=sem, vmem_limit_bytes=VMEM_LIMIT)


def _dot(a, b):
    return jnp.dot(a, b, preferred_element_type=F32)


def _dot_nt(a, b):
    return lax.dot_general(a, b, (((1,), (1,)), ((), ())), preferred_element_type=F32)


def _dot_tn(a, b):
    return lax.dot_general(a, b, (((0,), (0,)), ((), ())), preferred_element_type=F32)


def _dot_hi(a, b):
    return jnp.dot(a, b, preferred_element_type=F32, precision=HI)


def _rms_mod(xt, g, shift, scale):
    ms = jnp.mean(xt * xt, axis=-1, keepdims=True)
    y = xt * lax.rsqrt(ms + EPS) * g
    return y * (1.0 + scale) + shift


def _silu(t):
    return t * jax.nn.sigmoid(t)


def _log_sigmoid(t):
    return jnp.minimum(t, 0.0) - jnp.log1p(jnp.exp(-jnp.abs(t)))


def _mod_kernel(c_ref, w_ref, b_ref, o_ref):
    sc = _silu(c_ref[...])
    o_ref[...] = _dot_hi(sc, w_ref[...]) + b_ref[...]


def _modulation(c_all, w_ada, b_ada):
    depth, d, nd = w_ada.shape
    r = c_all.shape[0]
    tn = 1024
    return pl.pallas_call(
        _mod_kernel,
        grid=(depth, nd // tn),
        in_specs=[pl.BlockSpec((r, d), lambda l, j: (0, 0)),
                  pl.BlockSpec((None, d, tn), lambda l, j: (l, 0, j)),
                  pl.BlockSpec((None, 1, tn), lambda l, j: (l, 0, j))],
        out_specs=pl.BlockSpec((None, r, tn), lambda l, j: (l, 0, j)),
        out_shape=jax.ShapeDtypeStruct((depth, r, nd), F32),
        compiler_params=_cparams(("parallel", "parallel")),
        name="modulation",
    )(c_all, w_ada, b_ada.reshape(depth, 1, nd))


def _fold_kernel(wa_ref, wf_ref, cc_ref, sc_ref, o_ref, *, scale):
    wf = wf_ref[...]
    mc = _dot_hi(cc_ref[...], wf) * scale
    ms = _dot_hi(sc_ref[...], wf) * scale
    wa = wa_ref[...]
    o_ref[:, :D_FOURIER] = _dot_hi(wa, mc)
    o_ref[:, D_FOURIER:] = _dot_hi(wa, ms)


def _fourier_fold(wa, wf_bd, cc_bd, sc_bd, scale):
    depth, d, _ = wa.shape
    return pl.pallas_call(
        functools.partial(_fold_kernel, scale=scale),
        grid=(depth,),
        in_specs=[pl.BlockSpec((None, d, D_FOURIER), lambda l: (l, 0, 0)),
                  pl.BlockSpec((None, D_FOURIER, D_FOURIER), lambda l: (l, 0, 0)),
                  pl.BlockSpec((D_FOURIER, D_FOURIER), lambda l: (0, 0)),
                  pl.BlockSpec((D_FOURIER, D_FOURIER), lambda l: (0, 0))],
        out_specs=pl.BlockSpec((None, d, 2 * D_FOURIER), lambda l: (l, 0, 0)),
        out_shape=jax.ShapeDtypeStruct((depth, d, 2 * D_FOURIER), F32),
        compiler_params=_cparams(("parallel",)),
        name="fourier_fold",
    )(wa, wf_bd, cc_bd, sc_bd)


def _ffn_kernel(x_ref, mod_ref, g_ref, win_ref, wout_ref, *rest, sub, d_ff, final):
    o_ref = rest[-1]
    xt = x_ref[...]
    shift = mod_ref[3 * sub:3 * sub + 1, :]
    scale = mod_ref[3 * sub + 1:3 * sub + 2, :]
    gate = mod_ref[3 * sub + 2:3 * sub + 3, :]
    h = _rms_mod(xt, g_ref[sub:sub + 1, :], shift, scale).astype(BF16)
    gu = _dot(h, win_ref[...])
    act = (_silu(gu[:, :d_ff]) * gu[:, d_ff:]).astype(BF16)
    y = xt + (0.5 * gate) * _dot(act, wout_ref[...])
    if final:
        gf_ref = rest[0]
        ms = jnp.mean(y * y, axis=-1, keepdims=True)
        y = y * lax.rsqrt(ms + EPS) * gf_ref[...]
    o_ref[...] = y


def _ffn(x, mod, mod_row, norm_g, w_in, w_out, sub, g_final=None, tm=512):
    b, n, d = x.shape
    d_ff = w_out.shape[0]
    tm = min(tm, n)
    final = g_final is not None
    const = lambda bi, t: (0, 0)
    in_specs = [pl.BlockSpec((None, tm, d), lambda bi, t: (bi, t, 0)),
                pl.BlockSpec((None, N_MOD, d), lambda bi, t: (mod_row(bi), 0, 0)),
                pl.BlockSpec((3, d), const),
                pl.BlockSpec((d, 2 * d_ff), const, pipeline_mode=pl.Buffered(1)),
                pl.BlockSpec((d_ff, d), const, pipeline_mode=pl.Buffered(1))]
    args = [x, mod, norm_g, w_in, w_out]
    if final:
        in_specs.append(pl.BlockSpec((1, d), const))
        args.append(g_final.reshape(1, d))
    return pl.pallas_call(
        functools.partial(_ffn_kernel, sub=sub, d_ff=d_ff, final=final),
        grid=(b, n // tm),
        in_specs=in_specs,
        out_specs=pl.BlockSpec((None, tm, d), lambda bi, t: (bi, t, 0)),
        out_shape=jax.ShapeDtypeStruct((b, n, d), F32),
        compiler_params=_cparams(("parallel", "parallel")),
        name="ffn",
    )(*args)


def _rope128(t, cos, sin_signed, low16):
    fwd = pltpu.roll(t, LANES - 16, axis=1)
    bwd = pltpu.roll(t, 16, axis=1)
    return t * cos + jnp.where(low16, fwd, bwd) * sin_signed


def _dup_halves(t, low64):
    sw = pltpu.roll(t, HEAD_DIM, axis=1)
    return jnp.where(low64, t, sw), jnp.where(low64, sw, t)


def _inproj_kernel(x_ref, xp_ref, xn_ref, mod_ref, g_ref, w_ref, wgt_ref, gbc_ref, gbr_ref, cw_ref,
                   cos_ref, sin_ref,
                   p_ref, q_ref, k2_ref, v2_ref, mq_ref, mk_ref, mv_ref, og_ref, gc_ref, gr_ref, *, rope):
    t = pl.program_id(1)
    nt = pl.num_programs(1)
    tm = x_ref.shape[0]
    shift = mod_ref[3:4, :]
    scale = mod_ref[4:5, :]
    g = g_ref[1:2, :]
    h = _rms_mod(x_ref[...], g, shift, scale).astype(BF16)
    u = _dot(h, w_ref[...])

    p_ref[...] = u[:, C_P:C_Q].astype(BF16)

    lane = lax.broadcasted_iota(jnp.int32, (1, LANES), 1)
    low16 = (lane & 16) == 0
    low64 = lane < HEAD_DIM
    qscale = HEAD_DIM ** -0.5
    if rope:
        cos = cos_ref[...]
        sin = sin_ref[...]
    for cb in range(D_ATT // LANES):
        qb = u[:, C_Q + cb * LANES:C_Q + (cb + 1) * LANES]
        if rope:
            qb = _rope128(qb, cos, sin, low16)
        q_ref[:, cb * LANES:(cb + 1) * LANES] = (qb * qscale).astype(BF16)
    kb = u[:, C_K:C_V]
    if rope:
        kb = _rope128(kb, cos, sin, low16)
    k_lo, k_hi = _dup_halves(kb, low64)
    k2_ref[:, :LANES] = k_lo.astype(BF16)
    k2_ref[:, LANES:] = k_hi.astype(BF16)
    v_lo, v_hi = _dup_halves(u[:, C_V:C_MQ], low64)
    v2_ref[:, :LANES] = v_lo.astype(BF16)
    v2_ref[:, LANES:] = v_hi.astype(BF16)

    uqk = u[:, C_MQ:C_MV]
    xh = jnp.concatenate([xp_ref[...], xn_ref[...]], axis=0)
    hh = _rms_mod(xh, g, shift, scale).astype(BF16)
    uh = _dot(hh, w_ref[:, C_MQ:C_MV])
    prev_row = jnp.where(t > 0, uh[SUBLANES - 1:SUBLANES, :], 0.0)
    next_row = jnp.where(t < nt - 1, uh[SUBLANES:SUBLANES + 1, :], 0.0)
    row = lax.broadcasted_iota(jnp.int32, (tm, 1), 0)
    u_prev = jnp.where(row == 0, prev_row, pltpu.roll(uqk, 1, axis=0))
    u_next = jnp.where(row == tm - 1, next_row, pltpu.roll(uqk, tm - 1, axis=0))
    cv = u_prev * cw_ref[0:1, :] + uqk * cw_ref[1:2, :] + u_next * cw_ref[2:3, :]
    qk = _silu(cv)
    mq_ref[...] = qk[:, :D_MLSTM].astype(BF16)
    mk_ref[...] = (qk[:, D_MLSTM:] * (MLSTM_DIM ** -0.5)).astype(BF16)
    mv_ref[...] = u[:, C_MV:C_MO].astype(BF16)
    og_ref[...] = jax.nn.sigmoid(u[:, C_MO:C_G]).astype(BF16)

    half = N_GATES // 2
    zc = u[:, C_G:C_END] + gbc_ref[...]
    is_f_col = (lane >= half) & (lane < N_GATES)
    gc_ref[...] = jnp.where(is_f_col, _log_sigmoid(zc), zc)
    zr = _dot_nt(wgt_ref[...], h) + gbr_ref[...]
    rowg = lax.broadcasted_iota(jnp.int32, (N_GATES, 1), 0)
    gr_ref[...] = jnp.where(rowg >= half, _log_sigmoid(zr), zr)


def _inproj(x, mod, mod_row, norm_g, w_all, wg_t, gb_col, gb_row, conv_w, cos128, sin128, rope, tm=512):
    b, n, d = x.shape
    tm = min(tm, n)
    nt = n // tm
    r8 = tm // SUBLANES
    n8 = n // SUBLANES
    const = lambda bi, t: (0, 0)
    tok = lambda w: pl.BlockSpec((None, tm, w), lambda bi, t: (bi, t, 0))
    in_specs = [tok(d),
                pl.BlockSpec((None, SUBLANES, d), lambda bi, t: (bi, jnp.maximum(t * r8 - 1, 0), 0)),
                pl.BlockSpec((None, SUBLANES, d), lambda bi, t: (bi, jnp.minimum((t + 1) * r8, n8 - 1), 0)),
                pl.BlockSpec((None, N_MOD, d), lambda bi, t: (mod_row(bi), 0, 0)),
                pl.BlockSpec((3, d), const),
                pl.BlockSpec((d, C_END), const),
                pl.BlockSpec((N_GATES, d), const),
                pl.BlockSpec((1, LANES), const),
                pl.BlockSpec((N_GATES, 1), const),
                pl.BlockSpec((3, 2 * D_MLSTM), const),
                pl.BlockSpec((tm, LANES), lambda bi, t: (t, 0)),
                pl.BlockSpec((tm, LANES), lambda bi, t: (t, 0))]
    out_specs = [tok(2 * D_FOURIER), tok(D_ATT), tok(2 * D_KV), tok(2 * D_KV),
                 tok(D_MLSTM), tok(D_MLSTM), tok(D_MLSTM), tok(D_MLSTM), tok(LANES),
                 pl.BlockSpec((None, N_GATES, tm), lambda bi, t: (bi, 0, t))]
    shp = lambda w, dt: jax.ShapeDtypeStruct((b, n, w), dt)
    out_shape = [shp(2 * D_FOURIER, BF16), shp(D_ATT, BF16), shp(2 * D_KV, BF16), shp(2 * D_KV, BF16),
                 shp(D_MLSTM, BF16), shp(D_MLSTM, BF16), shp(D_MLSTM, BF16), shp(D_MLSTM, BF16),
                 shp(LANES, F32), jax.ShapeDtypeStruct((b, N_GATES, n), F32)]
    return pl.pallas_call(
        functools.partial(_inproj_kernel, rope=rope),
        grid=(b, nt),
        in_specs=in_specs,
        out_specs=out_specs,
        out_shape=out_shape,
        compiler_params=_cparams(("parallel", "parallel")),
        name="inproj",
    )(x, x, x, mod, norm_g, w_all, wg_t, gb_col, gb_row, conv_w, cos128, sin128)


def _dft_kernel(cn_ref, sn_ref, p_ref, y_ref):
    y = _dot(cn_ref[...], p_ref[:, :D_FOURIER]) + _dot(sn_ref[...], p_ref[:, D_FOURIER:])
    y_ref[...] = y.astype(BF16)


def _dft(cn, sn, p, tk=512):
    b, n, _ = p.shape
    tk = min(tk, n)
    return pl.pallas_call(
        _dft_kernel,
        grid=(n // tk, b),
        in_specs=[pl.BlockSpec((tk, n), lambda i, bi: (i, 0)),
                  pl.BlockSpec((tk, n), lambda i, bi: (i, 0)),
                  pl.BlockSpec((None, n, 2 * D_FOURIER), lambda i, bi: (bi, 0, 0))],
        out_specs=pl.BlockSpec((None, tk, D_FOURIER), lambda i, bi: (bi, i, 0)),
        out_shape=jax.ShapeDtypeStruct((b, n, D_FOURIER), BF16),
        compiler_params=_cparams(("parallel", "parallel")),
        name="dft",
    )(cn, sn, p)


def _attn_kernel(sink_ref, q_ref, *rest, local):
    if local:
        kp_ref, kc_ref, kn_ref, vp_ref, vc_ref, vn_ref, kx_ref, vx_ref, o_ref = rest
    else:
        kx_ref, vx_ref, o_ref = rest
    blk = ATT_BLOCK
    lane = lax.broadcasted_iota(jnp.int32, (1, LANES), 1)
    low64 = lane < HEAD_DIM
    group = ATT_HEADS // ATT_KV_HEADS
    if local:
        j = pl.program_id(1)
        nb = pl.num_programs(1)
        kband = jnp.concatenate([kp_ref[...], kc_ref[...], kn_ref[...]], axis=0)
        vband = jnp.concatenate([vp_ref[...], vc_ref[...], vn_ref[...]], axis=0)
        qi = lax.broadcasted_iota(jnp.int32, (group * blk, 3 * blk), 0) & (blk - 1)
        sj = lax.broadcasted_iota(jnp.int32, (group * blk, 3 * blk), 1)
        valid = ((jnp.abs(sj - blk - qi) <= WINDOW)
                 & ((sj >= blk) | (j > 0)) & ((sj < 2 * blk) | (j < nb - 1)))
    for kv in range(ATT_KV_HEADS):
        parts = []
        sinks = []
        for cb in range(group // 2):
            qcb = q_ref[:, (kv * 2 + cb) * LANES:(kv * 2 + cb + 1) * LANES]
            zero = jnp.zeros_like(qcb)
            parts.append(jnp.where(low64, qcb, zero))
            parts.append(jnp.where(low64, zero, qcb))
        for hh in range(group):
            sinks.append(jnp.full((blk, 1), sink_ref[kv * group + hh], F32))
        qs = jnp.concatenate(parts, axis=0)
        sink = jnp.concatenate(sinks, axis=0)
        kx = kx_ref[:, kv * LANES:(kv + 1) * LANES]
        vx = vx_ref[:, kv * LANES:(kv + 1) * LANES]
        s_ctx = _dot_nt(qs, kx)
        m = jnp.maximum(jnp.max(s_ctx, axis=1, keepdims=True), sink)
        if local:
            s_loc = jnp.where(valid, _dot_nt(qs, kband[:, kv * LANES:(kv + 1) * LANES]), NEG)
            m = jnp.maximum(m, jnp.max(s_loc, axis=1, keepdims=True))
        p_ctx = jnp.exp(s_ctx - m)
        den = jnp.sum(p_ctx, axis=1, keepdims=True) + jnp.exp(sink - m)
        o = _dot(p_ctx.astype(BF16), vx)
        if local:
            p_loc = jnp.exp(s_loc - m)
            den = den + jnp.sum(p_loc, axis=1, keepdims=True)
            o = o + _dot(p_loc.astype(BF16), vband[:, kv * LANES:(kv + 1) * LANES])
        o = o / den
        for cb in range(group // 2):
            lo = o[(2 * cb) * blk:(2 * cb + 1) * blk, :]
            hi = o[(2 * cb + 1) * blk:(2 * cb + 2) * blk, :]
            o_ref[:, (kv * 2 + cb) * LANES:(kv * 2 + cb + 1) * LANES] = jnp.where(low64, lo, hi).astype(BF16)


def _attention(sink, q, k2, v2, kx2, vx2, local):
    b, n, _ = q.shape
    cl = kx2.shape[1]
    blk = ATT_BLOCK
    nb = n // blk
    w2 = 2 * D_KV
    qspec = pl.BlockSpec((None, blk, D_ATT), lambda bi, j: (bi, j, 0))
    xspec = pl.BlockSpec((None, cl, w2), lambda bi, j: (bi, 0, 0))
    in_specs = [pl.BlockSpec(memory_space=pltpu.SMEM), qspec]
    args = [sink, q]
    if local:
        prev = pl.BlockSpec((None, blk, w2), lambda bi, j: (bi, jnp.maximum(j - 1, 0), 0))
        cur = pl.BlockSpec((None, blk, w2), lambda bi, j: (bi, j, 0))
        nxt = pl.BlockSpec((None, blk, w2), lambda bi, j: (bi, jnp.minimum(j + 1, nb - 1), 0))
        in_specs += [prev, cur, nxt, prev, cur, nxt]
        args += [k2, k2, k2, v2, v2, v2]
    in_specs += [xspec, xspec]
    args += [kx2, vx2]
    return pl.pallas_call(
        functools.partial(_attn_kernel, local=local),
        grid=(b, nb),
        in_specs=in_specs,
        out_specs=qspec,
        out_shape=jax.ShapeDtypeStruct((b, n, D_ATT), BF16),
        compiler_params=_cparams(("parallel", "parallel")),
        name="attention",
    )(*args)


def _mlstm_kernel(qf_ref, kf_ref, vf_ref, gcf_ref, grf_ref, qb_ref, kb_ref, vb_ref, gcb_ref, grb_ref,
                  s0_ref, m0_ref, hf_ref, hb_ref, sfin_ref, mfin_ref, s_scr, m_scr):
    c = pl.program_id(1)
    nc = pl.num_programs(1)
    L = MLSTM_L
    H = MLSTM_HEADS

    @pl.when(c == 0)
    def _():
        s_scr[...] = s0_ref[...]
        m_scr[...] = m0_ref[...]

    ti = lax.broadcasted_iota(jnp.int32, (L, L), 0)
    si = lax.broadcasted_iota(jnp.int32, (L, L), 1)
    lower = si <= ti
    upper = si >= ti
    tril = jnp.where(lower, 1.0, 0.0).astype(F32)
    triu = jnp.where(upper, 1.0, 0.0).astype(F32)
    lane_h = lax.broadcasted_iota(jnp.int32, (1, D_MLSTM), 1) // MLSTM_DIM
    lane128 = lax.broadcasted_iota(jnp.int32, (1, LANES), 1)
    row_h = lax.broadcasted_iota(jnp.int32, (D_MLSTM, 1), 0) // MLSTM_DIM
    col = lax.broadcasted_iota(jnp.int32, (1, S_AUG), 1)
    col_h = jnp.where(col < D_MLSTM, col // MLSTM_DIM, col - D_MLSTM)
    blockdiag = row_h == col_h

    dirs = ((qf_ref, kf_ref, vf_ref, gcf_ref, grf_ref, hf_ref),
            (qb_ref, kb_ref, vb_ref, gcb_ref, grb_ref, hb_ref))
    for d, (q_ref, k_ref, v_ref, gc_ref, gr_ref, h_ref) in enumerate(dirs):
        q = q_ref[...]
        k = k_ref[...]
        v = v_ref[...]
        gc = gc_ref[...]
        gr = gr_ref[...]
        if d == 0:
            cum_col = _dot_hi(tril, gc)
            cum_row = _dot_hi(gr, triu)
            valid = lower
        else:
            cum_col = _dot_hi(triu, gc)
            cum_row = _dot_hi(gr, tril)
            valid = upper
        s_old = s_scr[d]
        sq = _dot(q, s_old.astype(BF16))
        out = jnp.zeros((L, D_MLSTM), F32)
        wv = jnp.zeros((L, D_MLSTM), F32)
        wn = jnp.zeros((L, LANES), F32)
        dec = jnp.zeros((1, S_AUG), F32)
        for hd in range(H):
            r = d * H + hd
            rf = 2 * H + r
            a_col = cum_col[:, rf:rf + 1]
            a_row = cum_row[rf:rf + 1, :]
            li_row = gr[r:r + 1, :]
            li_col = gc[:, r:r + 1]
            a_tot = a_row[:, L - 1:L] if d == 0 else a_row[:, 0:1]
            m_h = m_scr[r:r + 1, 0:1]
            dm = jnp.where(valid, a_col - a_row + li_row, NEG)
            g_prev = a_col + m_h
            m_t = jnp.maximum(g_prev, jnp.max(dm, axis=1, keepdims=True))
            w = jnp.exp(dm - m_t)
            w_prev = jnp.exp(g_prev - m_t)
            hm = lane_h == hd
            s = _dot_nt(jnp.where(hm, q, jnp.zeros_like(q)), k)
            p = s * w
            num = _dot(p.astype(BF16), v) + w_prev * sq[:, :D_MLSTM]
            den = jnp.sum(p, axis=1, keepdims=True) + w_prev * sq[:, D_MLSTM + hd:D_MLSTM + hd + 1]
            hout = num / jnp.maximum(jnp.abs(den), jnp.exp(-m_t))
            out = jnp.where(hm, hout, out)
            dl_row = a_tot - a_row + li_row
            dl_col = a_tot - a_col + li_col
            m_new = jnp.maximum(a_tot + m_h, jnp.max(dl_row, axis=1, keepdims=True))
            w_col = jnp.exp(dl_col - m_new)
            decay = jnp.exp(a_tot + m_h - m_new)
            wv = jnp.where(hm, w_col, wv)
            wn = jnp.where(lane128 == hd, w_col, wn)
            dec = jnp.where(col_h == hd, decay, dec)
            m_scr[r:r + 1, :] = jnp.broadcast_to(m_new, (1, LANES))
        h_ref[...] = out
        wva = jnp.concatenate([v.astype(F32) * wv, wn], axis=1).astype(BF16)
        upd = _dot_tn(k, wva)
        s_scr[d] = dec * s_old + jnp.where(blockdiag, upd, 0.0)

    @pl.when(c == nc - 1)
    def _():
        sfin_ref[...] = s_scr[...]
        mfin_ref[...] = m_scr[...]


def _mlstm(mq, mk, mv, gcol, grow, s0, m0):
    b, n, _ = mq.shape
    L = MLSTM_L
    nc = n // L
    fwd = lambda w: pl.BlockSpec((None, L, w), lambda bi, c: (bi, c, 0))
    bwd = lambda w: pl.BlockSpec((None, L, w), lambda bi, c: (bi, nc - 1 - c, 0))
    grf = pl.BlockSpec((None, N_GATES, L), lambda bi, c: (bi, 0, c))
    grb = pl.BlockSpec((None, N_GATES, L), lambda bi, c: (bi, 0, nc - 1 - c))
    sspec = pl.BlockSpec((None, 2, D_MLSTM, S_AUG), lambda bi, c: (bi, 0, 0, 0))
    mspec = pl.BlockSpec((None, 2 * MLSTM_HEADS, LANES), lambda bi, c: (bi, 0, 0))
    return pl.pallas_call(
        _mlstm_kernel,
        grid=(b, nc),
        in_specs=[fwd(D_MLSTM), fwd(D_MLSTM), fwd(D_MLSTM), fwd(LANES), grf,
                  bwd(D_MLSTM), bwd(D_MLSTM), bwd(D_MLSTM), bwd(LANES), grb, sspec, mspec],
        out_specs=[fwd(D_MLSTM), bwd(D_MLSTM), sspec, mspec],
        out_shape=[jax.ShapeDtypeStruct((b, n, D_MLSTM), F32), jax.ShapeDtypeStruct((b, n, D_MLSTM), F32),
                   jax.ShapeDtypeStruct((b, 2, D_MLSTM, S_AUG), F32),
                   jax.ShapeDtypeStruct((b, 2 * MLSTM_HEADS, LANES), F32)],
        scratch_shapes=[pltpu.VMEM((2, D_MLSTM, S_AUG), F32), pltpu.VMEM((2 * MLSTM_HEADS, LANES), F32)],
        compiler_params=_cparams(("parallel", "arbitrary")),
        name="mlstm",
    )(mq, mk, mv, gcol, grow, mq, mk, mv, gcol, grow, s0, m0)


def _outproj_kernel(x_ref, mod_ref, yf_ref, att_ref, hf_ref, hb_ref, og_ref, wf_ref, wa_ref, wm_ref, o_ref):
    mh = (og_ref[...].astype(F32) * (hf_ref[...] + hb_ref[...])).astype(BF16)
    y = _dot(yf_ref[...], wf_ref[...]) + _dot(att_ref[...], wa_ref[...]) + _dot(mh, wm_ref[...])
    o_ref[...] = x_ref[...] + mod_ref[5:6, :] * y


def _outproj(x, mod, mod_row, yf, att, hf, hb, og, wo_f, wo_a, wo_m, tm=512):
    b, n, d = x.shape
    tm = min(tm, n)
    const = lambda bi, t: (0, 0)
    tok = lambda w: pl.BlockSpec((None, tm, w), lambda bi, t: (bi, t, 0))
    return pl.pallas_call(
        _outproj_kernel,
        grid=(b, n // tm),
        in_specs=[tok(d), pl.BlockSpec((None, N_MOD, d), lambda bi, t: (mod_row(bi), 0, 0)),
                  tok(D_FOURIER), tok(D_ATT), tok(D_MLSTM), tok(D_MLSTM), tok(D_MLSTM),
                  pl.BlockSpec((D_FOURIER, d), const), pl.BlockSpec((D_ATT, d), const),
                  pl.BlockSpec((D_MLSTM, d), const)],
        out_specs=tok(d),
        out_shape=jax.ShapeDtypeStruct((b, n, d), F32),
        compiler_params=_cparams(("parallel", "parallel")),
        name="outproj",
    )(x, mod, yf, att, hf, hb, og, wo_f, wo_a, wo_m)


def _rope_tables(n):
    rows = n // GRID_W
    row = jnp.broadcast_to(jnp.arange(rows, dtype=F32)[:, None], (rows, GRID_W)).reshape(n)
    colp = jnp.broadcast_to(jnp.arange(GRID_W, dtype=F32)[None, :], (rows, GRID_W)).reshape(n)
    nf = HEAD_DIM // 4
    inv = ROPE_BASE ** (-jnp.arange(nf, dtype=F32) / nf)
    ar = row[:, None] * inv
    ac = colp[:, None] * inv
    ang = jnp.concatenate([ar, ar, ac, ac], axis=-1)
    cos = jnp.cos(ang)
    sin = jnp.sin(ang)
    sign = jnp.where((jnp.arange(HEAD_DIM) & 16) == 0, -1.0, 1.0).astype(F32)
    reps = LANES // HEAD_DIM
    return jnp.tile(cos, (1, reps)), jnp.tile(sin * sign, (1, reps))


def _dft_tables(n):
    nk1 = max(n // 64, 1)
    nk0 = n // nk1
    j = jnp.arange(n, dtype=jnp.int32)
    k1 = jnp.arange(nk1, dtype=jnp.int32) * nk0
    k0 = jnp.arange(nk0, dtype=jnp.int32)
    w = 2.0 * math.pi / n
    ang_a = ((k1[:, None] * j[None, :]) % n).astype(F32) * w
    ang_b = ((k0[:, None] * j[None, :]) % n).astype(F32) * w
    ca, sa = jnp.cos(ang_a)[:, None, :], jnp.sin(ang_a)[:, None, :]
    cb, sb = jnp.cos(ang_b)[None, :, :], jnp.sin(ang_b)[None, :, :]
    cn = (ca * cb - sa * sb).reshape(n, n)
    sn = (sa * cb + ca * sb).reshape(n, n)
    return cn.astype(BF16), sn.astype(BF16)


def _blockdiag(blocks):
    g, c = blocks.shape[-3], blocks.shape[-1]
    eye = jnp.eye(g, dtype=blocks.dtype)
    out = jnp.einsum('...gce,gh->...gche', blocks, eye)
    return out.reshape(blocks.shape[:-3] + (g * c, g * c))


def _channel_dft_blockdiag():
    e = jnp.arange(FOURIER_CH, dtype=jnp.int32)
    ang = ((e[:, None] * e[None, :]) % FOURIER_CH).astype(F32) * (2.0 * math.pi / FOURIER_CH)
    reps = (FOURIER_GROUPS, 1, 1)
    return _blockdiag(jnp.tile(jnp.cos(ang)[None], reps)), _blockdiag(jnp.tile(-jnp.sin(ang)[None], reps))


def kernel(x, c, ctx, c_ctx, w_ada, b_ada, norm_g, w_ffn_in, w_ffn_out, w_in, w_out, w_fourier, attn_sink,
           conv_qk, b_gate_i, b_gate_f, g_final):
    b, n, d = x.shape
    cl = ctx.shape[1]
    depth = w_ada.shape[0]
    nh = 2 * MLSTM_HEADS

    rows = -(-(b + 1) // SUBLANES) * SUBLANES
    c_all = jnp.zeros((rows, d), F32).at[:b].set(c).at[b].set(c_ctx)
    mod = _modulation(c_all, w_ada, b_ada).reshape(depth, rows, N_MOD, d)
    x_row = lambda bi: bi
    c_row = lambda bi: b

    offs = [0]
    for s in (D_FOURIER, D_ATT, D_KV, D_KV, D_MLSTM, D_MLSTM, D_MLSTM, D_MLSTM, nh, nh):
        offs.append(offs[-1] + s)
    w_a = w_in[:, :, offs[0]:offs[1]]
    w_gate = w_in[:, :, offs[8]:offs[10]]
    cc_bd, sc_bd = _channel_dft_blockdiag()
    w_p = _fourier_fold(w_a, _blockdiag(w_fourier), cc_bd, sc_bd, 1.0 / math.sqrt(n * FOURIER_CH))
    w_p_ctx = w_p * math.sqrt(n / cl)
    pad = jnp.zeros((depth, d, LANES - N_GATES), F32)
    w_mid = w_in[:, :, offs[1]:offs[8]]
    w_all = jnp.concatenate([w_p, w_mid, w_gate, pad], axis=-1).astype(BF16)
    w_all_ctx = jnp.concatenate([w_p_ctx, w_mid, w_gate, pad], axis=-1).astype(BF16)
    wg_t = jnp.swapaxes(w_gate, 1, 2).astype(BF16)
    gbias = jnp.concatenate([b_gate_i.reshape(depth, nh), b_gate_f.reshape(depth, nh)], axis=-1)
    gb_col = jnp.pad(gbias, ((0, 0), (0, LANES - N_GATES)))[:, None, :]
    gb_row = gbias[:, :, None]
    wf_in = w_ffn_in.astype(BF16)
    wf_out = w_ffn_out.astype(BF16)
    wo = w_out.astype(BF16)
    wo_f = wo[:, :D_FOURIER]
    wo_a = wo[:, D_FOURIER:D_FOURIER + D_ATT]
    wo_m = wo[:, D_FOURIER + D_ATT:]

    cos128, sin128 = _rope_tables(n)
    cn, sn = _dft_tables(n)
    cnc, snc = _dft_tables(cl)
    zeros_c = jnp.zeros((cl, LANES), F32)
    s_zero = jnp.zeros((b, 2, D_MLSTM, S_AUG), F32)
    m_zero = jnp.zeros((b, nh, LANES), F32)

    xc = ctx
    for l in range(depth):
        last = l == depth - 1
        ml = mod[l]
        x = _ffn(x, ml, x_row, norm_g[l], wf_in[l, 0], wf_out[l, 0], 0)
        xc = _ffn(xc, ml, c_row, norm_g[l], wf_in[l, 0], wf_out[l, 0], 0)

        px, qx, k2x, v2x, mqx, mkx, mvx, ogx, gcx, grx = _inproj(
            x, ml, x_row, norm_g[l], w_all[l], wg_t[l], gb_col[l], gb_row[l], conv_qk[l], cos128, sin128, True)
        pc, qc, k2c, v2c, mqc, mkc, mvc, ogc, gcc, grc = _inproj(
            xc, ml, c_row, norm_g[l], w_all_ctx[l], wg_t[l], gb_col[l], gb_row[l], conv_qk[l],
            zeros_c, zeros_c, False)

        hcf, hcb, s_ctx, m_ctx = _mlstm(mqc, mkc, mvc, gcc, grc, s_zero, m_zero)
        hxf, hxb, _, _ = _mlstm(mqx, mkx, mvx, gcx, grx, s_ctx, m_ctx)
        att_x = _attention(attn_sink[l], qx, k2x, v2x, k2c, v2c, True)
        yf_x = _dft(cn, sn, px)
        x = _outproj(x, ml, x_row, yf_x, att_x, hxf, hxb, ogx, wo_f[l], wo_a[l], wo_m[l])
        x = _ffn(x, ml, x_row, norm_g[l], wf_in[l, 1], wf_out[l, 1], 2, g_final=g_final if last else None)

        if not last:
            att_c = _attention(attn_sink[l], qc, None, None, k2c, v2c, False)
            yf_c = _dft(cnc, snc, pc)
            xc = _outproj(xc, ml, c_row, yf_c, att_c, hcf, hcb, ogc, wo_f[l], wo_a[l], wo_m[l])
            xc = _ffn(xc, ml, c_row, norm_g[l], wf_in[l, 1], wf_out[l, 1], 2)
    return x
```

```python
import functools
import math

import jax
import jax.numpy as jnp
from jax import lax
from jax.experimental import pallas as pl
from jax.experimental.pallas import tpu as pltpu

F32 = jnp.float32
BF16 = jnp.bfloat16
HI = lax.Precision.HIGHEST

GRID_W = 64
EPS = 1e-6
NEG = -1e30
FOURIER_GROUPS = 4
FOURIER_CH = 64
D_FOURIER = FOURIER_GROUPS * FOURIER_CH
HEAD_DIM = 64
ATT_HEADS = 8
ATT_KV_HEADS = 2
D_ATT = ATT_HEADS * HEAD_DIM
D_KV = ATT_KV_HEADS * HEAD_DIM
ATT_BLOCK = 128
WINDOW = 128
ROPE_BASE = 10000.0
MLSTM_HEADS = 4
MLSTM_DIM = 64
D_MLSTM = MLSTM_HEADS * MLSTM_DIM
N_MOD = 9
N_GATES = 4 * MLSTM_HEADS

LANES = 128
SUBLANES = 8
VMEM_LIMIT = 56 * 1024 * 1024

MLSTM_L = 128

C_P = 0
C_Q = C_P + 2 * D_FOURIER
C_K = C_Q + D_ATT
C_V = C_K + D_KV
C_MQ = C_V + D_KV
C_MK = C_MQ + D_MLSTM
C_MV = C_MK + D_MLSTM
C_MO = C_MV + D_MLSTM
C_END = C_MO + D_MLSTM


def _cparams(sem):
    return pltpu.CompilerParams(dimension_semantics=sem, vmem_limit_bytes=VMEM_LIMIT)


def _dot(a, b):
    return jnp.dot(a, b, preferred_element_type=F32)


def _dot_nt(a, b):
    return lax.dot_general(a, b, (((1,), (1,)), ((), ())), preferred_element_type=F32)


def _dot_tn(a, b):
    return lax.dot_general(a, b, (((0,), (0,)), ((), ())), preferred_element_type=F32)


def _dot_split3(x, onehot):
    rows = x.shape[0]
    hi = x.astype(BF16)
    r1 = x - hi.astype(F32)
    mid = r1.astype(BF16)
    lo = (r1 - mid.astype(F32)).astype(BF16)
    y = _dot(jnp.concatenate([hi, mid, lo], axis=0), onehot)
    return y[0:rows] + y[rows:2 * rows] + y[2 * rows:3 * rows]


def _dot_hi(a, b):
    return jnp.dot(a, b, preferred_element_type=F32, precision=HI)


def _rms_mod(xt, g, shift, scale):
    ms = jnp.mean(xt * xt, axis=-1, keepdims=True)
    y = xt * lax.rsqrt(ms + EPS) * g
    return y * (1.0 + scale) + shift


def _silu(t):
    return t * jax.nn.sigmoid(t)


def _log_sigmoid(t):
    return jnp.minimum(t, 0.0) - jnp.log1p(jnp.exp(-jnp.abs(t)))


def _mod_kernel(c_ref, w_ref, b_ref, o_ref):
    sc = _silu(c_ref[...])
    o_ref[...] = _dot_hi(sc, w_ref[...]) + b_ref[...]


def _modulation(c_all, w_ada, b_ada):
    depth, d, nd = w_ada.shape
    r = c_all.shape[0]
    tn = 1024
    return pl.pallas_call(
        _mod_kernel,
        grid=(depth, nd // tn),
        in_specs=[pl.BlockSpec((r, d), lambda l, j: (0, 0)),
                  pl.BlockSpec((None, d, tn), lambda l, j: (l, 0, j)),
                  pl.BlockSpec((None, 1, tn), lambda l, j: (l, 0, j))],
        out_specs=pl.BlockSpec((None, r, tn), lambda l, j: (l, 0, j)),
        out_shape=jax.ShapeDtypeStruct((depth, r, nd), F32),
        compiler_params=_cparams(("parallel", "parallel")),
        name="modulation",
    )(c_all, w_ada, b_ada.reshape(depth, 1, nd))


def _fold_kernel(wa_ref, wf_ref, cc_ref, sc_ref, o_ref, *, scale):
    wf = wf_ref[...]
    mc = _dot_hi(cc_ref[...], wf) * scale
    ms = _dot_hi(sc_ref[...], wf) * scale
    wa = wa_ref[...]
    o_ref[:, :D_FOURIER] = _dot_hi(wa, mc)
    o_ref[:, D_FOURIER:] = _dot_hi(wa, ms)


def _fourier_fold(wa, wf_bd, cc_bd, sc_bd, scale):
    depth, d, _ = wa.shape
    return pl.pallas_call(
        functools.partial(_fold_kernel, scale=scale),
        grid=(depth,),
        in_specs=[pl.BlockSpec((None, d, D_FOURIER), lambda l: (l, 0, 0)),
                  pl.BlockSpec((None, D_FOURIER, D_FOURIER), lambda l: (l, 0, 0)),
                  pl.BlockSpec((D_FOURIER, D_FOURIER), lambda l: (0, 0)),
                  pl.BlockSpec((D_FOURIER, D_FOURIER), lambda l: (0, 0))],
        out_specs=pl.BlockSpec((None, d, 2 * D_FOURIER), lambda l: (l, 0, 0)),
        out_shape=jax.ShapeDtypeStruct((depth, d, 2 * D_FOURIER), F32),
        compiler_params=_cparams(("parallel",)),
        name="fourier_fold",
    )(wa, wf_bd, cc_bd, sc_bd)


def _ffn_kernel(x_ref, mod_ref, g_ref, win_ref, wout_ref, *rest, sub, d_ff, final):
    o_ref = rest[-1]
    xt = x_ref[...]
    shift = mod_ref[3 * sub:3 * sub + 1, :]
    scale = mod_ref[3 * sub + 1:3 * sub + 2, :]
    gate = mod_ref[3 * sub + 2:3 * sub + 3, :]
    h = _rms_mod(xt, g_ref[sub:sub + 1, :], shift, scale).astype(BF16)
    gu = _dot(h, win_ref[...])
    act = (_silu(gu[:, :d_ff]) * gu[:, d_ff:]).astype(BF16)
    y = xt + (0.5 * gate) * _dot(act, wout_ref[...])
    if final:
        gf_ref = rest[0]
        ms = jnp.mean(y * y, axis=-1, keepdims=True)
        y = y * lax.rsqrt(ms + EPS) * gf_ref[...]
    o_ref[...] = y


def _ffn(x, mod, mod_row, norm_g, w_in, w_out, sub, g_final=None, tm=512):
    b, n, d = x.shape
    d_ff = w_out.shape[0]
    tm = min(tm, n)
    final = g_final is not None
    const = lambda bi, t: (0, 0)
    in_specs = [pl.BlockSpec((None, tm, d), lambda bi, t: (bi, t, 0)),
                pl.BlockSpec((None, N_MOD, d), lambda bi, t: (mod_row(bi), 0, 0)),
                pl.BlockSpec((3, d), const),
                pl.BlockSpec((d, 2 * d_ff), const, pipeline_mode=pl.Buffered(1)),
                pl.BlockSpec((d_ff, d), const, pipeline_mode=pl.Buffered(1))]
    args = [x, mod, norm_g, w_in, w_out]
    if final:
        in_specs.append(pl.BlockSpec((1, d), const))
        args.append(g_final.reshape(1, d))
    return pl.pallas_call(
        functools.partial(_ffn_kernel, sub=sub, d_ff=d_ff, final=final),
        grid=(b, n // tm),
        in_specs=in_specs,
        out_specs=pl.BlockSpec((None, tm, d), lambda bi, t: (bi, t, 0)),
        out_shape=jax.ShapeDtypeStruct((b, n, d), F32),
        compiler_params=_cparams(("parallel", "parallel")),
        name="ffn",
    )(*args)


def _rope128(t, cos, sin_signed, low16):
    fwd = pltpu.roll(t, LANES - 16, axis=1)
    bwd = pltpu.roll(t, 16, axis=1)
    return t * cos + jnp.where(low16, fwd, bwd) * sin_signed


def _dup_halves(t, low64):
    sw = pltpu.roll(t, HEAD_DIM, axis=1)
    return jnp.where(low64, t, sw), jnp.where(low64, sw, t)


def _inproj_kernel(x_ref, xp_ref, xn_ref, mod_ref, g_ref, w_ref, wgt_ref, gbr_ref, cw_ref,
                   cos_ref, sin_ref,
                   p_ref, qt_ref, k_ref, vt_ref, mk_ref, mqt_ref, mvt_ref, og_ref, gr_ref, *, rope):
    t = pl.program_id(1)
    nt = pl.num_programs(1)
    tm = x_ref.shape[0]
    shift = mod_ref[3:4, :]
    scale = mod_ref[4:5, :]
    g = g_ref[1:2, :]
    h = _rms_mod(x_ref[...], g, shift, scale).astype(BF16)
    u = _dot(h, w_ref[...])

    p_ref[...] = u[:, C_P:C_Q].astype(BF16)

    lane = lax.broadcasted_iota(jnp.int32, (1, LANES), 1)
    low16 = (lane & 16) == 0
    qscale = HEAD_DIM ** -0.5
    if rope:
        cos = cos_ref[...]
        sin = sin_ref[...]
    for cb in range(D_ATT // LANES):
        qb = u[:, C_Q + cb * LANES:C_Q + (cb + 1) * LANES]
        if rope:
            qb = _rope128(qb, cos, sin, low16)
        qt_ref[cb * LANES:(cb + 1) * LANES, :] = jnp.transpose(qb * qscale).astype(BF16)
    kb = u[:, C_K:C_V]
    if rope:
        kb = _rope128(kb, cos, sin, low16)
    k_ref[...] = kb.astype(BF16)
    vt_ref[...] = jnp.transpose(u[:, C_V:C_MQ]).astype(BF16)

    uqk = u[:, C_MQ:C_MV]
    xh = jnp.concatenate([xp_ref[...], xn_ref[...]], axis=0)
    hh = _rms_mod(xh, g, shift, scale).astype(BF16)
    uh = _dot(hh, w_ref[:, C_MQ:C_MV])
    prev_row = jnp.where(t > 0, uh[SUBLANES - 1:SUBLANES, :], 0.0)
    next_row = jnp.where(t < nt - 1, uh[SUBLANES:SUBLANES + 1, :], 0.0)
    row = lax.broadcasted_iota(jnp.int32, (tm, 1), 0)
    u_prev = jnp.where(row == 0, prev_row, pltpu.roll(uqk, 1, axis=0))
    u_next = jnp.where(row == tm - 1, next_row, pltpu.roll(uqk, tm - 1, axis=0))
    cv = u_prev * cw_ref[0:1, :] + uqk * cw_ref[1:2, :] + u_next * cw_ref[2:3, :]
    qk = _silu(cv)
    mqt_ref[...] = jnp.transpose(qk[:, :D_MLSTM]).astype(BF16)
    mk_ref[...] = (qk[:, D_MLSTM:] * (MLSTM_DIM ** -0.5)).astype(BF16)
    mvt_ref[...] = jnp.transpose(u[:, C_MV:C_MO]).astype(BF16)
    og_ref[...] = jax.nn.sigmoid(u[:, C_MO:C_END]).astype(BF16)

    half = N_GATES // 2
    zr = _dot_nt(wgt_ref[...], h) + gbr_ref[...]
    rowg = lax.broadcasted_iota(jnp.int32, (N_GATES, 1), 0)
    gr_ref[...] = jnp.where(rowg >= half, _log_sigmoid(zr), zr)


def _inproj(x, mod, mod_row, norm_g, w_all, wg_t, gb_row, conv_w, cos128, sin128, rope, tm=512):
    b, n, d = x.shape
    tm = min(tm, n)
    nt = n // tm
    r8 = tm // SUBLANES
    n8 = n // SUBLANES
    const = lambda bi, t: (0, 0)
    tok = lambda w: pl.BlockSpec((None, tm, w), lambda bi, t: (bi, t, 0))
    in_specs = [tok(d),
                pl.BlockSpec((None, SUBLANES, d), lambda bi, t: (bi, jnp.maximum(t * r8 - 1, 0), 0)),
                pl.BlockSpec((None, SUBLANES, d), lambda bi, t: (bi, jnp.minimum((t + 1) * r8, n8 - 1), 0)),
                pl.BlockSpec((None, N_MOD, d), lambda bi, t: (mod_row(bi), 0, 0)),
                pl.BlockSpec((3, d), const),
                pl.BlockSpec((d, C_END), const),
                pl.BlockSpec((N_GATES, d), const),
                pl.BlockSpec((N_GATES, 1), const),
                pl.BlockSpec((3, 2 * D_MLSTM), const),
                pl.BlockSpec((tm, LANES), lambda bi, t: (t, 0)),
                pl.BlockSpec((tm, LANES), lambda bi, t: (t, 0))]
    trn = lambda rows: pl.BlockSpec((None, rows, tm), lambda bi, t: (bi, 0, t))
    out_specs = [tok(2 * D_FOURIER), trn(D_ATT), tok(D_KV), trn(D_KV),
                 tok(D_MLSTM), trn(D_MLSTM), trn(D_MLSTM), tok(D_MLSTM), trn(N_GATES)]
    shp = lambda w, dt: jax.ShapeDtypeStruct((b, n, w), dt)
    shpt = lambda rows, dt: jax.ShapeDtypeStruct((b, rows, n), dt)
    out_shape = [shp(2 * D_FOURIER, BF16), shpt(D_ATT, BF16), shp(D_KV, BF16), shpt(D_KV, BF16),
                 shp(D_MLSTM, BF16), shpt(D_MLSTM, BF16), shpt(D_MLSTM, BF16), shp(D_MLSTM, BF16),
                 shpt(N_GATES, F32)]
    return pl.pallas_call(
        functools.partial(_inproj_kernel, rope=rope),
        grid=(b, nt),
        in_specs=in_specs,
        out_specs=out_specs,
        out_shape=out_shape,
        compiler_params=_cparams(("parallel", "parallel")),
        name="inproj",
    )(x, x, x, mod, norm_g, w_all, wg_t, gb_row, conv_w, cos128, sin128)


def _dft_kernel(cn_ref, sn_ref, p_ref, y_ref):
    y = _dot(cn_ref[...], p_ref[:, :D_FOURIER]) + _dot(sn_ref[...], p_ref[:, D_FOURIER:])
    y_ref[...] = y.astype(BF16)


def _dft(cn, sn, p, tk=512):
    b, n, _ = p.shape
    tk = min(tk, n)
    return pl.pallas_call(
        _dft_kernel,
        grid=(n // tk, b),
        in_specs=[pl.BlockSpec((tk, n), lambda i, bi: (i, 0)),
                  pl.BlockSpec((tk, n), lambda i, bi: (i, 0)),
                  pl.BlockSpec((None, n, 2 * D_FOURIER), lambda i, bi: (bi, 0, 0))],
        out_specs=pl.BlockSpec((None, tk, D_FOURIER), lambda i, bi: (bi, i, 0)),
        out_shape=jax.ShapeDtypeStruct((b, n, D_FOURIER), BF16),
        compiler_params=_cparams(("parallel", "parallel")),
        name="dft",
    )(cn, sn, p)


def _attn_kernel_old(sink_ref, q_ref, *rest, local):
    if local:
        kp_ref, kc_ref, kn_ref, vp_ref, vc_ref, vn_ref, kx_ref, vx_ref, o_ref = rest
    else:
        kx_ref, vx_ref, o_ref = rest
    blk = ATT_BLOCK
    lane = lax.broadcasted_iota(jnp.int32, (1, LANES), 1)
    low64 = lane < HEAD_DIM
    group = ATT_HEADS // ATT_KV_HEADS
    if local:
        j = pl.program_id(1)
        nb = pl.num_programs(1)
        kband = jnp.concatenate([kp_ref[...], kc_ref[...], kn_ref[...]], axis=0)
        vband = jnp.concatenate([vp_ref[...], vc_ref[...], vn_ref[...]], axis=0)
        qi = lax.broadcasted_iota(jnp.int32, (group * blk, 3 * blk), 0) & (blk - 1)
        sj = lax.broadcasted_iota(jnp.int32, (group * blk, 3 * blk), 1)
        valid = ((jnp.abs(sj - blk - qi) <= WINDOW)
                 & ((sj >= blk) | (j > 0)) & ((sj < 2 * blk) | (j < nb - 1)))
    for kv in range(ATT_KV_HEADS):
        parts = []
        sinks = []
        for cb in range(group // 2):
            qcb = q_ref[:, (kv * 2 + cb) * LANES:(kv * 2 + cb + 1) * LANES]
            zero = jnp.zeros_like(qcb)
            parts.append(jnp.where(low64, qcb, zero))
            parts.append(jnp.where(low64, zero, qcb))
        for hh in range(group):
            sinks.append(jnp.full((blk, 1), sink_ref[kv * group + hh], F32))
        qs = jnp.concatenate(parts, axis=0)
        sink = jnp.concatenate(sinks, axis=0)
        kx = kx_ref[:, kv * LANES:(kv + 1) * LANES]
        vx = vx_ref[:, kv * LANES:(kv + 1) * LANES]
        s_ctx = _dot_nt(qs, kx)
        m = jnp.maximum(jnp.max(s_ctx, axis=1, keepdims=True), sink)
        if local:
            s_loc = jnp.where(valid, _dot_nt(qs, kband[:, kv * LANES:(kv + 1) * LANES]), NEG)
            m = jnp.maximum(m, jnp.max(s_loc, axis=1, keepdims=True))
        p_ctx = jnp.exp(s_ctx - m)
        den = jnp.sum(p_ctx, axis=1, keepdims=True) + jnp.exp(sink - m)
        o = _dot(p_ctx.astype(BF16), vx)
        if local:
            p_loc = jnp.exp(s_loc - m)
            den = den + jnp.sum(p_loc, axis=1, keepdims=True)
            o = o + _dot(p_loc.astype(BF16), vband[:, kv * LANES:(kv + 1) * LANES])
        o = o / den
        for cb in range(group // 2):
            lo = o[(2 * cb) * blk:(2 * cb + 1) * blk, :]
            hi = o[(2 * cb + 1) * blk:(2 * cb + 2) * blk, :]
            o_ref[:, (kv * 2 + cb) * LANES:(kv * 2 + cb + 1) * LANES] = jnp.where(low64, lo, hi).astype(BF16)


def _attention_old(sink, q, k2, v2, kx2, vx2, local):
    b, n, _ = q.shape
    cl = kx2.shape[1]
    blk = ATT_BLOCK
    nb = n // blk
    w2 = 2 * D_KV
    qspec = pl.BlockSpec((None, blk, D_ATT), lambda bi, j: (bi, j, 0))
    xspec = pl.BlockSpec((None, cl, w2), lambda bi, j: (bi, 0, 0))
    in_specs = [pl.BlockSpec(memory_space=pltpu.SMEM), qspec]
    args = [sink, q]
    if local:
        prev = pl.BlockSpec((None, blk, w2), lambda bi, j: (bi, jnp.maximum(j - 1, 0), 0))
        cur = pl.BlockSpec((None, blk, w2), lambda bi, j: (bi, j, 0))
        nxt = pl.BlockSpec((None, blk, w2), lambda bi, j: (bi, jnp.minimum(j + 1, nb - 1), 0))
        in_specs += [prev, cur, nxt, prev, cur, nxt]
        args += [k2, k2, k2, v2, v2, v2]
    in_specs += [xspec, xspec]
    args += [kx2, vx2]
    return pl.pallas_call(
        functools.partial(_attn_kernel, local=local),
        grid=(b, nb),
        in_specs=in_specs,
        out_specs=qspec,
        out_shape=jax.ShapeDtypeStruct((b, n, D_ATT), BF16),
        compiler_params=_cparams(("parallel", "parallel")),
        name="attention",
    )(*args)


def _attn_kernel(sink_ref, qt_ref, *rest, local):
    if local:
        kp_ref, kc_ref, kn_ref, vp_ref, vc_ref, vn_ref, kx_ref, vxt_ref, o_ref = rest
    else:
        kx_ref, vxt_ref, o_ref = rest
    blk = ATT_BLOCK
    group = ATT_HEADS // ATT_KV_HEADS
    gw = group * blk
    cl = kx_ref.shape[0]
    lane_head = lax.broadcasted_iota(jnp.int32, (1, gw), 1) // blk
    if local:
        j = pl.program_id(1)
        nb = pl.num_programs(1)
        ks = lax.broadcasted_iota(jnp.int32, (blk, blk), 0)
        qi = lax.broadcasted_iota(jnp.int32, (blk, blk), 1)
        bias_p = jnp.where((jnp.abs(ks - blk - qi) <= WINDOW) & (j > 0), 0.0, NEG)
        bias_n = jnp.where((jnp.abs(ks + blk - qi) <= WINDOW) & (j < nb - 1), 0.0, NEG)
        bias_p = jnp.concatenate([bias_p] * group, axis=1)
        bias_n = jnp.concatenate([bias_n] * group, axis=1)
        kall = jnp.concatenate([kx_ref[...], kp_ref[...], kc_ref[...], kn_ref[...]], axis=0)
        vtall = jnp.concatenate([vxt_ref[...], vp_ref[...], vc_ref[...], vn_ref[...]], axis=1)
    else:
        kall = kx_ref[...]
        vtall = vxt_ref[...]
    zq = jnp.zeros((HEAD_DIM, gw), BF16)
    for kv in range(ATT_KV_HEADS):
        r0 = kv * group * HEAD_DIM
        qblk = jnp.concatenate([qt_ref[r0 + i * HEAD_DIM:r0 + (i + 1) * HEAD_DIM, :] for i in range(group)], axis=1)
        qt_g = jnp.concatenate([qblk, zq] if kv == 0 else [zq, qblk], axis=0)
        s = _dot(kall, qt_g)
        sink = jnp.zeros((1, gw), F32)
        for i in range(group):
            sink = jnp.where(lane_head == i, sink_ref[kv * group + i], sink)
        if local:
            segs = [s[:cl], s[cl:cl + blk] + bias_p, s[cl + blk:cl + 2 * blk], s[cl + 2 * blk:] + bias_n]
        else:
            segs = [s]
        m = sink
        for sg in segs:
            m = jnp.maximum(m, jnp.max(sg, axis=0, keepdims=True))
        den = jnp.exp(sink - m)
        ps = []
        for sg in segs:
            p = jnp.exp(sg - m)
            den = den + jnp.sum(p, axis=0, keepdims=True)
            ps.append(p.astype(BF16))
        p_all = jnp.concatenate(ps, axis=0) if local else ps[0]
        vt_g = vtall[kv * HEAD_DIM:(kv + 1) * HEAD_DIM, :]
        ot = _dot(vt_g, p_all) * (1.0 / den)
        for cb in range(group // 2):
            pair = jnp.concatenate([ot[:, (2 * cb) * blk:(2 * cb + 1) * blk],
                                    ot[:, (2 * cb + 1) * blk:(2 * cb + 2) * blk]], axis=0)
            c0 = (kv * (group // 2) + cb) * LANES
            o_ref[:, c0:c0 + LANES] = jnp.transpose(pair).astype(BF16)


def _attention(sink, qt, k, vt, kx, vxt, local):
    b, _, n = qt.shape
    cl = kx.shape[1]
    blk = ATT_BLOCK
    nb = n // blk
    qspec = pl.BlockSpec((None, D_ATT, blk), lambda bi, j: (bi, 0, j))
    in_specs = [pl.BlockSpec(memory_space=pltpu.SMEM), qspec]
    args = [sink, qt]
    if local:
        pj = lambda j: jnp.maximum(j - 1, 0)
        nj = lambda j: jnp.minimum(j + 1, nb - 1)
        kspec = lambda jm: pl.BlockSpec((None, blk, D_KV), lambda bi, j: (bi, jm(j), 0))
        vspec = lambda jm: pl.BlockSpec((None, D_KV, blk), lambda bi, j: (bi, 0, jm(j)))
        same = lambda j: j
        in_specs += [kspec(pj), kspec(same), kspec(nj), vspec(pj), vspec(same), vspec(nj)]
        args += [k, k, k, vt, vt, vt]
    in_specs += [pl.BlockSpec((None, cl, D_KV), lambda bi, j: (bi, 0, 0)),
                 pl.BlockSpec((None, D_KV, cl), lambda bi, j: (bi, 0, 0))]
    args += [kx, vxt]
    return pl.pallas_call(
        functools.partial(_attn_kernel, local=local),
        grid=(b, nb),
        in_specs=in_specs,
        out_specs=pl.BlockSpec((None, blk, D_ATT), lambda bi, j: (bi, j, 0)),
        out_shape=jax.ShapeDtypeStruct((b, n, D_ATT), BF16),
        compiler_params=_cparams(("parallel", "parallel")),
        name="attention",
    )(*args)


def _mlstm_dir(d, k_ref, qt_ref, vt_ref, gr_ref, h_ref, c_scr, n_scr, m_scr, consts):
    L = MLSTM_L
    H = MLSTM_HEADS
    DH = MLSTM_DIM
    tri, valid_t, lane_h, row_h, blockdiag, nmask = consts
    k = k_ref[...]
    qt = qt_ref[...]
    vt = vt_ref[...]
    gr = gr_ref[...]
    cum = _dot_split3(gr, tri)
    r0 = d * H
    li8 = gr[0:2 * H, :]
    a8 = cum[2 * H:4 * H, :]
    a_tot8 = a8[:, L - 1:L] if d == 0 else a8[:, 0:1]
    m_in8 = m_scr[:, 0:1]
    g_prev8 = a8 + m_in8
    b8 = li8 - a8
    b_t = jnp.transpose(jnp.concatenate([b8, jnp.zeros((L - 2 * H, L), F32)], axis=0))

    c_old = c_scr[d]
    n_old = n_scr[d]
    inter = _dot(jnp.concatenate([c_old, n_old], axis=0).astype(BF16), qt)

    kst = jnp.concatenate([jnp.where(lane_h == hd, k, jnp.zeros_like(k)) for hd in range(H)], axis=0)
    s_t = _dot(kst, qt)
    p_parts, m_rows, wp_rows, den_rows = [], [], [], []
    for hd in range(H):
        r = r0 + hd
        d_t = jnp.where(valid_t, b_t[:, r:r + 1] + a8[r:r + 1, :], NEG)
        g_prev = g_prev8[r:r + 1, :]
        m_t = jnp.maximum(g_prev, jnp.max(d_t, axis=0, keepdims=True))
        p_t = s_t[hd * L:(hd + 1) * L, :] * jnp.exp(d_t - m_t)
        w_prev = jnp.exp(g_prev - m_t)
        den = jnp.sum(p_t, axis=0, keepdims=True) + w_prev * inter[D_MLSTM + r:D_MLSTM + r + 1, :]
        p_parts.append(p_t.astype(BF16))
        m_rows.append(m_t)
        wp_rows.append(w_prev)
        den_rows.append(den)
    vbd = jnp.concatenate([jnp.where(row_h == hd, vt, jnp.zeros_like(vt)) for hd in range(H)], axis=1)
    num_t = _dot(vbd, jnp.concatenate(p_parts, axis=0))
    outs = []
    for hd in range(H):
        inv = 1.0 / jnp.maximum(jnp.abs(den_rows[hd]), jnp.exp(-m_rows[hd]))
        sl = slice(hd * DH, (hd + 1) * DH)
        outs.append((num_t[sl, :] + wp_rows[hd] * inter[sl, :]) * inv)
    h_ref[...] = jnp.transpose(jnp.concatenate(outs, axis=0))

    dl8 = a_tot8 - a8 + li8
    m_new8 = jnp.maximum(a_tot8 + m_in8, jnp.max(dl8, axis=1, keepdims=True))
    w8 = jnp.exp(dl8 - m_new8)
    decay8 = jnp.exp(a_tot8 + m_in8 - m_new8)
    wexp = jnp.concatenate([jnp.broadcast_to(w8[r0 + hd:r0 + hd + 1, :], (DH, L)) for hd in range(H)], axis=0)
    dcol = jnp.concatenate([jnp.broadcast_to(decay8[r0 + hd:r0 + hd + 1, :], (DH, 1)) for hd in range(H)], axis=0)
    upd = _dot((vt.astype(F32) * wexp).astype(BF16), k)
    c_scr[d] = dcol * c_old + jnp.where(blockdiag, upd, 0.0)
    n_upd = _dot(w8.astype(BF16), k)
    n_scr[d] = jnp.where(nmask[d], decay8 * n_old + n_upd, 0.0)
    m_scr[r0:r0 + H, :] = jnp.broadcast_to(m_new8[r0:r0 + H, :], (H, LANES))


def _mlstm_kernel(kf_ref, qtf_ref, vtf_ref, grf_ref, kb_ref, qtb_ref, vtb_ref, grb_ref,
                  c0_ref, n0_ref, m0_ref, hf_ref, hb_ref, cfin_ref, nfin_ref, mfin_ref, c_scr, n_scr, m_scr):
    c = pl.program_id(0)
    nc = pl.num_programs(0)
    L = MLSTM_L
    H = MLSTM_HEADS
    nb = kf_ref.shape[0]

    @pl.when(c == 0)
    def _():
        c_scr[...] = c0_ref[...]
        n_scr[...] = n0_ref[...]
        m_scr[...] = m0_ref[...]

    ri = lax.broadcasted_iota(jnp.int32, (L, L), 0)
    ci = lax.broadcasted_iota(jnp.int32, (L, L), 1)
    upper = ci >= ri
    lower = ci <= ri
    triu = jnp.where(upper, 1.0, 0.0).astype(BF16)
    tril = jnp.where(lower, 1.0, 0.0).astype(BF16)
    lane_h = lax.broadcasted_iota(jnp.int32, (1, D_MLSTM), 1) // MLSTM_DIM
    row_h = lax.broadcasted_iota(jnp.int32, (D_MLSTM, 1), 0) // MLSTM_DIM
    blockdiag = row_h == lane_h
    row8 = lax.broadcasted_iota(jnp.int32, (2 * H, 1), 0)
    nmask = [(row8 - d * H) == lane_h for d in range(2)]
    for bi in range(nb):
        st = (c_scr.at[bi], n_scr.at[bi], m_scr.at[bi])
        _mlstm_dir(0, kf_ref.at[bi], qtf_ref.at[bi], vtf_ref.at[bi], grf_ref.at[bi], hf_ref.at[bi], *st,
                   (triu, upper, lane_h, row_h, blockdiag, nmask))
        _mlstm_dir(1, kb_ref.at[bi], qtb_ref.at[bi], vtb_ref.at[bi], grb_ref.at[bi], hb_ref.at[bi], *st,
                   (tril, lower, lane_h, row_h, blockdiag, nmask))

    @pl.when(c == nc - 1)
    def _():
        cfin_ref[...] = c_scr[...]
        nfin_ref[...] = n_scr[...]
        mfin_ref[...] = m_scr[...]


def _mlstm(mk, mqt, mvt, grow, state):
    b, n, _ = mk.shape
    L = MLSTM_L
    nc = n // L
    nh = 2 * MLSTM_HEADS
    tok = lambda cm: pl.BlockSpec((b, L, D_MLSTM), lambda c: (0, cm(c), 0))
    tr = lambda rows, cm: pl.BlockSpec((b, rows, L), lambda c: (0, 0, cm(c)))
    fw = lambda c: c
    bw = lambda c: nc - 1 - c
    cspec = pl.BlockSpec((b, 2, D_MLSTM, D_MLSTM), lambda c: (0, 0, 0, 0))
    nspec = pl.BlockSpec((b, 2, nh, D_MLSTM), lambda c: (0, 0, 0, 0))
    mspec = pl.BlockSpec((b, nh, LANES), lambda c: (0, 0, 0))
    return pl.pallas_call(
        _mlstm_kernel,
        grid=(nc,),
        in_specs=[tok(fw), tr(D_MLSTM, fw), tr(D_MLSTM, fw), tr(N_GATES, fw),
                  tok(bw), tr(D_MLSTM, bw), tr(D_MLSTM, bw), tr(N_GATES, bw), cspec, nspec, mspec],
        out_specs=[tok(fw), tok(bw), cspec, nspec, mspec],
        out_shape=[jax.ShapeDtypeStruct((b, n, D_MLSTM), F32), jax.ShapeDtypeStruct((b, n, D_MLSTM), F32),
                   jax.ShapeDtypeStruct((b, 2, D_MLSTM, D_MLSTM), F32),
                   jax.ShapeDtypeStruct((b, 2, nh, D_MLSTM), F32),
                   jax.ShapeDtypeStruct((b, nh, LANES), F32)],
        scratch_shapes=[pltpu.VMEM((b, 2, D_MLSTM, D_MLSTM), F32), pltpu.VMEM((b, 2, nh, D_MLSTM), F32),
                        pltpu.VMEM((b, nh, LANES), F32)],
        compiler_params=_cparams(("arbitrary",)),
        name="mlstm",
    )(mk, mqt, mvt, grow, mk, mqt, mvt, grow, *state)


def _outproj_kernel(x_ref, mod_ref, yf_ref, att_ref, hf_ref, hb_ref, og_ref, wf_ref, wa_ref, wm_ref, o_ref):
    mh = (og_ref[...].astype(F32) * (hf_ref[...] + hb_ref[...])).astype(BF16)
    y = _dot(yf_ref[...], wf_ref[...]) + _dot(att_ref[...], wa_ref[...]) + _dot(mh, wm_ref[...])
    o_ref[...] = x_ref[...] + mod_ref[5:6, :] * y


def _outproj(x, mod, mod_row, yf, att, hf, hb, og, wo_f, wo_a, wo_m, tm=512):
    b, n, d = x.shape
    tm = min(tm, n)
    const = lambda bi, t: (0, 0)
    tok = lambda w: pl.BlockSpec((None, tm, w), lambda bi, t: (bi, t, 0))
    return pl.pallas_call(
        _outproj_kernel,
        grid=(b, n // tm),
        in_specs=[tok(d), pl.BlockSpec((None, N_MOD, d), lambda bi, t: (mod_row(bi), 0, 0)),
                  tok(D_FOURIER), tok(D_ATT), tok(D_MLSTM), tok(D_MLSTM), tok(D_MLSTM),
                  pl.BlockSpec((D_FOURIER, d), const), pl.BlockSpec((D_ATT, d), const),
                  pl.BlockSpec((D_MLSTM, d), const)],
        out_specs=tok(d),
        out_shape=jax.ShapeDtypeStruct((b, n, d), F32),
        compiler_params=_cparams(("parallel", "parallel")),
        name="outproj",
    )(x, mod, yf, att, hf, hb, og, wo_f, wo_a, wo_m)


def _rope_tables(n):
    rows = n // GRID_W
    row = jnp.broadcast_to(jnp.arange(rows, dtype=F32)[:, None], (rows, GRID_W)).reshape(n)
    colp = jnp.broadcast_to(jnp.arange(GRID_W, dtype=F32)[None, :], (rows, GRID_W)).reshape(n)
    nf = HEAD_DIM // 4
    inv = ROPE_BASE ** (-jnp.arange(nf, dtype=F32) / nf)
    ar = row[:, None] * inv
    ac = colp[:, None] * inv
    ang = jnp.concatenate([ar, ar, ac, ac], axis=-1)
    cos = jnp.cos(ang)
    sin = jnp.sin(ang)
    sign = jnp.where((jnp.arange(HEAD_DIM) & 16) == 0, -1.0, 1.0).astype(F32)
    reps = LANES // HEAD_DIM
    return jnp.tile(cos, (1, reps)), jnp.tile(sin * sign, (1, reps))


def _dft_tables(n):
    nk1 = max(n // 64, 1)
    nk0 = n // nk1
    j = jnp.arange(n, dtype=jnp.int32)
    k1 = jnp.arange(nk1, dtype=jnp.int32) * nk0
    k0 = jnp.arange(nk0, dtype=jnp.int32)
    w = 2.0 * math.pi / n
    ang_a = ((k1[:, None] * j[None, :]) % n).astype(F32) * w
    ang_b = ((k0[:, None] * j[None, :]) % n).astype(F32) * w
    ca, sa = jnp.cos(ang_a)[:, None, :], jnp.sin(ang_a)[:, None, :]
    cb, sb = jnp.cos(ang_b)[None, :, :], jnp.sin(ang_b)[None, :, :]
    cn = (ca * cb - sa * sb).reshape(n, n)
    sn = (sa * cb + ca * sb).reshape(n, n)
    return cn.astype(BF16), sn.astype(BF16)


def _blockdiag(blocks):
    g, c = blocks.shape[-3], blocks.shape[-1]
    eye = jnp.eye(g, dtype=blocks.dtype)
    out = jnp.einsum('...gce,gh->...gche', blocks, eye)
    return out.reshape(blocks.shape[:-3] + (g * c, g * c))


def _channel_dft_blockdiag():
    e = jnp.arange(FOURIER_CH, dtype=jnp.int32)
    ang = ((e[:, None] * e[None, :]) % FOURIER_CH).astype(F32) * (2.0 * math.pi / FOURIER_CH)
    reps = (FOURIER_GROUPS, 1, 1)
    return _blockdiag(jnp.tile(jnp.cos(ang)[None], reps)), _blockdiag(jnp.tile(-jnp.sin(ang)[None], reps))


def kernel(x, c, ctx, c_ctx, w_ada, b_ada, norm_g, w_ffn_in, w_ffn_out, w_in, w_out, w_fourier, attn_sink,
           conv_qk, b_gate_i, b_gate_f, g_final):
    b, n, d = x.shape
    cl = ctx.shape[1]
    depth = w_ada.shape[0]
    nh = 2 * MLSTM_HEADS

    rows = -(-(b + 1) // SUBLANES) * SUBLANES
    c_all = jnp.zeros((rows, d), F32).at[:b].set(c).at[b].set(c_ctx)
    mod = _modulation(c_all, w_ada, b_ada).reshape(depth, rows, N_MOD, d)
    x_row = lambda bi: bi
    c_row = lambda bi: b

    offs = [0]
    for s in (D_FOURIER, D_ATT, D_KV, D_KV, D_MLSTM, D_MLSTM, D_MLSTM, D_MLSTM, nh, nh):
        offs.append(offs[-1] + s)
    w_a = w_in[:, :, offs[0]:offs[1]]
    w_gate = w_in[:, :, offs[8]:offs[10]]
    cc_bd, sc_bd = _channel_dft_blockdiag()
    w_p = _fourier_fold(w_a, _blockdiag(w_fourier), cc_bd, sc_bd, 1.0 / math.sqrt(n * FOURIER_CH))
    w_p_ctx = w_p * math.sqrt(n / cl)
    w_mid = w_in[:, :, offs[1]:offs[8]]
    w_all = jnp.concatenate([w_p, w_mid], axis=-1).astype(BF16)
    w_all_ctx = jnp.concatenate([w_p_ctx, w_mid], axis=-1).astype(BF16)
    wg_t = jnp.swapaxes(w_gate, 1, 2).astype(BF16)
    gbias = jnp.concatenate([b_gate_i.reshape(depth, nh), b_gate_f.reshape(depth, nh)], axis=-1)
    gb_row = gbias[:, :, None]
    wf_in = w_ffn_in.astype(BF16)
    wf_out = w_ffn_out.astype(BF16)
    wo = w_out.astype(BF16)
    wo_f = wo[:, :D_FOURIER]
    wo_a = wo[:, D_FOURIER:D_FOURIER + D_ATT]
    wo_m = wo[:, D_FOURIER + D_ATT:]

    cos128, sin128 = _rope_tables(n)
    cn, sn = _dft_tables(n)
    cnc, snc = _dft_tables(cl)
    zeros_c = jnp.zeros((cl, LANES), F32)
    state0 = (jnp.zeros((b, 2, D_MLSTM, D_MLSTM), F32), jnp.zeros((b, 2, nh, D_MLSTM), F32),
              jnp.zeros((b, nh, LANES), F32))

    xc = ctx
    for l in range(depth):
        last = l == depth - 1
        ml = mod[l]
        x = _ffn(x, ml, x_row, norm_g[l], wf_in[l, 0], wf_out[l, 0], 0)
        xc = _ffn(xc, ml, c_row, norm_g[l], wf_in[l, 0], wf_out[l, 0], 0)

        px, qx, k2x, v2x, mkx, mqtx, mvtx, ogx, grx = _inproj(
            x, ml, x_row, norm_g[l], w_all[l], wg_t[l], gb_row[l], conv_qk[l], cos128, sin128, True)
        pc, qc, k2c, v2c, mkc, mqtc, mvtc, ogc, grc = _inproj(
            xc, ml, c_row, norm_g[l], w_all_ctx[l], wg_t[l], gb_row[l], conv_qk[l], zeros_c, zeros_c, False)

        hcf, hcb, *state_c = _mlstm(mkc, mqtc, mvtc, grc, state0)
        hxf, hxb, *_ = _mlstm(mkx, mqtx, mvtx, grx, state_c)
        att_x = _attention(attn_sink[l], qx, k2x, v2x, k2c, v2c, True)
        yf_x = _dft(cn, sn, px)
        x = _outproj(x, ml, x_row, yf_x, att_x, hxf, hxb, ogx, wo_f[l], wo_a[l], wo_m[l])
        x = _ffn(x, ml, x_row, norm_g[l], wf_in[l, 1], wf_out[l, 1], 2, g_final=g_final if last else None)

        if not last:
            att_c = _attention(attn_sink[l], qc, None, None, k2c, v2c, False)
            yf_c = _dft(cnc, snc, pc)
            xc = _outproj(xc, ml, c_row, yf_c, att_c, hcf, hcb, ogc, wo_f[l], wo_a[l], wo_m[l])
            xc = _ffn(xc, ml, c_row, norm_g[l], wf_in[l, 1], wf_out[l, 1], 2)
    return x
```

```python
import functools
import math

import jax
import jax.numpy as jnp
from jax import lax
from jax.experimental import pallas as pl
from jax.experimental.pallas import tpu as pltpu

F32 = jnp.float32
BF16 = jnp.bfloat16
HI = lax.Precision.HIGHEST

GRID_W = 64
EPS = 1e-6
NEG = -1e30
FOURIER_GROUPS = 4
FOURIER_CH = 64
D_FOURIER = FOURIER_GROUPS * FOURIER_CH
HEAD_DIM = 64
ATT_HEADS = 8
ATT_KV_HEADS = 2
D_ATT = ATT_HEADS * HEAD_DIM
D_KV = ATT_KV_HEADS * HEAD_DIM
ATT_BLOCK = 128
WINDOW = 128
ROPE_BASE = 10000.0
MLSTM_HEADS = 4
MLSTM_DIM = 64
D_MLSTM = MLSTM_HEADS * MLSTM_DIM
N_MOD = 9
N_GATES = 4 * MLSTM_HEADS

LANES = 128
SUBLANES = 8
VMEM_LIMIT = 56 * 1024 * 1024

TM = 512
TK_DFT = 512
DFT_K0 = 64
TQ_ATT = 512
MLSTM_L = 128

C_P = 0
C_Q = C_P + 2 * D_FOURIER
C_K = C_Q + D_ATT
C_V = C_K + D_KV
C_MQ = C_V + D_KV
C_MK = C_MQ + D_MLSTM
C_MV = C_MK + D_MLSTM
C_MO = C_MV + D_MLSTM
C_END = C_MO + D_MLSTM


def _cparams(sem):
    return pltpu.CompilerParams(dimension_semantics=sem, vmem_limit_bytes=VMEM_LIMIT)


def _dot(a, b):
    return jnp.dot(a, b, preferred_element_type=F32)


def _dot_nt(a, b):
    return lax.dot_general(a, b, (((1,), (1,)), ((), ())), preferred_element_type=F32)


def _dot_hi(a, b):
    return jnp.dot(a, b, preferred_element_type=F32, precision=HI)


def _dot_split3(x, onehot):
    rows = x.shape[0]
    hi = x.astype(BF16)
    r1 = x - hi.astype(F32)
    mid = r1.astype(BF16)
    lo = (r1 - mid.astype(F32)).astype(BF16)
    y = _dot(jnp.concatenate([hi, mid, lo], axis=0), onehot)
    return y[0:rows] + y[rows:2 * rows] + y[2 * rows:3 * rows]


def _rms_mod(xt, g, shift, scale):
    ms = jnp.mean(xt * xt, axis=-1, keepdims=True)
    y = xt * lax.rsqrt(ms + EPS) * g
    return y * (1.0 + scale) + shift


def _silu(t):
    return t * jax.nn.sigmoid(t)


def _log_sigmoid(t):
    return jnp.minimum(t, 0.0) - jnp.log1p(jnp.exp(-jnp.abs(t)))


def _mod_kernel(c_ref, w_ref, b_ref, o_ref):
    sc = _silu(c_ref[...])
    o_ref[...] = _dot_hi(sc, w_ref[...]) + b_ref[...]


def _modulation(c_all, w_ada, b_ada):
    depth, d, nd = w_ada.shape
    r = c_all.shape[0]
    tn = 1024
    return pl.pallas_call(
        _mod_kernel,
        grid=(depth, nd // tn),
        in_specs=[pl.BlockSpec((r, d), lambda l, j: (0, 0)),
                  pl.BlockSpec((None, d, tn), lambda l, j: (l, 0, j)),
                  pl.BlockSpec((None, 1, tn), lambda l, j: (l, 0, j))],
        out_specs=pl.BlockSpec((None, r, tn), lambda l, j: (l, 0, j)),
        out_shape=jax.ShapeDtypeStruct((depth, r, nd), F32),
        compiler_params=_cparams(("parallel", "parallel")),
        name="modulation",
    )(c_all, w_ada, b_ada.reshape(depth, 1, nd))


def _fold_kernel(wa_ref, wf_ref, cc_ref, sc_ref, o_ref, *, scale):
    wf = wf_ref[...]
    mc = _dot_hi(cc_ref[...], wf) * scale
    ms = _dot_hi(sc_ref[...], wf) * scale
    wa = wa_ref[...]
    o_ref[:, :D_FOURIER] = _dot_hi(wa, mc)
    o_ref[:, D_FOURIER:] = _dot_hi(wa, ms)


def _fourier_fold(wa, wf_bd, cc_bd, sc_bd, scale):
    depth, d, _ = wa.shape
    return pl.pallas_call(
        functools.partial(_fold_kernel, scale=scale),
        grid=(depth,),
        in_specs=[pl.BlockSpec((None, d, D_FOURIER), lambda l: (l, 0, 0)),
                  pl.BlockSpec((None, D_FOURIER, D_FOURIER), lambda l: (l, 0, 0)),
                  pl.BlockSpec((D_FOURIER, D_FOURIER), lambda l: (0, 0)),
                  pl.BlockSpec((D_FOURIER, D_FOURIER), lambda l: (0, 0))],
        out_specs=pl.BlockSpec((None, d, 2 * D_FOURIER), lambda l: (l, 0, 0)),
        out_shape=jax.ShapeDtypeStruct((depth, d, 2 * D_FOURIER), F32),
        compiler_params=_cparams(("parallel",)),
        name="fourier_fold",
    )(wa, wf_bd, cc_bd, sc_bd)


def _ffn_kernel(*refs, sub, d_ff, mix, final):
    x_ref, mod_ref, g_ref, win_ref, wout_ref = refs[:5]
    o_ref = refs[-1]
    rest = list(refs[5:-1])
    xt = x_ref[...]
    if mix:
        yf_ref, att_ref, hf_ref, hb_ref, og_ref, wo_ref = rest[:6]
        rest = rest[6:]
        mh = (og_ref[...].astype(F32) * (hf_ref[...] + hb_ref[...])).astype(BF16)
        a0, a1 = D_FOURIER, D_FOURIER + D_ATT
        y = (_dot(yf_ref[...], wo_ref[:a0, :]) + _dot(att_ref[...], wo_ref[a0:a1, :])
             + _dot(mh, wo_ref[a1:, :]))
        xt = xt + mod_ref[5:6, :] * y
    shift = mod_ref[3 * sub:3 * sub + 1, :]
    scale = mod_ref[3 * sub + 1:3 * sub + 2, :]
    gate = mod_ref[3 * sub + 2:3 * sub + 3, :]
    h = _rms_mod(xt, g_ref[sub:sub + 1, :], shift, scale).astype(BF16)
    gu = _dot(h, win_ref[...])
    act = (_silu(gu[:, :d_ff]) * gu[:, d_ff:]).astype(BF16)
    y = xt + (0.5 * gate) * _dot(act, wout_ref[...])
    if final:
        gf_ref = rest[0]
        ms = jnp.mean(y * y, axis=-1, keepdims=True)
        y = y * lax.rsqrt(ms + EPS) * gf_ref[...]
    o_ref[...] = y


def _ffn(x, layer, half, sub, mod, mod_row, norm_g, w_in, w_out, mixer=None, w_o=None, g_final=None):
    b, n, d = x.shape
    d_ff = w_out.shape[2]
    tm = min(TM, n)
    mix = mixer is not None
    final = g_final is not None
    tok = lambda w: pl.BlockSpec((None, tm, w), lambda bi, t: (bi, t, 0))
    in_specs = [tok(d),
                pl.BlockSpec((None, None, N_MOD, d), lambda bi, t: (layer, mod_row(bi), 0, 0)),
                pl.BlockSpec((None, 3, d), lambda bi, t: (layer, 0, 0)),
                pl.BlockSpec((None, None, d, 2 * d_ff), lambda bi, t: (layer, half, 0, 0),
                             pipeline_mode=pl.Buffered(1)),
                pl.BlockSpec((None, None, d_ff, d), lambda bi, t: (layer, half, 0, 0),
                             pipeline_mode=pl.Buffered(1))]
    args = [x, mod, norm_g, w_in, w_out]
    if mix:
        in_specs += [tok(D_FOURIER), tok(D_ATT), tok(D_MLSTM), tok(D_MLSTM), tok(D_MLSTM),
                     pl.BlockSpec((None, d, d), lambda bi, t: (layer, 0, 0), pipeline_mode=pl.Buffered(1))]
        args += list(mixer) + [w_o]
    if final:
        in_specs.append(pl.BlockSpec((1, d), lambda bi, t: (0, 0)))
        args.append(g_final.reshape(1, d))
    return pl.pallas_call(
        functools.partial(_ffn_kernel, sub=sub, d_ff=d_ff, mix=mix, final=final),
        grid=(b, n // tm),
        in_specs=in_specs,
        out_specs=tok(d),
        out_shape=jax.ShapeDtypeStruct((b, n, d), F32),
        compiler_params=_cparams(("parallel", "parallel")),
        name="ffn_mix" if mix else "ffn",
    )(*args)


def _rope128(t, cos, sin_signed, low16):
    fwd = pltpu.roll(t, LANES - 16, axis=1)
    bwd = pltpu.roll(t, 16, axis=1)
    return t * cos + jnp.where(low16, fwd, bwd) * sin_signed


def _inproj_kernel(x_ref, xp_ref, xn_ref, mod_ref, g_ref, w_ref, wgt_ref, gbr_ref, cw_ref,
                   cos_ref, sin_ref,
                   p_ref, qt_ref, k_ref, vt_ref, mk_ref, mqt_ref, mvt_ref, og_ref, gr_ref, *, rope):
    t = pl.program_id(1)
    nt = pl.num_programs(1)
    tm = x_ref.shape[0]
    shift = mod_ref[3:4, :]
    scale = mod_ref[4:5, :]
    g = g_ref[1:2, :]
    h = _rms_mod(x_ref[...], g, shift, scale).astype(BF16)
    u = _dot(h, w_ref[...])

    p_ref[...] = u[:, C_P:C_Q].astype(BF16)

    lane = lax.broadcasted_iota(jnp.int32, (1, LANES), 1)
    low16 = (lane & 16) == 0
    qscale = HEAD_DIM ** -0.5
    if rope:
        cos = cos_ref[...]
        sin = sin_ref[...]
    for cb in range(D_ATT // LANES):
        qb = u[:, C_Q + cb * LANES:C_Q + (cb + 1) * LANES]
        if rope:
            qb = _rope128(qb, cos, sin, low16)
        qt_ref[cb * LANES:(cb + 1) * LANES, :] = jnp.transpose(qb * qscale).astype(BF16)
    kb = u[:, C_K:C_V]
    if rope:
        kb = _rope128(kb, cos, sin, low16)
    k_ref[...] = kb.astype(BF16)
    vt_ref[...] = jnp.transpose(u[:, C_V:C_MQ]).astype(BF16)

    uqk = u[:, C_MQ:C_MV]
    xh = jnp.concatenate([xp_ref[...], xn_ref[...]], axis=0)
    hh = _rms_mod(xh, g, shift, scale).astype(BF16)
    uh = _dot(hh, w_ref[:, C_MQ:C_MV])
    prev_row = jnp.where(t > 0, uh[SUBLANES - 1:SUBLANES, :], 0.0)
    next_row = jnp.where(t < nt - 1, uh[SUBLANES:SUBLANES + 1, :], 0.0)
    row = lax.broadcasted_iota(jnp.int32, (tm, 1), 0)
    u_prev = jnp.where(row == 0, prev_row, pltpu.roll(uqk, 1, axis=0))
    u_next = jnp.where(row == tm - 1, next_row, pltpu.roll(uqk, tm - 1, axis=0))
    cv = u_prev * cw_ref[0:1, :] + uqk * cw_ref[1:2, :] + u_next * cw_ref[2:3, :]
    qk = _silu(cv)
    mqt_ref[...] = jnp.transpose(qk[:, :D_MLSTM]).astype(BF16)
    mk_ref[...] = (qk[:, D_MLSTM:] * (MLSTM_DIM ** -0.5)).astype(BF16)
    mvt_ref[...] = jnp.transpose(u[:, C_MV:C_MO]).astype(BF16)
    og_ref[...] = jax.nn.sigmoid(u[:, C_MO:C_END]).astype(BF16)

    half = N_GATES // 2
    zr = _dot_nt(wgt_ref[...], h) + gbr_ref[...]
    rowg = lax.broadcasted_iota(jnp.int32, (N_GATES, 1), 0)
    gr_ref[...] = jnp.where(rowg >= half, _log_sigmoid(zr), zr)


def _inproj(x, layer, mod, mod_row, norm_g, w_all, wg_t, gb_row, conv_w, cos128, sin128, rope):
    b, n, d = x.shape
    tm = min(TM, n)
    nt = n // tm
    r8 = tm // SUBLANES
    n8 = n // SUBLANES
    lay = lambda *blk: pl.BlockSpec((None,) + blk, lambda bi, t: (layer,) + (0,) * len(blk))
    tok = lambda w: pl.BlockSpec((None, tm, w), lambda bi, t: (bi, t, 0))
    trn = lambda rows: pl.BlockSpec((None, rows, tm), lambda bi, t: (bi, 0, t))
    in_specs = [tok(d),
                pl.BlockSpec((None, SUBLANES, d), lambda bi, t: (bi, jnp.maximum(t * r8 - 1, 0), 0)),
                pl.BlockSpec((None, SUBLANES, d), lambda bi, t: (bi, jnp.minimum((t + 1) * r8, n8 - 1), 0)),
                pl.BlockSpec((None, None, N_MOD, d), lambda bi, t: (layer, mod_row(bi), 0, 0)),
                lay(3, d), lay(d, C_END), lay(N_GATES, d), lay(N_GATES, 1), lay(3, 2 * D_MLSTM),
                pl.BlockSpec((tm, LANES), lambda bi, t: (t, 0)),
                pl.BlockSpec((tm, LANES), lambda bi, t: (t, 0))]
    out_specs = [tok(2 * D_FOURIER), trn(D_ATT), tok(D_KV), trn(D_KV),
                 tok(D_MLSTM), trn(D_MLSTM), trn(D_MLSTM), tok(D_MLSTM), trn(N_GATES)]
    shp = lambda w, dt: jax.ShapeDtypeStruct((b, n, w), dt)
    shpt = lambda rows, dt: jax.ShapeDtypeStruct((b, rows, n), dt)
    out_shape = [shp(2 * D_FOURIER, BF16), shpt(D_ATT, BF16), shp(D_KV, BF16), shpt(D_KV, BF16),
                 shp(D_MLSTM, BF16), shpt(D_MLSTM, BF16), shpt(D_MLSTM, BF16), shp(D_MLSTM, BF16),
                 shpt(N_GATES, F32)]
    return pl.pallas_call(
        functools.partial(_inproj_kernel, rope=rope),
        grid=(b, nt),
        in_specs=in_specs,
        out_specs=out_specs,
        out_shape=out_shape,
        compiler_params=_cparams(("parallel", "parallel")),
        name="inproj",
    )(x, x, x, mod, norm_g, w_all, wg_t, gb_row, conv_w, cos128, sin128)


def _dft_kernel(cn_ref, sn_ref, p_ref, y_ref):
    y = _dot(cn_ref[...], p_ref[:, :D_FOURIER]) + _dot(sn_ref[...], p_ref[:, D_FOURIER:])
    y_ref[...] = y.astype(BF16)


def _dft(cn, sn, p):
    b, n, _ = p.shape
    tk = min(TK_DFT, n)
    return pl.pallas_call(
        _dft_kernel,
        grid=(n // tk, b),
        in_specs=[pl.BlockSpec((tk, n), lambda i, bi: (i, 0)),
                  pl.BlockSpec((tk, n), lambda i, bi: (i, 0)),
                  pl.BlockSpec((None, n, 2 * D_FOURIER), lambda i, bi: (bi, 0, 0))],
        out_specs=pl.BlockSpec((None, tk, D_FOURIER), lambda i, bi: (bi, i, 0)),
        out_shape=jax.ShapeDtypeStruct((b, n, D_FOURIER), BF16),
        compiler_params=_cparams(("parallel", "parallel")),
        name="dft",
    )(cn, sn, p)


def _dft_gen_kernel(ca_ref, sa_ref, cb_ref, sb_ref, p_ref, y_ref, cn_scr, sn_scr):
    @pl.when(pl.program_id(1) == 0)
    def _():
        cb = cb_ref[...]
        sb = sb_ref[...]
        for r in range(ca_ref.shape[0]):
            ca = ca_ref[r:r + 1, :]
            sa = sa_ref[r:r + 1, :]
            cn_scr[r * DFT_K0:(r + 1) * DFT_K0, :] = (ca * cb - sa * sb).astype(BF16)
            sn_scr[r * DFT_K0:(r + 1) * DFT_K0, :] = (sa * cb + ca * sb).astype(BF16)

    y = _dot(cn_scr[...], p_ref[:, :D_FOURIER]) + _dot(sn_scr[...], p_ref[:, D_FOURIER:])
    y_ref[...] = y.astype(BF16)


def _dft_gen(tables, p):
    ca, sa, cb, sb = tables
    b, n, _ = p.shape
    tk = TK_DFT
    rows = tk // DFT_K0
    return pl.pallas_call(
        _dft_gen_kernel,
        grid=(n // tk, b),
        in_specs=[pl.BlockSpec((rows, n), lambda i, bi: (i, 0)),
                  pl.BlockSpec((rows, n), lambda i, bi: (i, 0)),
                  pl.BlockSpec((DFT_K0, n), lambda i, bi: (0, 0)),
                  pl.BlockSpec((DFT_K0, n), lambda i, bi: (0, 0)),
                  pl.BlockSpec((None, n, 2 * D_FOURIER), lambda i, bi: (bi, 0, 0))],
        out_specs=pl.BlockSpec((None, tk, D_FOURIER), lambda i, bi: (bi, i, 0)),
        out_shape=jax.ShapeDtypeStruct((b, n, D_FOURIER), BF16),
        scratch_shapes=[pltpu.VMEM((tk, n), BF16), pltpu.VMEM((tk, n), BF16)],
        compiler_params=_cparams(("parallel", "arbitrary")),
        name="dft_gen",
    )(ca, sa, cb, sb, p)


def _attn_block(qt_blk, kall, vtall, biases, sinks, cl):
    blk = ATT_BLOCK
    group = ATT_HEADS // ATT_KV_HEADS
    gw = group * blk
    zq = jnp.zeros((HEAD_DIM, gw), BF16)
    cols = []
    for kv in range(ATT_KV_HEADS):
        r0 = kv * group * HEAD_DIM
        qblk = jnp.concatenate([qt_blk[r0 + i * HEAD_DIM:r0 + (i + 1) * HEAD_DIM, :] for i in range(group)], axis=1)
        qt_g = jnp.concatenate([qblk, zq] if kv == 0 else [zq, qblk], axis=0)
        s = _dot(kall, qt_g)
        sink = sinks[kv]
        if biases is not None:
            bias_p, bias_n = biases
            segs = [s[:cl], s[cl:cl + blk] + bias_p, s[cl + blk:cl + 2 * blk], s[cl + 2 * blk:] + bias_n]
        else:
            segs = [s]
        m = sink
        for sg in segs:
            m = jnp.maximum(m, jnp.max(sg, axis=0, keepdims=True))
        den = jnp.exp(sink - m)
        ps = []
        for sg in segs:
            p = jnp.exp(sg - m)
            den = den + jnp.sum(p, axis=0, keepdims=True)
            ps.append(p.astype(BF16))
        p_all = jnp.concatenate(ps, axis=0) if len(ps) > 1 else ps[0]
        vt_g = vtall[kv * HEAD_DIM:(kv + 1) * HEAD_DIM, :]
        ot = _dot(vt_g, p_all) * (1.0 / den)
        for cb in range(group // 2):
            pair = jnp.concatenate([ot[:, (2 * cb) * blk:(2 * cb + 1) * blk],
                                    ot[:, (2 * cb + 1) * blk:(2 * cb + 2) * blk]], axis=0)
            cols.append(jnp.transpose(pair).astype(BF16))
    return cols


def _attn_kernel(sink_ref, qt_ref, *rest, layer, local):
    if local:
        kp_ref, kc_ref, kn_ref, vp_ref, vc_ref, vn_ref, kx_ref, vxt_ref, o_ref = rest
    else:
        kx_ref, vxt_ref, o_ref = rest
    blk = ATT_BLOCK
    group = ATT_HEADS // ATT_KV_HEADS
    gw = group * blk
    cl = kx_ref.shape[0]
    nq = qt_ref.shape[1] // blk
    lane_head = lax.broadcasted_iota(jnp.int32, (1, gw), 1) // blk
    sinks = []
    for kv in range(ATT_KV_HEADS):
        sink = jnp.zeros((1, gw), F32)
        for i in range(group):
            sink = jnp.where(lane_head == i, sink_ref[layer, kv * group + i], sink)
        sinks.append(sink)
    kx = kx_ref[...]
    vxt = vxt_ref[...]
    if local:
        j = pl.program_id(1)
        nb = pl.num_programs(1) * nq
        ks = lax.broadcasted_iota(jnp.int32, (blk, blk), 0)
        qi = lax.broadcasted_iota(jnp.int32, (blk, blk), 1)
        in_p = jnp.abs(ks - blk - qi) <= WINDOW
        in_n = jnp.abs(ks + blk - qi) <= WINDOW
        kloc = jnp.concatenate([kp_ref[...], kc_ref[...], kn_ref[...]], axis=0)
        vloc = jnp.concatenate([vp_ref[...], vc_ref[...], vn_ref[...]], axis=1)
    for i in range(nq):
        qt_blk = qt_ref[:, i * blk:(i + 1) * blk]
        if local:
            jb = j * nq + i
            bias_p = jnp.where(in_p & (jb > 0), 0.0, NEG)
            bias_n = jnp.where(in_n & (jb < nb - 1), 0.0, NEG)
            biases = (jnp.concatenate([bias_p] * group, axis=1), jnp.concatenate([bias_n] * group, axis=1))
            kall = jnp.concatenate([kx, kloc[i * blk:(i + 3) * blk, :]], axis=0)
            vtall = jnp.concatenate([vxt, vloc[:, i * blk:(i + 3) * blk]], axis=1)
        else:
            biases = None
            kall = kx
            vtall = vxt
        cols = _attn_block(qt_blk, kall, vtall, biases, sinks, cl)
        for c, col in enumerate(cols):
            o_ref[i * blk:(i + 1) * blk, c * LANES:(c + 1) * LANES] = col


def _attention(sink, layer, qt, k, vt, kx, vxt, local):
    b, _, n = qt.shape
    cl = kx.shape[1]
    blk = ATT_BLOCK
    tq = min(TQ_ATT, n)
    nq = tq // blk
    nb = n // blk
    in_specs = [pl.BlockSpec(memory_space=pltpu.SMEM),
                pl.BlockSpec((None, D_ATT, tq), lambda bi, j: (bi, 0, j))]
    args = [sink, qt]
    if local:
        pj = lambda j: jnp.maximum(j * nq - 1, 0)
        nj = lambda j: jnp.minimum((j + 1) * nq, nb - 1)
        in_specs += [pl.BlockSpec((None, blk, D_KV), lambda bi, j: (bi, pj(j), 0)),
                     pl.BlockSpec((None, tq, D_KV), lambda bi, j: (bi, j, 0)),
                     pl.BlockSpec((None, blk, D_KV), lambda bi, j: (bi, nj(j), 0)),
                     pl.BlockSpec((None, D_KV, blk), lambda bi, j: (bi, 0, pj(j))),
                     pl.BlockSpec((None, D_KV, tq), lambda bi, j: (bi, 0, j)),
                     pl.BlockSpec((None, D_KV, blk), lambda bi, j: (bi, 0, nj(j)))]
        args += [k, k, k, vt, vt, vt]
    in_specs += [pl.BlockSpec((None, cl, D_KV), lambda bi, j: (bi, 0, 0)),
                 pl.BlockSpec((None, D_KV, cl), lambda bi, j: (bi, 0, 0))]
    args += [kx, vxt]
    return pl.pallas_call(
        functools.partial(_attn_kernel, layer=layer, local=local),
        grid=(b, n // tq),
        in_specs=in_specs,
        out_specs=pl.BlockSpec((None, tq, D_ATT), lambda bi, j: (bi, j, 0)),
        out_shape=jax.ShapeDtypeStruct((b, n, D_ATT), BF16),
        compiler_params=_cparams(("parallel", "parallel")),
        name="attention",
    )(*args)


def _mlstm_dir(d, k_ref, qt_ref, vt_ref, gr_ref, h_ref, c_scr, n_scr, m_scr, consts):
    L = MLSTM_L
    H = MLSTM_HEADS
    DH = MLSTM_DIM
    tri, valid_t, lane_h, row_h, blockdiag, nmask = consts
    k = k_ref[...]
    qt = qt_ref[...]
    vt = vt_ref[...]
    gr = gr_ref[...]
    cum = _dot_split3(gr, tri)
    r0 = d * H
    li8 = gr[0:2 * H, :]
    a8 = cum[2 * H:4 * H, :]
    a_tot8 = a8[:, L - 1:L] if d == 0 else a8[:, 0:1]
    m_in8 = m_scr[:, 0:1]
    g_prev8 = a8 + m_in8
    b8 = li8 - a8
    b_t = jnp.transpose(jnp.concatenate([b8, jnp.zeros((L - 2 * H, L), F32)], axis=0))

    c_old = c_scr[d]
    n_old = n_scr[d]
    inter = _dot(jnp.concatenate([c_old, n_old], axis=0).astype(BF16), qt)

    kst = jnp.concatenate([jnp.where(lane_h == hd, k, jnp.zeros_like(k)) for hd in range(H)], axis=0)
    s_t = _dot(kst, qt)
    p_parts, m_rows, wp_rows, den_rows = [], [], [], []
    for hd in range(H):
        r = r0 + hd
        d_t = jnp.where(valid_t, b_t[:, r:r + 1] + a8[r:r + 1, :], NEG)
        g_prev = g_prev8[r:r + 1, :]
        m_t = jnp.maximum(g_prev, jnp.max(d_t, axis=0, keepdims=True))
        p_t = s_t[hd * L:(hd + 1) * L, :] * jnp.exp(d_t - m_t)
        w_prev = jnp.exp(g_prev - m_t)
        den = jnp.sum(p_t, axis=0, keepdims=True) + w_prev * inter[D_MLSTM + r:D_MLSTM + r + 1, :]
        p_parts.append(p_t.astype(BF16))
        m_rows.append(m_t)
        wp_rows.append(w_prev)
        den_rows.append(den)
    vbd = jnp.concatenate([jnp.where(row_h == hd, vt, jnp.zeros_like(vt)) for hd in range(H)], axis=1)
    num_t = _dot(vbd, jnp.concatenate(p_parts, axis=0))
    outs = []
    for hd in range(H):
        inv = 1.0 / jnp.maximum(jnp.abs(den_rows[hd]), jnp.exp(-m_rows[hd]))
        sl = slice(hd * DH, (hd + 1) * DH)
        outs.append((num_t[sl, :] + wp_rows[hd] * inter[sl, :]) * inv)
    h_ref[...] = jnp.transpose(jnp.concatenate(outs, axis=0))

    dl8 = a_tot8 - a8 + li8
    m_new8 = jnp.maximum(a_tot8 + m_in8, jnp.max(dl8, axis=1, keepdims=True))
    w8 = jnp.exp(dl8 - m_new8)
    decay8 = jnp.exp(a_tot8 + m_in8 - m_new8)
    wexp = jnp.concatenate([jnp.broadcast_to(w8[r0 + hd:r0 + hd + 1, :], (DH, L)) for hd in range(H)], axis=0)
    dcol = jnp.concatenate([jnp.broadcast_to(decay8[r0 + hd:r0 + hd + 1, :], (DH, 1)) for hd in range(H)], axis=0)
    upd = _dot((vt.astype(F32) * wexp).astype(BF16), k)
    c_scr[d] = dcol * c_old + jnp.where(blockdiag, upd, 0.0)
    n_upd = _dot(w8.astype(BF16), k)
    n_scr[d] = jnp.where(nmask[d], decay8 * n_old + n_upd, 0.0)
    m_scr[r0:r0 + H, :] = jnp.broadcast_to(m_new8[r0:r0 + H, :], (H, LANES))


def _mlstm_kernel(kf_ref, qtf_ref, vtf_ref, grf_ref, kb_ref, qtb_ref, vtb_ref, grb_ref,
                  c0_ref, n0_ref, m0_ref, hf_ref, hb_ref, cfin_ref, nfin_ref, mfin_ref, c_scr, n_scr, m_scr):
    c = pl.program_id(0)
    nc = pl.num_programs(0)
    L = MLSTM_L
    H = MLSTM_HEADS
    nb = kf_ref.shape[0]

    @pl.when(c == 0)
    def _():
        c_scr[...] = c0_ref[...]
        n_scr[...] = n0_ref[...]
        m_scr[...] = m0_ref[...]

    ri = lax.broadcasted_iota(jnp.int32, (L, L), 0)
    ci = lax.broadcasted_iota(jnp.int32, (L, L), 1)
    upper = ci >= ri
    lower = ci <= ri
    triu = jnp.where(upper, 1.0, 0.0).astype(BF16)
    tril = jnp.where(lower, 1.0, 0.0).astype(BF16)
    lane_h = lax.broadcasted_iota(jnp.int32, (1, D_MLSTM), 1) // MLSTM_DIM
    row_h = lax.broadcasted_iota(jnp.int32, (D_MLSTM, 1), 0) // MLSTM_DIM
    blockdiag = row_h == lane_h
    row8 = lax.broadcasted_iota(jnp.int32, (2 * H, 1), 0)
    nmask = [(row8 - d * H) == lane_h for d in range(2)]
    for bi in range(nb):
        st = (c_scr.at[bi], n_scr.at[bi], m_scr.at[bi])
        _mlstm_dir(0, kf_ref.at[bi], qtf_ref.at[bi], vtf_ref.at[bi], grf_ref.at[bi], hf_ref.at[bi], *st,
                   (triu, upper, lane_h, row_h, blockdiag, nmask))
        _mlstm_dir(1, kb_ref.at[bi], qtb_ref.at[bi], vtb_ref.at[bi], grb_ref.at[bi], hb_ref.at[bi], *st,
                   (tril, lower, lane_h, row_h, blockdiag, nmask))

    @pl.when(c == nc - 1)
    def _():
        cfin_ref[...] = c_scr[...]
        nfin_ref[...] = n_scr[...]
        mfin_ref[...] = m_scr[...]


def _mlstm(mk, mqt, mvt, grow, state):
    b, n, _ = mk.shape
    L = MLSTM_L
    nc = n // L
    nh = 2 * MLSTM_HEADS
    tok = lambda cm: pl.BlockSpec((b, L, D_MLSTM), lambda c: (0, cm(c), 0))
    tr = lambda rows, cm: pl.BlockSpec((b, rows, L), lambda c: (0, 0, cm(c)))
    fw = lambda c: c
    bw = lambda c: nc - 1 - c
    cspec = pl.BlockSpec((b, 2, D_MLSTM, D_MLSTM), lambda c: (0, 0, 0, 0))
    nspec = pl.BlockSpec((b, 2, nh, D_MLSTM), lambda c: (0, 0, 0, 0))
    mspec = pl.BlockSpec((b, nh, LANES), lambda c: (0, 0, 0))
    return pl.pallas_call(
        _mlstm_kernel,
        grid=(nc,),
        in_specs=[tok(fw), tr(D_MLSTM, fw), tr(D_MLSTM, fw), tr(N_GATES, fw),
                  tok(bw), tr(D_MLSTM, bw), tr(D_MLSTM, bw), tr(N_GATES, bw), cspec, nspec, mspec],
        out_specs=[tok(fw), tok(bw), cspec, nspec, mspec],
        out_shape=[jax.ShapeDtypeStruct((b, n, D_MLSTM), F32), jax.ShapeDtypeStruct((b, n, D_MLSTM), F32),
                   jax.ShapeDtypeStruct((b, 2, D_MLSTM, D_MLSTM), F32),
                   jax.ShapeDtypeStruct((b, 2, nh, D_MLSTM), F32),
                   jax.ShapeDtypeStruct((b, nh, LANES), F32)],
        scratch_shapes=[pltpu.VMEM((b, 2, D_MLSTM, D_MLSTM), F32), pltpu.VMEM((b, 2, nh, D_MLSTM), F32),
                        pltpu.VMEM((b, nh, LANES), F32)],
        compiler_params=_cparams(("arbitrary",)),
        name="mlstm",
    )(mk, mqt, mvt, grow, mk, mqt, mvt, grow, *state)


def _rope_tables(n):
    rows = n // GRID_W
    row = jnp.broadcast_to(jnp.arange(rows, dtype=F32)[:, None], (rows, GRID_W)).reshape(n)
    colp = jnp.broadcast_to(jnp.arange(GRID_W, dtype=F32)[None, :], (rows, GRID_W)).reshape(n)
    nf = HEAD_DIM // 4
    inv = ROPE_BASE ** (-jnp.arange(nf, dtype=F32) / nf)
    ar = row[:, None] * inv
    ac = colp[:, None] * inv
    ang = jnp.concatenate([ar, ar, ac, ac], axis=-1)
    cos = jnp.cos(ang)
    sin = jnp.sin(ang)
    sign = jnp.where((jnp.arange(HEAD_DIM) & 16) == 0, -1.0, 1.0).astype(F32)
    reps = LANES // HEAD_DIM
    return jnp.tile(cos, (1, reps)), jnp.tile(sin * sign, (1, reps))


def _dft_factor_tables(n):
    nk1 = n // DFT_K0
    j = jnp.arange(n, dtype=jnp.int32)
    k1 = jnp.arange(nk1, dtype=jnp.int32) * DFT_K0
    k0 = jnp.arange(DFT_K0, dtype=jnp.int32)
    w = 2.0 * math.pi / n
    ang_a = ((k1[:, None] * j[None, :]) % n).astype(F32) * w
    ang_b = ((k0[:, None] * j[None, :]) % n).astype(F32) * w
    return jnp.cos(ang_a), jnp.sin(ang_a), jnp.cos(ang_b), jnp.sin(ang_b)


def _dft_tables(n):
    ca, sa, cb, sb = _dft_factor_tables(n)
    cn = (ca[:, None, :] * cb[None] - sa[:, None, :] * sb[None]).reshape(n, n)
    sn = (sa[:, None, :] * cb[None] + ca[:, None, :] * sb[None]).reshape(n, n)
    return cn.astype(BF16), sn.astype(BF16)


def _blockdiag(blocks):
    g, c = blocks.shape[-3], blocks.shape[-1]
    eye = jnp.eye(g, dtype=blocks.dtype)
    out = jnp.einsum('...gce,gh->...gche', blocks, eye)
    return out.reshape(blocks.shape[:-3] + (g * c, g * c))


def _channel_dft_blockdiag():
    e = jnp.arange(FOURIER_CH, dtype=jnp.int32)
    ang = ((e[:, None] * e[None, :]) % FOURIER_CH).astype(F32) * (2.0 * math.pi / FOURIER_CH)
    reps = (FOURIER_GROUPS, 1, 1)
    return _blockdiag(jnp.tile(jnp.cos(ang)[None], reps)), _blockdiag(jnp.tile(-jnp.sin(ang)[None], reps))


def kernel(x, c, ctx, c_ctx, w_ada, b_ada, norm_g, w_ffn_in, w_ffn_out, w_in, w_out, w_fourier, attn_sink,
           conv_qk, b_gate_i, b_gate_f, g_final):
    b, n, d = x.shape
    cl = ctx.shape[1]
    depth = w_ada.shape[0]
    nh = 2 * MLSTM_HEADS

    rows = -(-(b + 1) // SUBLANES) * SUBLANES
    c_all = jnp.zeros((rows, d), F32).at[:b].set(c).at[b].set(c_ctx)
    mod = _modulation(c_all, w_ada, b_ada).reshape(depth, rows, N_MOD, d)
    x_row = lambda bi: bi
    c_row = lambda bi: b

    offs = [0]
    for s in (D_FOURIER, D_ATT, D_KV, D_KV, D_MLSTM, D_MLSTM, D_MLSTM, D_MLSTM, nh, nh):
        offs.append(offs[-1] + s)
    w_a = w_in[:, :, offs[0]:offs[1]]
    w_gate = w_in[:, :, offs[8]:offs[10]]
    cc_bd, sc_bd = _channel_dft_blockdiag()
    w_p = _fourier_fold(w_a, _blockdiag(w_fourier), cc_bd, sc_bd, 1.0 / math.sqrt(n * FOURIER_CH))
    w_p_ctx = w_p * math.sqrt(n / cl)
    w_mid = w_in[:, :, offs[1]:offs[8]]
    w_all = jnp.concatenate([w_p, w_mid], axis=-1).astype(BF16)
    w_all_ctx = jnp.concatenate([w_p_ctx, w_mid], axis=-1).astype(BF16)
    wg_t = jnp.swapaxes(w_gate, 1, 2).astype(BF16)
    gb_row = jnp.concatenate([b_gate_i.reshape(depth, nh), b_gate_f.reshape(depth, nh)], axis=-1)[:, :, None]
    wf_in = w_ffn_in.astype(BF16)
    wf_out = w_ffn_out.astype(BF16)
    wo = w_out.astype(BF16)

    cos128, sin128 = _rope_tables(n)
    dft_x = _dft_factor_tables(n)
    cnc, snc = _dft_tables(cl)
    zeros_c = jnp.zeros((cl, LANES), F32)
    state0 = (jnp.zeros((b, 2, D_MLSTM, D_MLSTM), F32), jnp.zeros((b, 2, nh, D_MLSTM), F32),
              jnp.zeros((b, nh, LANES), F32))

    xc = ctx
    for l in range(depth):
        last = l == depth - 1
        x = _ffn(x, l, 0, 0, mod, x_row, norm_g, wf_in, wf_out)
        xc = _ffn(xc, l, 0, 0, mod, c_row, norm_g, wf_in, wf_out)

        px, qtx, kx, vtx, mkx, mqtx, mvtx, ogx, grx = _inproj(
            x, l, mod, x_row, norm_g, w_all, wg_t, gb_row, conv_qk, cos128, sin128, True)
        pc, qtc, kc, vtc, mkc, mqtc, mvtc, ogc, grc = _inproj(
            xc, l, mod, c_row, norm_g, w_all_ctx, wg_t, gb_row, conv_qk, zeros_c, zeros_c, False)

        hcf, hcb, *state_c = _mlstm(mkc, mqtc, mvtc, grc, state0)
        hxf, hxb, *_ = _mlstm(mkx, mqtx, mvtx, grx, state_c)
        att_x = _attention(attn_sink, l, qtx, kx, vtx, kc, vtc, True)
        yf_x = _dft_gen(dft_x, px) if n % TK_DFT == 0 else _dft(*_dft_tables(n), px)
        x = _ffn(x, l, 1, 2, mod, x_row, norm_g, wf_in, wf_out, mixer=(yf_x, att_x, hxf, hxb, ogx), w_o=wo,
                 g_final=g_final if last else None)

        if not last:
            att_c = _attention(attn_sink, l, qtc, None, None, kc, vtc, False)
            yf_c = _dft(cnc, snc, pc)
            xc = _ffn(xc, l, 1, 2, mod, c_row, norm_g, wf_in, wf_out, mixer=(yf_c, att_c, hcf, hcb, ogc), w_o=wo)
    return x
```

```python
import functools
import math

import jax
import jax.numpy as jnp
from jax import lax
from jax.experimental import pallas as pl
from jax.experimental.pallas import tpu as pltpu

F32 = jnp.float32
BF16 = jnp.bfloat16
HI = lax.Precision.HIGHEST

GRID_W = 64
EPS = 1e-6
NEG = -1e30
LOG2E = math.log2(math.e)
FOURIER_GROUPS = 4
FOURIER_CH = 64
D_FOURIER = FOURIER_GROUPS * FOURIER_CH
HEAD_DIM = 64
ATT_HEADS = 8
ATT_KV_HEADS = 2
D_ATT = ATT_HEADS * HEAD_DIM
D_KV = ATT_KV_HEADS * HEAD_DIM
ATT_BLOCK = 128
WINDOW = 128
ROPE_BASE = 10000.0
MLSTM_HEADS = 4
MLSTM_DIM = 64
D_MLSTM = MLSTM_HEADS * MLSTM_DIM
N_MOD = 9
N_GATES = 4 * MLSTM_HEADS

LANES = 128
SUBLANES = 8
VMEM_LIMIT = 56 * 1024 * 1024

TM = 512
TK_DFT = 512
DFT_K0 = 64
TQ_ATT = 512
MLSTM_L = 128

C_P = 0
C_Q = C_P + 2 * D_FOURIER
C_K = C_Q + D_ATT
C_V = C_K + D_KV
C_MQ = C_V + D_KV
C_MK = C_MQ + D_MLSTM
C_MV = C_MK + D_MLSTM
C_MO = C_MV + D_MLSTM
C_END = C_MO + D_MLSTM


def _cparams(sem):
    return pltpu.CompilerParams(dimension_semantics=sem, vmem_limit_bytes=VMEM_LIMIT)


def _dot(a, b):
    return jnp.dot(a, b, preferred_element_type=F32)


def _dot_nt(a, b):
    return lax.dot_general(a, b, (((1,), (1,)), ((), ())), preferred_element_type=F32)


def _dot_hi(a, b):
    return jnp.dot(a, b, preferred_element_type=F32, precision=HI)


def _dot_split3(x, onehot):
    rows = x.shape[0]
    hi = x.astype(BF16)
    r1 = x - hi.astype(F32)
    mid = r1.astype(BF16)
    lo = (r1 - mid.astype(F32)).astype(BF16)
    y = _dot(jnp.concatenate([hi, mid, lo], axis=0), onehot)
    return y[0:rows] + y[rows:2 * rows] + y[2 * rows:3 * rows]


def _rms_mod(xt, g, shift, scale):
    ms = jnp.mean(xt * xt, axis=-1, keepdims=True)
    y = xt * lax.rsqrt(ms + EPS) * g
    return y * (1.0 + scale) + shift


def _silu(t):
    return t * jax.nn.sigmoid(t)


def _log_sigmoid(t):
    return jnp.minimum(t, 0.0) - jnp.log1p(jnp.exp(-jnp.abs(t)))


def _mod_kernel(c_ref, w_ref, b_ref, o_ref):
    sc = _silu(c_ref[...])
    o_ref[...] = _dot(sc.astype(BF16), w_ref[...].astype(BF16)) + b_ref[...]


def _modulation(c_all, w_ada, b_ada):
    depth, d, nd = w_ada.shape
    r = c_all.shape[0]
    tn = 1024
    return pl.pallas_call(
        _mod_kernel,
        grid=(depth, nd // tn),
        in_specs=[pl.BlockSpec((r, d), lambda l, j: (0, 0)),
                  pl.BlockSpec((None, d, tn), lambda l, j: (l, 0, j)),
                  pl.BlockSpec((None, 1, tn), lambda l, j: (l, 0, j))],
        out_specs=pl.BlockSpec((None, r, tn), lambda l, j: (l, 0, j)),
        out_shape=jax.ShapeDtypeStruct((depth, r, nd), F32),
        compiler_params=_cparams(("parallel", "parallel")),
        name="modulation",
    )(c_all, w_ada, b_ada.reshape(depth, 1, nd))


def _fold_kernel(wa_ref, wf_ref, cc_ref, sc_ref, o_ref, *, scale):
    wf = wf_ref[...]
    mc = _dot_hi(cc_ref[...], wf) * scale
    ms = _dot_hi(sc_ref[...], wf) * scale
    wa = wa_ref[...]
    o_ref[:, :D_FOURIER] = _dot_hi(wa, mc)
    o_ref[:, D_FOURIER:] = _dot_hi(wa, ms)


def _fourier_fold(wa, wf_bd, cc_bd, sc_bd, scale):
    depth, d, _ = wa.shape
    return pl.pallas_call(
        functools.partial(_fold_kernel, scale=scale),
        grid=(depth,),
        in_specs=[pl.BlockSpec((None, d, D_FOURIER), lambda l: (l, 0, 0)),
                  pl.BlockSpec((None, D_FOURIER, D_FOURIER), lambda l: (l, 0, 0)),
                  pl.BlockSpec((D_FOURIER, D_FOURIER), lambda l: (0, 0)),
                  pl.BlockSpec((D_FOURIER, D_FOURIER), lambda l: (0, 0))],
        out_specs=pl.BlockSpec((None, d, 2 * D_FOURIER), lambda l: (l, 0, 0)),
        out_shape=jax.ShapeDtypeStruct((depth, d, 2 * D_FOURIER), F32),
        compiler_params=_cparams(("parallel",)),
        name="fourier_fold",
    )(wa, wf_bd, cc_bd, sc_bd)


def _ffn_kernel(*refs, sub, d_ff, mix, final, cast_next):
    x_ref, mod_ref, g_ref, win_ref, wout_ref = refs[:5]
    if cast_next:
        o_ref, nin_o_ref, nout_o_ref = refs[-3:]
        nin_ref, nout_ref = refs[-5:-3]
        nin_o_ref[...] = nin_ref[...].astype(BF16)
        nout_o_ref[...] = nout_ref[...].astype(BF16)
        rest = list(refs[5:-5])
    else:
        o_ref = refs[-1]
        rest = list(refs[5:-1])
    xt = x_ref[...]
    if mix:
        yf_ref, att_ref, hf_ref, hb_ref, og_ref, wo_ref = rest[:6]
        rest = rest[6:]
        mh = (og_ref[...].astype(F32) * (hf_ref[...] + hb_ref[...])).astype(BF16)
        a0, a1 = D_FOURIER, D_FOURIER + D_ATT
        y = (_dot(yf_ref[...], wo_ref[:a0, :]) + _dot(att_ref[...], wo_ref[a0:a1, :])
             + _dot(mh, wo_ref[a1:, :]))
        xt = xt + mod_ref[5:6, :] * y
    shift = mod_ref[3 * sub:3 * sub + 1, :]
    scale = mod_ref[3 * sub + 1:3 * sub + 2, :]
    gate = mod_ref[3 * sub + 2:3 * sub + 3, :]
    h = _rms_mod(xt, g_ref[sub:sub + 1, :], shift, scale).astype(BF16)
    gu = _dot(h, win_ref[...])
    act = (_silu(gu[:, :d_ff]) * gu[:, d_ff:]).astype(BF16)
    y = xt + (0.5 * gate) * _dot(act, wout_ref[...])
    if final:
        gf_ref = rest[0]
        ms = jnp.mean(y * y, axis=-1, keepdims=True)
        y = y * lax.rsqrt(ms + EPS) * gf_ref[...]
    o_ref[...] = y


def _slab_count(rows, steps):
    for nblk in range(steps, 0, -1):
        if rows % nblk == 0 and (rows // nblk) % (2 * SUBLANES) == 0:
            return nblk
    return 1


def _ffn(x, layer, sub, mod, mod_row, norm_g, w_in, w_out, mixer=None, w_o=None, g_final=None, cast_next=None):
    b, n, d = x.shape
    d_ff = w_out.shape[0]
    tm = min(TM, n)
    nt = n // tm
    mix = mixer is not None
    final = g_final is not None
    tok = lambda w: pl.BlockSpec((None, tm, w), lambda bi, t: (bi, t, 0))
    whole = lambda r, c: pl.BlockSpec((r, c), lambda bi, t: (0, 0), pipeline_mode=pl.Buffered(1))
    in_specs = [tok(d),
                pl.BlockSpec((None, None, N_MOD, d), lambda bi, t: (layer, mod_row(bi), 0, 0)),
                pl.BlockSpec((None, 3, d), lambda bi, t: (layer, 0, 0)),
                whole(d, 2 * d_ff), whole(d_ff, d)]
    args = [x, mod, norm_g, w_in, w_out]
    if mix:
        in_specs += [tok(D_FOURIER), tok(D_ATT), tok(D_MLSTM), tok(D_MLSTM), tok(D_MLSTM),
                     pl.BlockSpec((None, d, d), lambda bi, t: (layer, 0, 0), pipeline_mode=pl.Buffered(1))]
        args += list(mixer) + [w_o]
    if final:
        in_specs.append(pl.BlockSpec((1, d), lambda bi, t: (0, 0)))
        args.append(g_final.reshape(1, d))
    out_specs = [tok(d)]
    out_shape = [jax.ShapeDtypeStruct((b, n, d), F32)]
    if cast_next is not None:
        nin, nout, nl, nh_ = cast_next
        steps = b * nt
        for arr, rows, cols in ((nin, d, 2 * d_ff), (nout, d_ff, d)):
            nblk = _slab_count(rows, steps)
            slab = lambda bi, t, nblk=nblk: jnp.minimum(bi * nt + t, nblk - 1)
            in_specs.append(pl.BlockSpec((None, None, rows // nblk, cols),
                                         lambda bi, t, slab=slab: (nl, nh_, slab(bi, t), 0)))
            out_specs.append(pl.BlockSpec((rows // nblk, cols), lambda bi, t, slab=slab: (slab(bi, t), 0)))
            out_shape.append(jax.ShapeDtypeStruct((rows, cols), BF16))
            args.append(arr)
    res = pl.pallas_call(
        functools.partial(_ffn_kernel, sub=sub, d_ff=d_ff, mix=mix, final=final, cast_next=cast_next is not None),
        grid=(b, nt),
        in_specs=in_specs,
        out_specs=out_specs,
        out_shape=out_shape,
        compiler_params=_cparams(("arbitrary", "arbitrary")),
        name="ffn_mix" if mix else "ffn",
    )(*args)
    return res if cast_next is not None else res[0]


def _rope128(t, cos, sin_signed, low16):
    fwd = pltpu.roll(t, LANES - 16, axis=1)
    bwd = pltpu.roll(t, 16, axis=1)
    return t * cos + jnp.where(low16, fwd, bwd) * sin_signed


def _inproj_kernel(x_ref, xp_ref, xn_ref, mod_ref, g_ref, w_ref, wgt_ref, gbr_ref, cw_ref,
                   cos_ref, sin_ref,
                   p_ref, qt_ref, k_ref, vt_ref, mk_ref, mqt_ref, mvt_ref, og_ref, gr_ref, *, rope):
    t = pl.program_id(1)
    nt = pl.num_programs(1)
    tm = x_ref.shape[0]
    shift = mod_ref[3:4, :]
    scale = mod_ref[4:5, :]
    g = g_ref[1:2, :]
    h = _rms_mod(x_ref[...], g, shift, scale).astype(BF16)
    seg = lambda c0, c1: _dot(h, w_ref[:, c0:c1])

    p_ref[...] = seg(C_P, C_Q).astype(BF16)

    lane = lax.broadcasted_iota(jnp.int32, (1, LANES), 1)
    low16 = (lane & 16) == 0
    qscale = HEAD_DIM ** -0.5 * LOG2E
    if rope:
        cos = cos_ref[...]
        sin = sin_ref[...]
    uq = seg(C_Q, C_K)
    for cb in range(D_ATT // LANES):
        qb = uq[:, cb * LANES:(cb + 1) * LANES]
        if rope:
            qb = _rope128(qb, cos, sin, low16)
        qt_ref[cb * LANES:(cb + 1) * LANES, :] = jnp.transpose(qb * qscale).astype(BF16)
    ukv = seg(C_K, C_MQ)
    kb = ukv[:, :D_KV]
    if rope:
        kb = _rope128(kb, cos, sin, low16)
    k_ref[...] = kb.astype(BF16)
    vt_ref[...] = jnp.transpose(ukv[:, D_KV:]).astype(BF16)

    uqk = seg(C_MQ, C_MV)
    xh = jnp.concatenate([xp_ref[...], xn_ref[...]], axis=0)
    hh = _rms_mod(xh, g, shift, scale).astype(BF16)
    uh = _dot(hh, w_ref[:, C_MQ:C_MV])
    prev_row = jnp.where(t > 0, uh[SUBLANES - 1:SUBLANES, :], 0.0)
    next_row = jnp.where(t < nt - 1, uh[SUBLANES:SUBLANES + 1, :], 0.0)
    row = lax.broadcasted_iota(jnp.int32, (tm, 1), 0)
    u_prev = jnp.where(row == 0, prev_row, pltpu.roll(uqk, 1, axis=0))
    u_next = jnp.where(row == tm - 1, next_row, pltpu.roll(uqk, tm - 1, axis=0))
    cv = u_prev * cw_ref[0:1, :] + uqk * cw_ref[1:2, :] + u_next * cw_ref[2:3, :]
    qk = _silu(cv)
    mqt_ref[...] = jnp.transpose(qk[:, :D_MLSTM]).astype(BF16)
    mk_ref[...] = (qk[:, D_MLSTM:] * (MLSTM_DIM ** -0.5)).astype(BF16)
    uvo = seg(C_MV, C_END)
    mvt_ref[...] = jnp.transpose(uvo[:, :D_MLSTM]).astype(BF16)
    og_ref[...] = jax.nn.sigmoid(uvo[:, D_MLSTM:]).astype(BF16)

    half = N_GATES // 2
    zr = _dot_nt(wgt_ref[...], h) + gbr_ref[...]
    rowg = lax.broadcasted_iota(jnp.int32, (N_GATES, 1), 0)
    gr_ref[...] = jnp.where(rowg >= half, _log_sigmoid(zr), zr)


def _inproj(x, layer, mod, mod_row, norm_g, w_all, wg_t, gb_row, conv_w, cos128, sin128, rope):
    b, n, d = x.shape
    tm = min(TM, n)
    nt = n // tm
    r8 = tm // SUBLANES
    n8 = n // SUBLANES
    lay = lambda *blk: pl.BlockSpec((None,) + blk, lambda bi, t: (layer,) + (0,) * len(blk))
    tok = lambda w: pl.BlockSpec((None, tm, w), lambda bi, t: (bi, t, 0))
    trn = lambda rows: pl.BlockSpec((None, rows, tm), lambda bi, t: (bi, 0, t))
    in_specs = [tok(d),
                pl.BlockSpec((None, SUBLANES, d), lambda bi, t: (bi, jnp.maximum(t * r8 - 1, 0), 0)),
                pl.BlockSpec((None, SUBLANES, d), lambda bi, t: (bi, jnp.minimum((t + 1) * r8, n8 - 1), 0)),
                pl.BlockSpec((None, None, N_MOD, d), lambda bi, t: (layer, mod_row(bi), 0, 0)),
                lay(3, d), lay(d, C_END), lay(N_GATES, d), lay(N_GATES, 1), lay(3, 2 * D_MLSTM),
                pl.BlockSpec((tm, LANES), lambda bi, t: (t, 0)),
                pl.BlockSpec((tm, LANES), lambda bi, t: (t, 0))]
    out_specs = [tok(2 * D_FOURIER), trn(D_ATT), tok(D_KV), trn(D_KV),
                 tok(D_MLSTM), trn(D_MLSTM), trn(D_MLSTM), tok(D_MLSTM), trn(N_GATES)]
    shp = lambda w, dt: jax.ShapeDtypeStruct((b, n, w), dt)
    shpt = lambda rows, dt: jax.ShapeDtypeStruct((b, rows, n), dt)
    out_shape = [shp(2 * D_FOURIER, BF16), shpt(D_ATT, BF16), shp(D_KV, BF16), shpt(D_KV, BF16),
                 shp(D_MLSTM, BF16), shpt(D_MLSTM, BF16), shpt(D_MLSTM, BF16), shp(D_MLSTM, BF16),
                 shpt(N_GATES, F32)]
    return pl.pallas_call(
        functools.partial(_inproj_kernel, rope=rope),
        grid=(b, nt),
        in_specs=in_specs,
        out_specs=out_specs,
        out_shape=out_shape,
        compiler_params=_cparams(("parallel", "parallel")),
        name="inproj",
    )(x, x, x, mod, norm_g, w_all, wg_t, gb_row, conv_w, cos128, sin128)


def _dft_kernel(cn_ref, sn_ref, p_ref, y_ref):
    y = _dot(cn_ref[...], p_ref[:, :D_FOURIER]) + _dot(sn_ref[...], p_ref[:, D_FOURIER:])
    y_ref[...] = y.astype(BF16)


def _dft(cn, sn, p):
    b, n, _ = p.shape
    tk = min(TK_DFT, n)
    return pl.pallas_call(
        _dft_kernel,
        grid=(n // tk, b),
        in_specs=[pl.BlockSpec((tk, n), lambda i, bi: (i, 0)),
                  pl.BlockSpec((tk, n), lambda i, bi: (i, 0)),
                  pl.BlockSpec((None, n, 2 * D_FOURIER), lambda i, bi: (bi, 0, 0))],
        out_specs=pl.BlockSpec((None, tk, D_FOURIER), lambda i, bi: (bi, i, 0)),
        out_shape=jax.ShapeDtypeStruct((b, n, D_FOURIER), BF16),
        compiler_params=_cparams(("parallel", "parallel")),
        name="dft",
    )(cn, sn, p)


def _dft_gen_kernel(ca_ref, sa_ref, cb_ref, sb_ref, p_ref, y_ref, cn_scr, sn_scr):
    @pl.when(pl.program_id(1) == 0)
    def _():
        cb = cb_ref[...]
        sb = sb_ref[...]
        for r in range(ca_ref.shape[0]):
            ca = ca_ref[r:r + 1, :]
            sa = sa_ref[r:r + 1, :]
            cn_scr[r * DFT_K0:(r + 1) * DFT_K0, :] = (ca * cb - sa * sb).astype(BF16)
            sn_scr[r * DFT_K0:(r + 1) * DFT_K0, :] = (sa * cb + ca * sb).astype(BF16)

    y = _dot(cn_scr[...], p_ref[:, :D_FOURIER]) + _dot(sn_scr[...], p_ref[:, D_FOURIER:])
    y_ref[...] = y.astype(BF16)


def _dft_gen(tables, p):
    ca, sa, cb, sb = tables
    b, n, _ = p.shape
    tk = TK_DFT
    rows = tk // DFT_K0
    return pl.pallas_call(
        _dft_gen_kernel,
        grid=(n // tk, b),
        in_specs=[pl.BlockSpec((rows, n), lambda i, bi: (i, 0)),
                  pl.BlockSpec((rows, n), lambda i, bi: (i, 0)),
                  pl.BlockSpec((DFT_K0, n), lambda i, bi: (0, 0)),
                  pl.BlockSpec((DFT_K0, n), lambda i, bi: (0, 0)),
                  pl.BlockSpec((None, n, 2 * D_FOURIER), lambda i, bi: (bi, 0, 0))],
        out_specs=pl.BlockSpec((None, tk, D_FOURIER), lambda i, bi: (bi, i, 0)),
        out_shape=jax.ShapeDtypeStruct((b, n, D_FOURIER), BF16),
        scratch_shapes=[pltpu.VMEM((tk, n), BF16), pltpu.VMEM((tk, n), BF16)],
        compiler_params=_cparams(("parallel", "arbitrary")),
        name="dft_gen",
    )(ca, sa, cb, sb, p)


def _attn_block(qt_blk, kall, vtall, biases, sinks, cl):
    blk = ATT_BLOCK
    group = ATT_HEADS // ATT_KV_HEADS
    gw = group * blk
    zq = jnp.zeros((HEAD_DIM, gw), BF16)
    cols = []
    for kv in range(ATT_KV_HEADS):
        r0 = kv * group * HEAD_DIM
        qblk = jnp.concatenate([qt_blk[r0 + i * HEAD_DIM:r0 + (i + 1) * HEAD_DIM, :] for i in range(group)], axis=1)
        qt_g = jnp.concatenate([qblk, zq] if kv == 0 else [zq, qblk], axis=0)
        s = _dot(kall, qt_g)
        sink = sinks[kv]
        if biases is not None:
            bias_p, bias_n = biases
            segs = [s[:cl], s[cl:cl + blk] + bias_p, s[cl + blk:cl + 2 * blk], s[cl + 2 * blk:] + bias_n]
        else:
            segs = [s]
        m = sink
        for sg in segs:
            m = jnp.maximum(m, jnp.max(sg, axis=0, keepdims=True))
        den = jnp.exp2(sink - m)
        ps = []
        for sg in segs:
            p = jnp.exp2(sg - m)
            den = den + jnp.sum(p, axis=0, keepdims=True)
            ps.append(p.astype(BF16))
        p_all = jnp.concatenate(ps, axis=0) if len(ps) > 1 else ps[0]
        vt_g = vtall[kv * HEAD_DIM:(kv + 1) * HEAD_DIM, :]
        ot = _dot(vt_g, p_all) * (1.0 / den)
        for cb in range(group // 2):
            pair = jnp.concatenate([ot[:, (2 * cb) * blk:(2 * cb + 1) * blk],
                                    ot[:, (2 * cb + 1) * blk:(2 * cb + 2) * blk]], axis=0)
            cols.append(jnp.transpose(pair).astype(BF16))
    return cols


def _attn_kernel(sink_ref, qt_ref, *rest, layer, local):
    if local:
        kp_ref, kc_ref, kn_ref, vp_ref, vc_ref, vn_ref, kx_ref, vxt_ref, o_ref = rest
    else:
        kx_ref, vxt_ref, o_ref = rest
    blk = ATT_BLOCK
    group = ATT_HEADS // ATT_KV_HEADS
    gw = group * blk
    cl = kx_ref.shape[0]
    nq = qt_ref.shape[1] // blk
    lane_head = lax.broadcasted_iota(jnp.int32, (1, gw), 1) // blk
    sinks = []
    for kv in range(ATT_KV_HEADS):
        sink = jnp.zeros((1, gw), F32)
        for i in range(group):
            sink = jnp.where(lane_head == i, sink_ref[layer, kv * group + i] * LOG2E, sink)
        sinks.append(sink)
    kx = kx_ref[...]
    vxt = vxt_ref[...]
    if local:
        j = pl.program_id(1)
        nb = pl.num_programs(1) * nq
        ks = lax.broadcasted_iota(jnp.int32, (blk, blk), 0)
        qi = lax.broadcasted_iota(jnp.int32, (blk, blk), 1)
        in_p = jnp.abs(ks - blk - qi) <= WINDOW
        in_n = jnp.abs(ks + blk - qi) <= WINDOW
        kloc = jnp.concatenate([kp_ref[...], kc_ref[...], kn_ref[...]], axis=0)
        vloc = jnp.concatenate([vp_ref[...], vc_ref[...], vn_ref[...]], axis=1)
    for i in range(nq):
        qt_blk = qt_ref[:, i * blk:(i + 1) * blk]
        if local:
            jb = j * nq + i
            bias_p = jnp.where(in_p & (jb > 0), 0.0, NEG)
            bias_n = jnp.where(in_n & (jb < nb - 1), 0.0, NEG)
            biases = (jnp.concatenate([bias_p] * group, axis=1), jnp.concatenate([bias_n] * group, axis=1))
            kall = jnp.concatenate([kx, kloc[i * blk:(i + 3) * blk, :]], axis=0)
            vtall = jnp.concatenate([vxt, vloc[:, i * blk:(i + 3) * blk]], axis=1)
        else:
            biases = None
            kall = kx
            vtall = vxt
        cols = _attn_block(qt_blk, kall, vtall, biases, sinks, cl)
        for c, col in enumerate(cols):
            o_ref[i * blk:(i + 1) * blk, c * LANES:(c + 1) * LANES] = col


def _attention(sink, layer, qt, k, vt, kx, vxt, local):
    b, _, n = qt.shape
    cl = kx.shape[1]
    blk = ATT_BLOCK
    tq = min(TQ_ATT, n)
    nq = tq // blk
    nb = n // blk
    in_specs = [pl.BlockSpec(memory_space=pltpu.SMEM),
                pl.BlockSpec((None, D_ATT, tq), lambda bi, j: (bi, 0, j))]
    args = [sink, qt]
    if local:
        pj = lambda j: jnp.maximum(j * nq - 1, 0)
        nj = lambda j: jnp.minimum((j + 1) * nq, nb - 1)
        in_specs += [pl.BlockSpec((None, blk, D_KV), lambda bi, j: (bi, pj(j), 0)),
                     pl.BlockSpec((None, tq, D_KV), lambda bi, j: (bi, j, 0)),
                     pl.BlockSpec((None, blk, D_KV), lambda bi, j: (bi, nj(j), 0)),
                     pl.BlockSpec((None, D_KV, blk), lambda bi, j: (bi, 0, pj(j))),
                     pl.BlockSpec((None, D_KV, tq), lambda bi, j: (bi, 0, j)),
                     pl.BlockSpec((None, D_KV, blk), lambda bi, j: (bi, 0, nj(j)))]
        args += [k, k, k, vt, vt, vt]
    in_specs += [pl.BlockSpec((None, cl, D_KV), lambda bi, j: (bi, 0, 0)),
                 pl.BlockSpec((None, D_KV, cl), lambda bi, j: (bi, 0, 0))]
    args += [kx, vxt]
    return pl.pallas_call(
        functools.partial(_attn_kernel, layer=layer, local=local),
        grid=(b, n // tq),
        in_specs=in_specs,
        out_specs=pl.BlockSpec((None, tq, D_ATT), lambda bi, j: (bi, j, 0)),
        out_shape=jax.ShapeDtypeStruct((b, n, D_ATT), BF16),
        compiler_params=_cparams(("parallel", "parallel")),
        name="attention",
    )(*args)


def _mlstm_dir(d, k_ref, qt_ref, vt_ref, gr_ref, h_ref, c_scr, n_scr, m_scr, consts):
    L = MLSTM_L
    H = MLSTM_HEADS
    DH = MLSTM_DIM
    tri, valid_t, lane_h, row_h, blockdiag, nmask = consts
    k = k_ref[...]
    qt = qt_ref[...]
    vt = vt_ref[...]
    gr = gr_ref[...]
    cum = _dot_split3(gr, tri)
    r0 = d * H
    li8 = gr[0:2 * H, :]
    a8 = cum[2 * H:4 * H, :]
    a_tot8 = a8[:, L - 1:L] if d == 0 else a8[:, 0:1]
    m_in8 = m_scr[:, 0:1]
    g_prev8 = a8 + m_in8
    b8 = li8 - a8
    b_t = jnp.transpose(jnp.concatenate([b8, jnp.zeros((L - 2 * H, L), F32)], axis=0))

    c_old = c_scr[d]
    n_old = n_scr[d]
    inter = _dot(jnp.concatenate([c_old, n_old], axis=0).astype(BF16), qt)

    kst = jnp.concatenate([jnp.where(lane_h == hd, k, jnp.zeros_like(k)) for hd in range(H)], axis=0)
    s_t = _dot(kst, qt)
    p_parts, m_rows, wp_rows, den_rows = [], [], [], []
    for hd in range(H):
        r = r0 + hd
        d_t = jnp.where(valid_t, b_t[:, r:r + 1] + a8[r:r + 1, :], NEG)
        g_prev = g_prev8[r:r + 1, :]
        m_t = jnp.maximum(g_prev, jnp.max(d_t, axis=0, keepdims=True))
        p_t = s_t[hd * L:(hd + 1) * L, :] * jnp.exp(d_t - m_t)
        w_prev = jnp.exp(g_prev - m_t)
        den = jnp.sum(p_t, axis=0, keepdims=True) + w_prev * inter[D_MLSTM + r:D_MLSTM + r + 1, :]
        p_parts.append(p_t.astype(BF16))
        m_rows.append(m_t)
        wp_rows.append(w_prev)
        den_rows.append(den)
    vbd = jnp.concatenate([jnp.where(row_h == hd, vt, jnp.zeros_like(vt)) for hd in range(H)], axis=1)
    num_t = _dot(vbd, jnp.concatenate(p_parts, axis=0))
    outs = []
    for hd in range(H):
        inv = 1.0 / jnp.maximum(jnp.abs(den_rows[hd]), jnp.exp(-m_rows[hd]))
        sl = slice(hd * DH, (hd + 1) * DH)
        outs.append((num_t[sl, :] + wp_rows[hd] * inter[sl, :]) * inv)
    h_ref[...] = jnp.transpose(jnp.concatenate(outs, axis=0))

    dl8 = a_tot8 - a8 + li8
    m_new8 = jnp.maximum(a_tot8 + m_in8, jnp.max(dl8, axis=1, keepdims=True))
    w8 = jnp.exp(dl8 - m_new8)
    decay8 = jnp.exp(a_tot8 + m_in8 - m_new8)
    wexp = jnp.concatenate([jnp.broadcast_to(w8[r0 + hd:r0 + hd + 1, :], (DH, L)) for hd in range(H)], axis=0)
    dcol = jnp.concatenate([jnp.broadcast_to(decay8[r0 + hd:r0 + hd + 1, :], (DH, 1)) for hd in range(H)], axis=0)
    upd = _dot((vt.astype(F32) * wexp).astype(BF16), k)
    c_scr[d] = dcol * c_old + jnp.where(blockdiag, upd, 0.0)
    n_upd = _dot(w8.astype(BF16), k)
    n_scr[d] = jnp.where(nmask[d], decay8 * n_old + n_upd, 0.0)
    m_scr[r0:r0 + H, :] = jnp.broadcast_to(m_new8[r0:r0 + H, :], (H, LANES))


def _mlstm_kernel(kf_ref, qtf_ref, vtf_ref, grf_ref, kb_ref, qtb_ref, vtb_ref, grb_ref,
                  c0_ref, n0_ref, m0_ref, hf_ref, hb_ref, cfin_ref, nfin_ref, mfin_ref, c_scr, n_scr, m_scr):
    c = pl.program_id(0)
    nc = pl.num_programs(0)
    L = MLSTM_L
    H = MLSTM_HEADS
    nb = kf_ref.shape[0]

    @pl.when(c == 0)
    def _():
        c_scr[...] = c0_ref[...]
        n_scr[...] = n0_ref[...]
        m_scr[...] = m0_ref[...]

    ri = lax.broadcasted_iota(jnp.int32, (L, L), 0)
    ci = lax.broadcasted_iota(jnp.int32, (L, L), 1)
    upper = ci >= ri
    lower = ci <= ri
    triu = jnp.where(upper, 1.0, 0.0).astype(BF16)
    tril = jnp.where(lower, 1.0, 0.0).astype(BF16)
    lane_h = lax.broadcasted_iota(jnp.int32, (1, D_MLSTM), 1) // MLSTM_DIM
    row_h = lax.broadcasted_iota(jnp.int32, (D_MLSTM, 1), 0) // MLSTM_DIM
    blockdiag = row_h == lane_h
    row8 = lax.broadcasted_iota(jnp.int32, (2 * H, 1), 0)
    nmask = [(row8 - d * H) == lane_h for d in range(2)]
    for bi in range(nb):
        st = (c_scr.at[bi], n_scr.at[bi], m_scr.at[bi])
        _mlstm_dir(0, kf_ref.at[bi], qtf_ref.at[bi], vtf_ref.at[bi], grf_ref.at[bi], hf_ref.at[bi], *st,
                   (triu, upper, lane_h, row_h, blockdiag, nmask))
        _mlstm_dir(1, kb_ref.at[bi], qtb_ref.at[bi], vtb_ref.at[bi], grb_ref.at[bi], hb_ref.at[bi], *st,
                   (tril, lower, lane_h, row_h, blockdiag, nmask))

    @pl.when(c == nc - 1)
    def _():
        cfin_ref[...] = c_scr[...]
        nfin_ref[...] = n_scr[...]
        mfin_ref[...] = m_scr[...]


def _mlstm(mk, mqt, mvt, grow, state):
    b, n, _ = mk.shape
    L = MLSTM_L
    nc = n // L
    nh = 2 * MLSTM_HEADS
    tok = lambda cm: pl.BlockSpec((b, L, D_MLSTM), lambda c: (0, cm(c), 0))
    tr = lambda rows, cm: pl.BlockSpec((b, rows, L), lambda c: (0, 0, cm(c)))
    fw = lambda c: c
    bw = lambda c: nc - 1 - c
    cspec = pl.BlockSpec((b, 2, D_MLSTM, D_MLSTM), lambda c: (0, 0, 0, 0))
    nspec = pl.BlockSpec((b, 2, nh, D_MLSTM), lambda c: (0, 0, 0, 0))
    mspec = pl.BlockSpec((b, nh, LANES), lambda c: (0, 0, 0))
    return pl.pallas_call(
        _mlstm_kernel,
        grid=(nc,),
        in_specs=[tok(fw), tr(D_MLSTM, fw), tr(D_MLSTM, fw), tr(N_GATES, fw),
                  tok(bw), tr(D_MLSTM, bw), tr(D_MLSTM, bw), tr(N_GATES, bw), cspec, nspec, mspec],
        out_specs=[tok(fw), tok(bw), cspec, nspec, mspec],
        out_shape=[jax.ShapeDtypeStruct((b, n, D_MLSTM), F32), jax.ShapeDtypeStruct((b, n, D_MLSTM), F32),
                   jax.ShapeDtypeStruct((b, 2, D_MLSTM, D_MLSTM), F32),
                   jax.ShapeDtypeStruct((b, 2, nh, D_MLSTM), F32),
                   jax.ShapeDtypeStruct((b, nh, LANES), F32)],
        scratch_shapes=[pltpu.VMEM((b, 2, D_MLSTM, D_MLSTM), F32), pltpu.VMEM((b, 2, nh, D_MLSTM), F32),
                        pltpu.VMEM((b, nh, LANES), F32)],
        compiler_params=_cparams(("arbitrary",)),
        name="mlstm",
    )(mk, mqt, mvt, grow, mk, mqt, mvt, grow, *state)


def _rope_tables(n):
    rows = n // GRID_W
    row = jnp.broadcast_to(jnp.arange(rows, dtype=F32)[:, None], (rows, GRID_W)).reshape(n)
    colp = jnp.broadcast_to(jnp.arange(GRID_W, dtype=F32)[None, :], (rows, GRID_W)).reshape(n)
    nf = HEAD_DIM // 4
    inv = ROPE_BASE ** (-jnp.arange(nf, dtype=F32) / nf)
    ar = row[:, None] * inv
    ac = colp[:, None] * inv
    ang = jnp.concatenate([ar, ar, ac, ac], axis=-1)
    cos = jnp.cos(ang)
    sin = jnp.sin(ang)
    sign = jnp.where((jnp.arange(HEAD_DIM) & 16) == 0, -1.0, 1.0).astype(F32)
    reps = LANES // HEAD_DIM
    return jnp.tile(cos, (1, reps)), jnp.tile(sin * sign, (1, reps))


def _dft_factor_tables(n):
    nk1 = n // DFT_K0
    j = jnp.arange(n, dtype=jnp.int32)
    k1 = jnp.arange(nk1, dtype=jnp.int32) * DFT_K0
    k0 = jnp.arange(DFT_K0, dtype=jnp.int32)
    w = 2.0 * math.pi / n
    ang_a = ((k1[:, None] * j[None, :]) % n).astype(F32) * w
    ang_b = ((k0[:, None] * j[None, :]) % n).astype(F32) * w
    return jnp.cos(ang_a), jnp.sin(ang_a), jnp.cos(ang_b), jnp.sin(ang_b)


def _dft_tables(n):
    ca, sa, cb, sb = _dft_factor_tables(n)
    cn = (ca[:, None, :] * cb[None] - sa[:, None, :] * sb[None]).reshape(n, n)
    sn = (sa[:, None, :] * cb[None] + ca[:, None, :] * sb[None]).reshape(n, n)
    return cn.astype(BF16), sn.astype(BF16)


def _blockdiag(blocks):
    g, c = blocks.shape[-3], blocks.shape[-1]
    eye = jnp.eye(g, dtype=blocks.dtype)
    out = jnp.einsum('...gce,gh->...gche', blocks, eye)
    return out.reshape(blocks.shape[:-3] + (g * c, g * c))


def _channel_dft_blockdiag():
    e = jnp.arange(FOURIER_CH, dtype=jnp.int32)
    ang = ((e[:, None] * e[None, :]) % FOURIER_CH).astype(F32) * (2.0 * math.pi / FOURIER_CH)
    reps = (FOURIER_GROUPS, 1, 1)
    return _blockdiag(jnp.tile(jnp.cos(ang)[None], reps)), _blockdiag(jnp.tile(-jnp.sin(ang)[None], reps))


def kernel(x, c, ctx, c_ctx, w_ada, b_ada, norm_g, w_ffn_in, w_ffn_out, w_in, w_out, w_fourier, attn_sink,
           conv_qk, b_gate_i, b_gate_f, g_final):
    b, n, d = x.shape
    cl = ctx.shape[1]
    depth = w_ada.shape[0]
    nh = 2 * MLSTM_HEADS

    rows = -(-(b + 1) // SUBLANES) * SUBLANES
    c_all = jnp.zeros((rows, d), F32).at[:b].set(c).at[b].set(c_ctx)
    mod = _modulation(c_all, w_ada, b_ada).reshape(depth, rows, N_MOD, d)
    x_row = lambda bi: bi
    c_row = lambda bi: b

    offs = [0]
    for s in (D_FOURIER, D_ATT, D_KV, D_KV, D_MLSTM, D_MLSTM, D_MLSTM, D_MLSTM, nh, nh):
        offs.append(offs[-1] + s)
    w_a = w_in[:, :, offs[0]:offs[1]]
    w_gate = w_in[:, :, offs[8]:offs[10]]
    cc_bd, sc_bd = _channel_dft_blockdiag()
    w_p = _fourier_fold(w_a, _blockdiag(w_fourier), cc_bd, sc_bd, 1.0 / math.sqrt(n * FOURIER_CH))
    w_p_ctx = w_p * math.sqrt(n / cl)
    w_mid = w_in[:, :, offs[1]:offs[8]]
    w_all = jnp.concatenate([w_p, w_mid], axis=-1).astype(BF16)
    w_all_ctx = jnp.concatenate([w_p_ctx, w_mid], axis=-1).astype(BF16)
    wg_t = jnp.swapaxes(w_gate, 1, 2).astype(BF16)
    gb_row = jnp.concatenate([b_gate_i.reshape(depth, nh), b_gate_f.reshape(depth, nh)], axis=-1)[:, :, None]
    wf = (w_ffn_in[0, 0].astype(BF16), w_ffn_out[0, 0].astype(BF16))
    wo = w_out.astype(BF16)

    cos128, sin128 = _rope_tables(n)
    dft_x = _dft_factor_tables(n)
    cnc, snc = _dft_tables(cl)
    zeros_c = jnp.zeros((cl, LANES), F32)
    state0 = (jnp.zeros((b, 2, D_MLSTM, D_MLSTM), F32), jnp.zeros((b, 2, nh, D_MLSTM), F32),
              jnp.zeros((b, nh, LANES), F32))

    xc = ctx
    for l in range(depth):
        last = l == depth - 1
        x, *wf_next = _ffn(x, l, 0, mod, x_row, norm_g, *wf, cast_next=(w_ffn_in, w_ffn_out, l, 1))
        xc = _ffn(xc, l, 0, mod, c_row, norm_g, *wf)
        wf = wf_next

        px, qtx, kx, vtx, mkx, mqtx, mvtx, ogx, grx = _inproj(
            x, l, mod, x_row, norm_g, w_all, wg_t, gb_row, conv_qk, cos128, sin128, True)
        pc, qtc, kc, vtc, mkc, mqtc, mvtc, ogc, grc = _inproj(
            xc, l, mod, c_row, norm_g, w_all_ctx, wg_t, gb_row, conv_qk, zeros_c, zeros_c, False)

        hcf, hcb, *state_c = _mlstm(mkc, mqtc, mvtc, grc, state0)
        hxf, hxb, *_ = _mlstm(mkx, mqtx, mvtx, grx, state_c)
        att_x = _attention(attn_sink, l, qtx, kx, vtx, kc, vtc, True)
        yf_x = _dft_gen(dft_x, px) if n % TK_DFT == 0 else _dft(*_dft_tables(n), px)
        mix_x = (yf_x, att_x, hxf, hxb, ogx)
        if last:
            x = _ffn(x, l, 2, mod, x_row, norm_g, *wf, mixer=mix_x, w_o=wo, g_final=g_final)
        else:
            x, *wf_next = _ffn(x, l, 2, mod, x_row, norm_g, *wf, mixer=mix_x, w_o=wo,
                               cast_next=(w_ffn_in, w_ffn_out, l + 1, 0))
            att_c = _attention(attn_sink, l, qtc, None, None, kc, vtc, False)
            yf_c = _dft(cnc, snc, pc)
            xc = _ffn(xc, l, 2, mod, c_row, norm_g, *wf, mixer=(yf_c, att_c, hcf, hcb, ogc), w_o=wo)
            wf = wf_next
    return x
```

```python
import functools
import math

import jax
import jax.numpy as jnp
from jax import lax
from jax.experimental import pallas as pl
from jax.experimental.pallas import tpu as pltpu

F32 = jnp.float32
BF16 = jnp.bfloat16
HI = lax.Precision.HIGHEST

GRID_W = 64
EPS = 1e-6
NEG = -1e30
LOG2E = math.log2(math.e)
FOURIER_GROUPS = 4
FOURIER_CH = 64
D_FOURIER = FOURIER_GROUPS * FOURIER_CH
HEAD_DIM = 64
ATT_HEADS = 8
ATT_KV_HEADS = 2
D_ATT = ATT_HEADS * HEAD_DIM
D_KV = ATT_KV_HEADS * HEAD_DIM
ATT_BLOCK = 128
WINDOW = 128
ROPE_BASE = 10000.0
MLSTM_HEADS = 4
MLSTM_DIM = 64
D_MLSTM = MLSTM_HEADS * MLSTM_DIM
N_MOD = 9
N_GATES = 4 * MLSTM_HEADS

LANES = 128
SUBLANES = 8
VMEM_LIMIT = 56 * 1024 * 1024

TM = 512
FFN_SPLIT = 2
TK_DFT = 512
DFT_K0 = 64
DFT_TAIL = 16
TQ_ATT = 512
MLSTM_L = 128

C_P = 0
C_Q = C_P + 2 * D_FOURIER
C_K = C_Q + D_ATT
C_V = C_K + D_KV
C_MQ = C_V + D_KV
C_MK = C_MQ + D_MLSTM
C_MV = C_MK + D_MLSTM
C_MO = C_MV + D_MLSTM
C_END = C_MO + D_MLSTM


def _cparams(sem):
    return pltpu.CompilerParams(dimension_semantics=sem, vmem_limit_bytes=VMEM_LIMIT)


def _dot(a, b):
    return jnp.dot(a, b, preferred_element_type=F32)


def _dot_nt(a, b):
    return lax.dot_general(a, b, (((1,), (1,)), ((), ())), preferred_element_type=F32)


def _dot_hi(a, b):
    return jnp.dot(a, b, preferred_element_type=F32, precision=HI)


def _dot_split3(x, onehot):
    rows = x.shape[0]
    hi = x.astype(BF16)
    r1 = x - hi.astype(F32)
    mid = r1.astype(BF16)
    lo = (r1 - mid.astype(F32)).astype(BF16)
    y = _dot(jnp.concatenate([hi, mid, lo], axis=0), onehot)
    return y[0:rows] + y[rows:2 * rows] + y[2 * rows:3 * rows]


def _rms_mod(xt, g, shift, scale):
    ms = jnp.mean(xt * xt, axis=-1, keepdims=True)
    y = xt * lax.rsqrt(ms + EPS) * g
    return y * (1.0 + scale) + shift


def _silu(t):
    return t * jax.nn.sigmoid(t)


def _log_sigmoid(t):
    return jnp.minimum(t, 0.0) - jnp.log1p(jnp.exp(-jnp.abs(t)))


def _mod_kernel(c_ref, w_ref, b_ref, o_ref):
    sc = _silu(c_ref[...])
    o_ref[...] = _dot(sc.astype(BF16), w_ref[...].astype(BF16)) + b_ref[...]


def _modulation(c_all, w_ada, b_ada):
    depth, d, nd = w_ada.shape
    r = c_all.shape[0]
    tn = 1024
    return pl.pallas_call(
        _mod_kernel,
        grid=(depth, nd // tn),
        in_specs=[pl.BlockSpec((r, d), lambda l, j: (0, 0)),
                  pl.BlockSpec((None, d, tn), lambda l, j: (l, 0, j)),
                  pl.BlockSpec((None, 1, tn), lambda l, j: (l, 0, j))],
        out_specs=pl.BlockSpec((None, r, tn), lambda l, j: (l, 0, j)),
        out_shape=jax.ShapeDtypeStruct((depth, r, nd), F32),
        compiler_params=_cparams(("parallel", "parallel")),
        name="modulation",
    )(c_all, w_ada, b_ada.reshape(depth, 1, nd))


def _fold_kernel(wa_ref, wf_ref, cc_ref, sc_ref, o_ref, *, scale):
    wf = wf_ref[...]
    mc = _dot_hi(cc_ref[...], wf) * scale
    ms = _dot_hi(sc_ref[...], wf) * scale
    wa = wa_ref[...]
    o_ref[:, :D_FOURIER] = _dot_hi(wa, mc)
    o_ref[:, D_FOURIER:] = _dot_hi(wa, ms)


def _fourier_fold(wa, wf_bd, cc_bd, sc_bd, scale):
    depth, d, _ = wa.shape
    return pl.pallas_call(
        functools.partial(_fold_kernel, scale=scale),
        grid=(depth,),
        in_specs=[pl.BlockSpec((None, d, D_FOURIER), lambda l: (l, 0, 0)),
                  pl.BlockSpec((None, D_FOURIER, D_FOURIER), lambda l: (l, 0, 0)),
                  pl.BlockSpec((D_FOURIER, D_FOURIER), lambda l: (0, 0)),
                  pl.BlockSpec((D_FOURIER, D_FOURIER), lambda l: (0, 0))],
        out_specs=pl.BlockSpec((None, d, 2 * D_FOURIER), lambda l: (l, 0, 0)),
        out_shape=jax.ShapeDtypeStruct((depth, d, 2 * D_FOURIER), F32),
        compiler_params=_cparams(("parallel",)),
        name="fourier_fold",
    )(wa, wf_bd, cc_bd, sc_bd)


def _ffn_kernel(*refs, sub, d_ff, mix, split_yf, final, cast_next):
    x_ref, mod_ref, g_ref, win_ref, wout_ref = refs[:5]
    if cast_next:
        o_ref, nin_o_ref, nout_o_ref = refs[-3:]
        nin_ref, nout_ref = refs[-5:-3]
        nin_o_ref[...] = nin_ref[...].astype(BF16)
        nout_o_ref[...] = nout_ref[...].astype(BF16)
        rest = list(refs[5:-5])
    else:
        o_ref = refs[-1]
        rest = list(refs[5:-1])
    shift = mod_ref[3 * sub:3 * sub + 1, :]
    scale = mod_ref[3 * sub + 1:3 * sub + 2, :]
    gate = mod_ref[3 * sub + 2:3 * sub + 3, :]
    tm = x_ref.shape[0]
    rs = tm // FFN_SPLIT
    def stages(part):
        rows = slice(part * rs, (part + 1) * rs)
        v = {}

        def s_norm():
            xt = x_ref[rows, :]
            if mix:
                if split_yf:
                    yt_ref, yb_ref, att_ref, hf_ref, hb_ref, og_ref, wo_ref = rest[:7]
                    in_top = pl.program_id(1) < pl.num_programs(1) // 2
                    yf = jnp.where(in_top, yt_ref[rows, :], yb_ref[rows, :])
                else:
                    yf_ref, att_ref, hf_ref, hb_ref, og_ref, wo_ref = rest[:6]
                    yf = yf_ref[rows, :]
                mh = (og_ref[rows, :].astype(F32) * (hf_ref[rows, :] + hb_ref[rows, :])).astype(BF16)
                a0, a1 = D_FOURIER, D_FOURIER + D_ATT
                y = (_dot(yf, wo_ref[:a0, :]) + _dot(att_ref[rows, :], wo_ref[a0:a1, :])
                     + _dot(mh, wo_ref[a1:, :]))
                xt = xt + mod_ref[5:6, :] * y
            v['xt'] = xt
            v['h'] = _rms_mod(xt, g_ref[sub:sub + 1, :], shift, scale).astype(BF16)

        def s_up():
            v['gu'] = _dot(v.pop('h'), win_ref[...])

        def s_act():
            gu = v.pop('gu')
            v['act'] = (_silu(gu[:, :d_ff]) * gu[:, d_ff:]).astype(BF16)

        def s_down():
            y = v.pop('xt') + (0.5 * gate) * _dot(v.pop('act'), wout_ref[...])
            if final:
                gf_ref = rest[-1]
                ms = jnp.mean(y * y, axis=-1, keepdims=True)
                y = y * lax.rsqrt(ms + EPS) * gf_ref[...]
            o_ref[rows, :] = y

        return s_norm, s_up, s_act, s_down

    parts = [stages(part) for part in range(FFN_SPLIT)]
    for step in range(FFN_SPLIT + 3):
        for stage in range(4):
            part = step - stage
            if 0 <= part < FFN_SPLIT:
                parts[part][stage]()


def _slab_count(rows, steps):
    for nblk in range(steps, 0, -1):
        if rows % nblk == 0 and (rows // nblk) % (2 * SUBLANES) == 0:
            return nblk
    return 1


def _ffn(x, layer, sub, mod, mod_row, norm_g, w_in, w_out, mixer=None, w_o=None, g_final=None, cast_next=None):
    b, n, d = x.shape
    d_ff = w_out.shape[0]
    tm = min(TM, n)
    nt = n // tm
    mix = mixer is not None
    final = g_final is not None
    tok = lambda w: pl.BlockSpec((None, tm, w), lambda bi, t: (bi, t, 0))
    whole = lambda r, c: pl.BlockSpec((r, c), lambda bi, t: (0, 0), pipeline_mode=pl.Buffered(1))
    in_specs = [tok(d),
                pl.BlockSpec((None, None, N_MOD, d), lambda bi, t: (layer, mod_row(bi), 0, 0)),
                pl.BlockSpec((None, 3, d), lambda bi, t: (layer, 0, 0)),
                whole(d, 2 * d_ff), whole(d_ff, d)]
    args = [x, mod, norm_g, w_in, w_out]
    split_yf = mix and isinstance(mixer[0], tuple)
    if mix:
        if split_yf:
            hh = nt // 2
            in_specs += [pl.BlockSpec((None, tm, D_FOURIER), lambda bi, t: (bi, jnp.minimum(t, hh - 1), 0)),
                         pl.BlockSpec((None, tm, D_FOURIER), lambda bi, t: (bi, jnp.maximum(t - hh, 0), 0))]
            args += list(mixer[0])
        else:
            in_specs.append(tok(D_FOURIER))
            args.append(mixer[0])
        in_specs += [tok(D_ATT), tok(D_MLSTM), tok(D_MLSTM), tok(D_MLSTM),
                     pl.BlockSpec((None, d, d), lambda bi, t: (layer, 0, 0), pipeline_mode=pl.Buffered(1))]
        args += list(mixer[1:]) + [w_o]
    if final:
        in_specs.append(pl.BlockSpec((1, d), lambda bi, t: (0, 0)))
        args.append(g_final.reshape(1, d))
    out_specs = [tok(d)]
    out_shape = [jax.ShapeDtypeStruct((b, n, d), F32)]
    if cast_next is not None:
        nin, nout, nl, nh_ = cast_next
        steps = b * nt
        for arr, rows, cols in ((nin, d, 2 * d_ff), (nout, d_ff, d)):
            nblk = _slab_count(rows, steps)
            slab = lambda bi, t, nblk=nblk: jnp.minimum(bi * nt + t, nblk - 1)
            in_specs.append(pl.BlockSpec((None, None, rows // nblk, cols),
                                         lambda bi, t, slab=slab: (nl, nh_, slab(bi, t), 0)))
            out_specs.append(pl.BlockSpec((rows // nblk, cols), lambda bi, t, slab=slab: (slab(bi, t), 0)))
            out_shape.append(jax.ShapeDtypeStruct((rows, cols), BF16))
            args.append(arr)
    res = pl.pallas_call(
        functools.partial(_ffn_kernel, sub=sub, d_ff=d_ff, mix=mix, split_yf=split_yf, final=final,
                          cast_next=cast_next is not None),
        grid=(b, nt),
        in_specs=in_specs,
        out_specs=out_specs,
        out_shape=out_shape,
        compiler_params=_cparams(("arbitrary", "arbitrary")),
        name="ffn_mix" if mix else "ffn",
    )(*args)
    return res if cast_next is not None else res[0]


def _rope128(t, cos, sin_signed, low16):
    fwd = pltpu.roll(t, LANES - 16, axis=1)
    bwd = pltpu.roll(t, 16, axis=1)
    return t * cos + jnp.where(low16, fwd, bwd) * sin_signed


def _inproj_kernel(x_ref, xp_ref, xn_ref, mod_ref, g_ref, w_ref, wgt_ref, gbr_ref, cw_ref,
                   cos_ref, sin_ref,
                   p_ref, qt_ref, k_ref, vt_ref, mk_ref, mqt_ref, mvt_ref, og_ref, gr_ref, *, rope):
    t = pl.program_id(1)
    nt = pl.num_programs(1)
    tm = x_ref.shape[0]
    shift = mod_ref[3:4, :]
    scale = mod_ref[4:5, :]
    g = g_ref[1:2, :]
    h = _rms_mod(x_ref[...], g, shift, scale).astype(BF16)
    seg = lambda c0, c1: _dot(h, w_ref[:, c0:c1])
    lane = lax.broadcasted_iota(jnp.int32, (1, LANES), 1)
    low16 = (lane & 16) == 0
    qscale = HEAD_DIM ** -0.5 * LOG2E
    if rope:
        cos = cos_ref[...]
        sin = sin_ref[...]

    up = seg(C_P, C_Q)
    uq = seg(C_Q, C_K)
    p_ref[...] = up.astype(BF16)

    ukv = seg(C_K, C_MQ)
    for cb in range(D_ATT // LANES):
        qb = uq[:, cb * LANES:(cb + 1) * LANES]
        if rope:
            qb = _rope128(qb, cos, sin, low16)
        qt_ref[cb * LANES:(cb + 1) * LANES, :] = jnp.transpose(qb * qscale).astype(BF16)

    uqk = seg(C_MQ, C_MV)
    xh = jnp.concatenate([xp_ref[...], xn_ref[...]], axis=0)
    hh = _rms_mod(xh, g, shift, scale).astype(BF16)
    uh = _dot(hh, w_ref[:, C_MQ:C_MV])
    kb = ukv[:, :D_KV]
    if rope:
        kb = _rope128(kb, cos, sin, low16)
    k_ref[...] = kb.astype(BF16)
    vt_ref[...] = jnp.transpose(ukv[:, D_KV:]).astype(BF16)

    uvo = seg(C_MV, C_END)
    prev_row = jnp.where(t > 0, uh[SUBLANES - 1:SUBLANES, :], 0.0)
    next_row = jnp.where(t < nt - 1, uh[SUBLANES:SUBLANES + 1, :], 0.0)
    row = lax.broadcasted_iota(jnp.int32, (tm, 1), 0)
    u_prev = jnp.where(row == 0, prev_row, pltpu.roll(uqk, 1, axis=0))
    u_next = jnp.where(row == tm - 1, next_row, pltpu.roll(uqk, tm - 1, axis=0))
    cv = u_prev * cw_ref[0:1, :] + uqk * cw_ref[1:2, :] + u_next * cw_ref[2:3, :]
    qk = _silu(cv)
    mqt_ref[...] = jnp.transpose(qk[:, :D_MLSTM]).astype(BF16)
    mk_ref[...] = (qk[:, D_MLSTM:] * (MLSTM_DIM ** -0.5)).astype(BF16)

    half = N_GATES // 2
    zr = _dot_nt(wgt_ref[...], h) + gbr_ref[...]
    mvt_ref[...] = jnp.transpose(uvo[:, :D_MLSTM]).astype(BF16)
    og_ref[...] = jax.nn.sigmoid(uvo[:, D_MLSTM:]).astype(BF16)
    rowg = lax.broadcasted_iota(jnp.int32, (N_GATES, 1), 0)
    gr_ref[...] = jnp.where(rowg >= half, _log_sigmoid(zr), zr)


def _inproj(x, layer, mod, mod_row, norm_g, w_all, wg_t, gb_row, conv_w, cos128, sin128, rope):
    b, n, d = x.shape
    tm = min(TM, n)
    nt = n // tm
    r8 = tm // SUBLANES
    n8 = n // SUBLANES
    lay = lambda *blk: pl.BlockSpec((None,) + blk, lambda bi, t: (layer,) + (0,) * len(blk))
    tok = lambda w: pl.BlockSpec((None, tm, w), lambda bi, t: (bi, t, 0))
    trn = lambda rows: pl.BlockSpec((None, rows, tm), lambda bi, t: (bi, 0, t))
    in_specs = [tok(d),
                pl.BlockSpec((None, SUBLANES, d), lambda bi, t: (bi, jnp.maximum(t * r8 - 1, 0), 0)),
                pl.BlockSpec((None, SUBLANES, d), lambda bi, t: (bi, jnp.minimum((t + 1) * r8, n8 - 1), 0)),
                pl.BlockSpec((None, None, N_MOD, d), lambda bi, t: (layer, mod_row(bi), 0, 0)),
                lay(3, d), lay(d, C_END), lay(N_GATES, d), lay(N_GATES, 1), lay(3, 2 * D_MLSTM),
                pl.BlockSpec((tm, LANES), lambda bi, t: (t, 0)),
                pl.BlockSpec((tm, LANES), lambda bi, t: (t, 0))]
    out_specs = [tok(2 * D_FOURIER), trn(D_ATT), tok(D_KV), trn(D_KV),
                 tok(D_MLSTM), trn(D_MLSTM), trn(D_MLSTM), tok(D_MLSTM), trn(N_GATES)]
    shp = lambda w, dt: jax.ShapeDtypeStruct((b, n, w), dt)
    shpt = lambda rows, dt: jax.ShapeDtypeStruct((b, rows, n), dt)
    out_shape = [shp(2 * D_FOURIER, BF16), shpt(D_ATT, BF16), shp(D_KV, BF16), shpt(D_KV, BF16),
                 shp(D_MLSTM, BF16), shpt(D_MLSTM, BF16), shpt(D_MLSTM, BF16), shp(D_MLSTM, BF16),
                 shpt(N_GATES, F32)]
    return pl.pallas_call(
        functools.partial(_inproj_kernel, rope=rope),
        grid=(b, nt),
        in_specs=in_specs,
        out_specs=out_specs,
        out_shape=out_shape,
        compiler_params=_cparams(("parallel", "parallel")),
        name="inproj",
    )(x, x, x, mod, norm_g, w_all, wg_t, gb_row, conv_w, cos128, sin128)


def _dft_kernel(cn_ref, sn_ref, p_ref, y_ref):
    y = _dot(cn_ref[...], p_ref[:, :D_FOURIER]) + _dot(sn_ref[...], p_ref[:, D_FOURIER:])
    y_ref[...] = y.astype(BF16)


def _dft(cn, sn, p):
    b, n, _ = p.shape
    tk = min(TK_DFT, n)
    return pl.pallas_call(
        _dft_kernel,
        grid=(n // tk, b),
        in_specs=[pl.BlockSpec((tk, n), lambda i, bi: (i, 0)),
                  pl.BlockSpec((tk, n), lambda i, bi: (i, 0)),
                  pl.BlockSpec((None, n, 2 * D_FOURIER), lambda i, bi: (bi, 0, 0))],
        out_specs=pl.BlockSpec((None, tk, D_FOURIER), lambda i, bi: (bi, i, 0)),
        out_shape=jax.ShapeDtypeStruct((b, n, D_FOURIER), BF16),
        compiler_params=_cparams(("parallel", "parallel")),
        name="dft",
    )(cn, sn, p)


def _anti_identity(rows, cols, offset):
    r = lax.broadcasted_iota(jnp.int32, (rows, cols), 0)
    c = lax.broadcasted_iota(jnp.int32, (rows, cols), 1)
    return jnp.where(r + c == offset, 1.0, 0.0).astype(BF16)


def _dft_fold_kernel(p_ref, pf_ref, mid_ref):
    n = p_ref.shape[0]
    nh = n // 2
    tk = TK_DFT
    flip = _anti_identity(tk, tk, tk - 1)
    nblk = nh // tk
    rev = jnp.concatenate([_dot(flip, p_ref[n - (i + 1) * tk:n - i * tk, :]) for i in range(nblk)], axis=0)
    row = lax.broadcasted_iota(jnp.int32, (nh, 1), 0)
    mirror = jnp.where(row == 0, 0.0, pltpu.roll(rev, 1, axis=0))
    top = p_ref[0:nh, :].astype(F32)
    pf_ref[:, :D_FOURIER] = (top[:, :D_FOURIER] + mirror[:, :D_FOURIER]).astype(BF16)
    pf_ref[:, D_FOURIER:] = (top[:, D_FOURIER:] - mirror[:, D_FOURIER:]).astype(BF16)
    mid_ref[...] = jnp.broadcast_to(p_ref[nh:nh + 1, :D_FOURIER].astype(F32), mid_ref.shape)


def _dft_half_kernel(ca_ref, sa_ref, cb_ref, sb_ref, pf_ref, mid_ref, top_ref, bot_ref, cn_scr, sn_scr):
    i = pl.program_id(0)
    tk = top_ref.shape[0]
    groups = tk // DFT_K0

    @pl.when(pl.program_id(1) == 0)
    def _():
        cb = cb_ref[...]
        sb = sb_ref[...]
        for r in range(groups + 1):
            nr = DFT_K0 if r < groups else DFT_TAIL
            ca = ca_ref[pl.ds(i * groups + r, 1), :]
            sa = sa_ref[pl.ds(i * groups + r, 1), :]
            cn_scr[r * DFT_K0:r * DFT_K0 + nr, :] = (ca * cb[:nr] - sa * sb[:nr]).astype(BF16)
            sn_scr[r * DFT_K0:r * DFT_K0 + nr, :] = (sa * cb[:nr] + ca * sb[:nr]).astype(BF16)

    e = _dot(cn_scr[...], pf_ref[:, :D_FOURIER])
    o = _dot(sn_scr[...], pf_ref[:, D_FOURIER:])
    rows = lax.broadcasted_iota(jnp.int32, (tk + DFT_TAIL, 1), 0)
    mid = mid_ref[0:1, :]
    e = e + jnp.where((rows & 1) == 0, mid, -mid)
    top_ref[...] = (e + o)[:tk].astype(BF16)
    bot_ref[...] = _dot(_anti_identity(tk, tk + DFT_TAIL, tk), (e - o).astype(BF16)).astype(BF16)


def _dft_half(tables, p):
    ca, sa, cb, sb = tables
    b, n, _ = p.shape
    nh = n // 2
    tk = TK_DFT
    nt = nh // tk
    pf, mid = pl.pallas_call(
        _dft_fold_kernel,
        grid=(b,),
        in_specs=[pl.BlockSpec((None, n, 2 * D_FOURIER), lambda bi: (bi, 0, 0))],
        out_specs=[pl.BlockSpec((None, nh, 2 * D_FOURIER), lambda bi: (bi, 0, 0)),
                   pl.BlockSpec((None, SUBLANES, D_FOURIER), lambda bi: (bi, 0, 0))],
        out_shape=[jax.ShapeDtypeStruct((b, nh, 2 * D_FOURIER), BF16),
                   jax.ShapeDtypeStruct((b, SUBLANES, D_FOURIER), F32)],
        compiler_params=_cparams(("parallel",)),
        name="dft_fold",
    )(p)
    whole = lambda a: pl.BlockSpec(a.shape, lambda i, bi: (0, 0))
    return pl.pallas_call(
        _dft_half_kernel,
        grid=(nt, b),
        in_specs=[whole(ca), whole(sa), whole(cb), whole(sb),
                  pl.BlockSpec((None, nh, 2 * D_FOURIER), lambda i, bi: (bi, 0, 0)),
                  pl.BlockSpec((None, SUBLANES, D_FOURIER), lambda i, bi: (bi, 0, 0))],
        out_specs=[pl.BlockSpec((None, tk, D_FOURIER), lambda i, bi: (bi, i, 0)),
                   pl.BlockSpec((None, tk, D_FOURIER), lambda i, bi: (bi, nt - 1 - i, 0))],
        out_shape=[jax.ShapeDtypeStruct((b, nh, D_FOURIER), BF16), jax.ShapeDtypeStruct((b, nh, D_FOURIER), BF16)],
        scratch_shapes=[pltpu.VMEM((tk + DFT_TAIL, nh), BF16), pltpu.VMEM((tk + DFT_TAIL, nh), BF16)],
        compiler_params=_cparams(("parallel", "arbitrary")),
        name="dft_half",
    )(ca, sa, cb, sb, pf, mid)


def _attn_block(qt_blk, kall, vtall, biases, sinks, cl):
    blk = ATT_BLOCK
    group = ATT_HEADS // ATT_KV_HEADS
    gw = group * blk
    zq = jnp.zeros((HEAD_DIM, gw), BF16)
    cols = []
    for kv in range(ATT_KV_HEADS):
        r0 = kv * group * HEAD_DIM
        qblk = jnp.concatenate([qt_blk[r0 + i * HEAD_DIM:r0 + (i + 1) * HEAD_DIM, :] for i in range(group)], axis=1)
        qt_g = jnp.concatenate([qblk, zq] if kv == 0 else [zq, qblk], axis=0)
        s = _dot(kall, qt_g)
        sink = sinks[kv]
        if biases is not None:
            bias_p, bias_n = biases
            add = lambda sg, bias: jnp.concatenate(
                [sg[:, i * blk:(i + 1) * blk] + bias for i in range(group)], axis=1)
            segs = [s[:cl], add(s[cl:cl + blk], bias_p), s[cl + blk:cl + 2 * blk], add(s[cl + 2 * blk:], bias_n)]
        else:
            segs = [s]
        m = sink
        for sg in segs:
            m = jnp.maximum(m, jnp.max(sg, axis=0, keepdims=True))
        den = jnp.exp2(sink - m)
        ps = []
        for sg in segs:
            p = jnp.exp2(sg - m)
            den = den + jnp.sum(p, axis=0, keepdims=True)
            ps.append(p.astype(BF16))
        p_all = jnp.concatenate(ps, axis=0) if len(ps) > 1 else ps[0]
        vt_g = vtall[kv * HEAD_DIM:(kv + 1) * HEAD_DIM, :]
        ot = _dot(vt_g, p_all) * (1.0 / den)
        for cb in range(group // 2):
            pair = jnp.concatenate([ot[:, (2 * cb) * blk:(2 * cb + 1) * blk],
                                    ot[:, (2 * cb + 1) * blk:(2 * cb + 2) * blk]], axis=0)
            cols.append(jnp.transpose(pair).astype(BF16))
    return cols


def _attn_kernel(sink_ref, qt_ref, *rest, layer, local):
    if local:
        kp_ref, kc_ref, kn_ref, vp_ref, vc_ref, vn_ref, kx_ref, vxt_ref, o_ref = rest
    else:
        kx_ref, vxt_ref, o_ref = rest
    blk = ATT_BLOCK
    group = ATT_HEADS // ATT_KV_HEADS
    gw = group * blk
    cl = kx_ref.shape[0]
    nq = qt_ref.shape[1] // blk
    lane_head = lax.broadcasted_iota(jnp.int32, (1, gw), 1) // blk
    sinks = []
    for kv in range(ATT_KV_HEADS):
        sink = jnp.zeros((1, gw), F32)
        for i in range(group):
            sink = jnp.where(lane_head == i, sink_ref[layer, kv * group + i] * LOG2E, sink)
        sinks.append(sink)
    kx = kx_ref[...]
    vxt = vxt_ref[...]
    if local:
        j = pl.program_id(1)
        nb = pl.num_programs(1) * nq
        ks = lax.broadcasted_iota(jnp.int32, (blk, blk), 0)
        qi = lax.broadcasted_iota(jnp.int32, (blk, blk), 1)
        in_p = jnp.abs(ks - blk - qi) <= WINDOW
        in_n = jnp.abs(ks + blk - qi) <= WINDOW
        kloc = jnp.concatenate([kp_ref[...], kc_ref[...], kn_ref[...]], axis=0)
        vloc = jnp.concatenate([vp_ref[...], vc_ref[...], vn_ref[...]], axis=1)
    zq = jnp.zeros((HEAD_DIM, gw), BF16)

    def scores(i, kv):
        qt_blk = qt_ref[:, i * blk:(i + 1) * blk]
        r0 = kv * group * HEAD_DIM
        qblk = jnp.concatenate([qt_blk[r0 + h * HEAD_DIM:r0 + (h + 1) * HEAD_DIM, :] for h in range(group)], axis=1)
        qt_g = jnp.concatenate([qblk, zq] if kv == 0 else [zq, qblk], axis=0)
        if not local:
            return [_dot(kx, qt_g)]
        jb = j * nq + i
        bias_p = jnp.where(in_p & (jb > 0), 0.0, NEG)
        bias_n = jnp.where(in_n & (jb < nb - 1), 0.0, NEG)
        s = _dot(jnp.concatenate([kx, kloc[i * blk:(i + 3) * blk, :]], axis=0), qt_g)
        add = lambda sg, bias: jnp.concatenate([sg[:, h * blk:(h + 1) * blk] + bias for h in range(group)], axis=1)
        return [s[:cl], add(s[cl:cl + blk], bias_p), s[cl + blk:cl + 2 * blk], add(s[cl + 2 * blk:], bias_n)]

    def colmax(segs, kv):
        m = sinks[kv]
        for sg in segs:
            m = jnp.maximum(m, jnp.max(sg, axis=0, keepdims=True))
        return m

    def softmax(segs, kv, m):
        den = jnp.exp2(sinks[kv] - m)
        ps = []
        for sg in segs:
            p = jnp.exp2(sg - m)
            den = den + jnp.sum(p, axis=0, keepdims=True)
            ps.append(p.astype(BF16))
        return (jnp.concatenate(ps, axis=0) if len(ps) > 1 else ps[0]), den

    def values(i, kv, p_all, den):
        vtall = jnp.concatenate([vxt, vloc[:, i * blk:(i + 3) * blk]], axis=1) if local else vxt
        ot = _dot(vtall[kv * HEAD_DIM:(kv + 1) * HEAD_DIM, :], p_all) * (1.0 / den)
        for cb in range(group // 2):
            pair = jnp.concatenate([ot[:, (2 * cb) * blk:(2 * cb + 1) * blk],
                                    ot[:, (2 * cb + 1) * blk:(2 * cb + 2) * blk]], axis=0)
            c0 = (kv * (group // 2) + cb) * LANES
            o_ref[i * blk:(i + 1) * blk, c0:c0 + LANES] = jnp.transpose(pair).astype(BF16)

    chains = [(i, kv) for i in range(nq) for kv in range(ATT_KV_HEADS)]
    segs_q, max_q, soft_q = {}, {}, {}
    for step in range(len(chains) + 3):
        if step < len(chains):
            segs_q[step] = scores(*chains[step])
        c = step - 1
        if 0 <= c < len(chains):
            max_q[c] = colmax(segs_q[c], chains[c][1])
        c = step - 2
        if 0 <= c < len(chains):
            soft_q[c] = softmax(segs_q.pop(c), chains[c][1], max_q.pop(c))
        c = step - 3
        if 0 <= c < len(chains):
            values(*chains[c], *soft_q.pop(c))


def _attention(sink, layer, qt, k, vt, kx, vxt, local):
    b, _, n = qt.shape
    cl = kx.shape[1]
    blk = ATT_BLOCK
    tq = min(TQ_ATT, n)
    nq = tq // blk
    nb = n // blk
    in_specs = [pl.BlockSpec(memory_space=pltpu.SMEM),
                pl.BlockSpec((None, D_ATT, tq), lambda bi, j: (bi, 0, j))]
    args = [sink, qt]
    if local:
        pj = lambda j: jnp.maximum(j * nq - 1, 0)
        nj = lambda j: jnp.minimum((j + 1) * nq, nb - 1)
        in_specs += [pl.BlockSpec((None, blk, D_KV), lambda bi, j: (bi, pj(j), 0)),
                     pl.BlockSpec((None, tq, D_KV), lambda bi, j: (bi, j, 0)),
                     pl.BlockSpec((None, blk, D_KV), lambda bi, j: (bi, nj(j), 0)),
                     pl.BlockSpec((None, D_KV, blk), lambda bi, j: (bi, 0, pj(j))),
                     pl.BlockSpec((None, D_KV, tq), lambda bi, j: (bi, 0, j)),
                     pl.BlockSpec((None, D_KV, blk), lambda bi, j: (bi, 0, nj(j)))]
        args += [k, k, k, vt, vt, vt]
    in_specs += [pl.BlockSpec((None, cl, D_KV), lambda bi, j: (bi, 0, 0)),
                 pl.BlockSpec((None, D_KV, cl), lambda bi, j: (bi, 0, 0))]
    args += [kx, vxt]
    return pl.pallas_call(
        functools.partial(_attn_kernel, layer=layer, local=local),
        grid=(b, n // tq),
        in_specs=in_specs,
        out_specs=pl.BlockSpec((None, tq, D_ATT), lambda bi, j: (bi, j, 0)),
        out_shape=jax.ShapeDtypeStruct((b, n, D_ATT), BF16),
        compiler_params=_cparams(("parallel", "parallel")),
        name="attention",
    )(*args)


def _mlstm_stages(d, k_ref, qt_ref, vt_ref, gr_ref, h_ref, c_scr, n_scr, m_scr, consts):
    L = MLSTM_L
    H = MLSTM_HEADS
    DH = MLSTM_DIM
    tri, valid_t, lane_h, row_h, blockdiag, nmask = consts
    r0 = d * H
    v = {}

    def stage_in():
        k = k_ref[...]
        qt = qt_ref[...]
        gr = gr_ref[...]
        cum = _dot_split3(gr, tri)
        li = gr[r0:r0 + H, :]
        a = cum[2 * H + r0:2 * H + r0 + H, :]
        a_tot = a[:, L - 1:L] if d == 0 else a[:, 0:1]
        m_in = m_scr[r0:r0 + H, 0:1]
        c_old = c_scr[d]
        n_old = n_scr[d]
        kst = jnp.concatenate([jnp.where(lane_h == hd, k, jnp.zeros_like(k)) for hd in range(H)], axis=0)
        v.update(k=k, li=li, a=a, a_tot=a_tot, m_in=m_in, c_old=c_old, n_old=n_old,
                 g_prev=a + m_in,
                 b_t=jnp.transpose(jnp.concatenate([li - a, jnp.zeros((L - H, L), F32)], axis=0)),
                 inter=_dot(jnp.concatenate([c_old, n_old], axis=0).astype(BF16), qt),
                 s_t=_dot(kst, qt))

    def stage_weights():
        p_parts, m_rows, wp_rows, den_rows = [], [], [], []
        for hd in range(H):
            d_t = jnp.where(valid_t, v['b_t'][:, hd:hd + 1] + v['a'][hd:hd + 1, :], NEG)
            g_prev = v['g_prev'][hd:hd + 1, :]
            m_t = jnp.maximum(g_prev, jnp.max(d_t, axis=0, keepdims=True))
            p_t = v['s_t'][hd * L:(hd + 1) * L, :] * jnp.exp(d_t - m_t)
            w_prev = jnp.exp(g_prev - m_t)
            den = jnp.sum(p_t, axis=0, keepdims=True) + w_prev * v['inter'][D_MLSTM + hd:D_MLSTM + hd + 1, :]
            p_parts.append(p_t.astype(BF16))
            m_rows.append(m_t)
            wp_rows.append(w_prev)
            den_rows.append(den)
        v.update(p=jnp.concatenate(p_parts, axis=0), m_rows=m_rows, wp_rows=wp_rows, den_rows=den_rows)

    def stage_out():
        vt = vt_ref[...]
        k = v['k']
        vbd = jnp.concatenate([jnp.where(row_h == hd, vt, jnp.zeros_like(vt)) for hd in range(H)], axis=1)
        num_t = _dot(vbd, v['p'])
        outs = []
        for hd in range(H):
            inv = 1.0 / jnp.maximum(jnp.abs(v['den_rows'][hd]), jnp.exp(-v['m_rows'][hd]))
            sl = slice(hd * DH, (hd + 1) * DH)
            outs.append((num_t[sl, :] + v['wp_rows'][hd] * v['inter'][sl, :]) * inv)
        h_ref[...] = jnp.transpose(jnp.concatenate(outs, axis=0))
        a_tot, m_in = v['a_tot'], v['m_in']
        dl = a_tot - v['a'] + v['li']
        m_new = jnp.maximum(a_tot + m_in, jnp.max(dl, axis=1, keepdims=True))
        w = jnp.exp(dl - m_new)
        decay = jnp.exp(a_tot + m_in - m_new)
        wexp = jnp.concatenate([jnp.broadcast_to(w[hd:hd + 1, :], (DH, L)) for hd in range(H)], axis=0)
        dcol = jnp.concatenate([jnp.broadcast_to(decay[hd:hd + 1, :], (DH, 1)) for hd in range(H)], axis=0)
        upd = _dot((vt.astype(F32) * wexp).astype(BF16), k)
        c_scr[d] = dcol * v['c_old'] + jnp.where(blockdiag, upd, 0.0)
        n_upd = _dot(jnp.concatenate([w, jnp.zeros((H, L), F32)], axis=0).astype(BF16), k)
        decay8 = jnp.concatenate([decay, jnp.zeros((H, 1), F32)], axis=0)
        n_scr[d] = jnp.where(nmask, decay8 * v['n_old'] + n_upd, 0.0)
        m_scr[r0:r0 + H, :] = jnp.broadcast_to(m_new, (H, LANES))

    return stage_in, stage_weights, stage_out


def _mlstm_kernel(kf_ref, qtf_ref, vtf_ref, grf_ref, kb_ref, qtb_ref, vtb_ref, grb_ref,
                  c0_ref, n0_ref, m0_ref, hf_ref, hb_ref, cfin_ref, nfin_ref, mfin_ref, c_scr, n_scr, m_scr):
    c = pl.program_id(0)
    nc = pl.num_programs(0)
    L = MLSTM_L
    H = MLSTM_HEADS
    nb = kf_ref.shape[0]

    @pl.when(c == 0)
    def _():
        c_scr[...] = c0_ref[...]
        n_scr[...] = n0_ref[...]
        m_scr[...] = m0_ref[...]

    ri = lax.broadcasted_iota(jnp.int32, (L, L), 0)
    ci = lax.broadcasted_iota(jnp.int32, (L, L), 1)
    upper = ci >= ri
    lower = ci <= ri
    triu = jnp.where(upper, 1.0, 0.0).astype(BF16)
    tril = jnp.where(lower, 1.0, 0.0).astype(BF16)
    lane_h = lax.broadcasted_iota(jnp.int32, (1, D_MLSTM), 1) // MLSTM_DIM
    row_h = lax.broadcasted_iota(jnp.int32, (D_MLSTM, 1), 0) // MLSTM_DIM
    blockdiag = row_h == lane_h
    nmask = lax.broadcasted_iota(jnp.int32, (2 * H, 1), 0) == lane_h
    chains = []
    for bi in range(nb):
        st = (c_scr.at[bi], n_scr.at[bi], m_scr.at[bi])
        chains.append(_mlstm_stages(0, kf_ref.at[bi], qtf_ref.at[bi], vtf_ref.at[bi], grf_ref.at[bi],
                                    hf_ref.at[bi], *st, (triu, upper, lane_h, row_h, blockdiag, nmask)))
        chains.append(_mlstm_stages(1, kb_ref.at[bi], qtb_ref.at[bi], vtb_ref.at[bi], grb_ref.at[bi],
                                    hb_ref.at[bi], *st, (tril, lower, lane_h, row_h, blockdiag, nmask)))
    for step in range(len(chains) + 2):
        for stage in range(3):
            ch = step - stage
            if 0 <= ch < len(chains):
                chains[ch][stage]()

    @pl.when(c == nc - 1)
    def _():
        cfin_ref[...] = c_scr[...]
        nfin_ref[...] = n_scr[...]
        mfin_ref[...] = m_scr[...]


def _mlstm(mk, mqt, mvt, grow, state):
    b, n, _ = mk.shape
    L = MLSTM_L
    nc = n // L
    nh = 2 * MLSTM_HEADS
    tok = lambda cm: pl.BlockSpec((b, L, D_MLSTM), lambda c: (0, cm(c), 0))
    tr = lambda rows, cm: pl.BlockSpec((b, rows, L), lambda c: (0, 0, cm(c)))
    fw = lambda c: c
    bw = lambda c: nc - 1 - c
    cspec = pl.BlockSpec((b, 2, D_MLSTM, D_MLSTM), lambda c: (0, 0, 0, 0))
    nspec = pl.BlockSpec((b, 2, nh, D_MLSTM), lambda c: (0, 0, 0, 0))
    mspec = pl.BlockSpec((b, nh, LANES), lambda c: (0, 0, 0))
    return pl.pallas_call(
        _mlstm_kernel,
        grid=(nc,),
        in_specs=[tok(fw), tr(D_MLSTM, fw), tr(D_MLSTM, fw), tr(N_GATES, fw),
                  tok(bw), tr(D_MLSTM, bw), tr(D_MLSTM, bw), tr(N_GATES, bw), cspec, nspec, mspec],
        out_specs=[tok(fw), tok(bw), cspec, nspec, mspec],
        out_shape=[jax.ShapeDtypeStruct((b, n, D_MLSTM), F32), jax.ShapeDtypeStruct((b, n, D_MLSTM), F32),
                   jax.ShapeDtypeStruct((b, 2, D_MLSTM, D_MLSTM), F32),
                   jax.ShapeDtypeStruct((b, 2, nh, D_MLSTM), F32),
                   jax.ShapeDtypeStruct((b, nh, LANES), F32)],
        scratch_shapes=[pltpu.VMEM((b, 2, D_MLSTM, D_MLSTM), F32), pltpu.VMEM((b, 2, nh, D_MLSTM), F32),
                        pltpu.VMEM((b, nh, LANES), F32)],
        compiler_params=_cparams(("arbitrary",)),
        name="mlstm",
    )(mk, mqt, mvt, grow, mk, mqt, mvt, grow, *state)


def _rope_tables(n):
    rows = n // GRID_W
    row = jnp.broadcast_to(jnp.arange(rows, dtype=F32)[:, None], (rows, GRID_W)).reshape(n)
    colp = jnp.broadcast_to(jnp.arange(GRID_W, dtype=F32)[None, :], (rows, GRID_W)).reshape(n)
    nf = HEAD_DIM // 4
    inv = ROPE_BASE ** (-jnp.arange(nf, dtype=F32) / nf)
    ar = row[:, None] * inv
    ac = colp[:, None] * inv
    ang = jnp.concatenate([ar, ar, ac, ac], axis=-1)
    cos = jnp.cos(ang)
    sin = jnp.sin(ang)
    sign = jnp.where((jnp.arange(HEAD_DIM) & 16) == 0, -1.0, 1.0).astype(F32)
    reps = LANES // HEAD_DIM
    return jnp.tile(cos, (1, reps)), jnp.tile(sin * sign, (1, reps))


def _dft_factor_tables(n):
    nk1 = n // DFT_K0
    j = jnp.arange(n, dtype=jnp.int32)
    k1 = jnp.arange(nk1, dtype=jnp.int32) * DFT_K0
    k0 = jnp.arange(DFT_K0, dtype=jnp.int32)
    w = 2.0 * math.pi / n
    ang_a = ((k1[:, None] * j[None, :]) % n).astype(F32) * w
    ang_b = ((k0[:, None] * j[None, :]) % n).astype(F32) * w
    return jnp.cos(ang_a), jnp.sin(ang_a), jnp.cos(ang_b), jnp.sin(ang_b)


def _dft_half_tables(n):
    nh = n // 2
    nk1 = -(-(nh // DFT_K0 + 1) // SUBLANES) * SUBLANES
    j = jnp.arange(nh, dtype=jnp.int32)
    k1 = jnp.arange(nk1, dtype=jnp.int32) * DFT_K0
    k0 = jnp.arange(DFT_K0, dtype=jnp.int32)
    w = 2.0 * math.pi / n
    ang_a = ((k1[:, None] * j[None, :]) % n).astype(F32) * w
    ang_b = ((k0[:, None] * j[None, :]) % n).astype(F32) * w
    return jnp.cos(ang_a), jnp.sin(ang_a), jnp.cos(ang_b), jnp.sin(ang_b)


def _dft_tables(n):
    ca, sa, cb, sb = _dft_factor_tables(n)
    cn = (ca[:, None, :] * cb[None] - sa[:, None, :] * sb[None]).reshape(n, n)
    sn = (sa[:, None, :] * cb[None] + ca[:, None, :] * sb[None]).reshape(n, n)
    return cn.astype(BF16), sn.astype(BF16)


def _blockdiag(blocks):
    g, c = blocks.shape[-3], blocks.shape[-1]
    eye = jnp.eye(g, dtype=blocks.dtype)
    out = jnp.einsum('...gce,gh->...gche', blocks, eye)
    return out.reshape(blocks.shape[:-3] + (g * c, g * c))


def _channel_dft_blockdiag():
    e = jnp.arange(FOURIER_CH, dtype=jnp.int32)
    ang = ((e[:, None] * e[None, :]) % FOURIER_CH).astype(F32) * (2.0 * math.pi / FOURIER_CH)
    reps = (FOURIER_GROUPS, 1, 1)
    return _blockdiag(jnp.tile(jnp.cos(ang)[None], reps)), _blockdiag(jnp.tile(-jnp.sin(ang)[None], reps))


def kernel(x, c, ctx, c_ctx, w_ada, b_ada, norm_g, w_ffn_in, w_ffn_out, w_in, w_out, w_fourier, attn_sink,
           conv_qk, b_gate_i, b_gate_f, g_final):
    b, n, d = x.shape
    cl = ctx.shape[1]
    depth = w_ada.shape[0]
    nh = 2 * MLSTM_HEADS

    rows = -(-(b + 1) // SUBLANES) * SUBLANES
    c_all = jnp.zeros((rows, d), F32).at[:b].set(c).at[b].set(c_ctx)
    mod = _modulation(c_all, w_ada, b_ada).reshape(depth, rows, N_MOD, d)
    x_row = lambda bi: bi
    c_row = lambda bi: b

    offs = [0]
    for s in (D_FOURIER, D_ATT, D_KV, D_KV, D_MLSTM, D_MLSTM, D_MLSTM, D_MLSTM, nh, nh):
        offs.append(offs[-1] + s)
    w_a = w_in[:, :, offs[0]:offs[1]]
    w_gate = w_in[:, :, offs[8]:offs[10]]
    cc_bd, sc_bd = _channel_dft_blockdiag()
    w_p = _fourier_fold(w_a, _blockdiag(w_fourier), cc_bd, sc_bd, 1.0 / math.sqrt(n * FOURIER_CH))
    w_p_ctx = w_p * math.sqrt(n / cl)
    w_mid = w_in[:, :, offs[1]:offs[8]]
    w_all = jnp.concatenate([w_p, w_mid], axis=-1).astype(BF16)
    w_all_ctx = jnp.concatenate([w_p_ctx, w_mid], axis=-1).astype(BF16)
    wg_t = jnp.swapaxes(w_gate, 1, 2).astype(BF16)
    gb_row = jnp.concatenate([b_gate_i.reshape(depth, nh), b_gate_f.reshape(depth, nh)], axis=-1)[:, :, None]
    wf = (w_ffn_in[0, 0].astype(BF16), w_ffn_out[0, 0].astype(BF16))
    wo = w_out.astype(BF16)

    cos128, sin128 = _rope_tables(n)
    half_spectrum = n % (2 * TK_DFT) == 0 and TK_DFT == TM
    dft_x = _dft_half_tables(n) if half_spectrum else _dft_tables(n)
    cnc, snc = _dft_tables(cl)
    zeros_c = jnp.zeros((cl, LANES), F32)
    state0 = (jnp.zeros((b, 2, D_MLSTM, D_MLSTM), F32), jnp.zeros((b, 2, nh, D_MLSTM), F32),
              jnp.zeros((b, nh, LANES), F32))

    xc = ctx
    for l in range(depth):
        last = l == depth - 1
        x, *wf_next = _ffn(x, l, 0, mod, x_row, norm_g, *wf, cast_next=(w_ffn_in, w_ffn_out, l, 1))
        xc = _ffn(xc, l, 0, mod, c_row, norm_g, *wf)
        wf = wf_next

        px, qtx, kx, vtx, mkx, mqtx, mvtx, ogx, grx = _inproj(
            x, l, mod, x_row, norm_g, w_all, wg_t, gb_row, conv_qk, cos128, sin128, True)
        pc, qtc, kc, vtc, mkc, mqtc, mvtc, ogc, grc = _inproj(
            xc, l, mod, c_row, norm_g, w_all_ctx, wg_t, gb_row, conv_qk, zeros_c, zeros_c, False)

        hcf, hcb, *state_c = _mlstm(mkc, mqtc, mvtc, grc, state0)
        hxf, hxb, *_ = _mlstm(mkx, mqtx, mvtx, grx, state_c)
        att_x = _attention(attn_sink, l, qtx, kx, vtx, kc, vtc, True)
        yf_x = tuple(_dft_half(dft_x, px)) if half_spectrum else _dft(*dft_x, px)
        mix_x = (yf_x, att_x, hxf, hxb, ogx)
        if last:
            x = _ffn(x, l, 2, mod, x_row, norm_g, *wf, mixer=mix_x, w_o=wo, g_final=g_final)
        else:
            x, *wf_next = _ffn(x, l, 2, mod, x_row, norm_g, *wf, mixer=mix_x, w_o=wo,
                               cast_next=(w_ffn_in, w_ffn_out, l + 1, 0))
            att_c = _attention(attn_sink, l, qtc, None, None, kc, vtc, False)
            yf_c = _dft(cnc, snc, pc)
            xc = _ffn(xc, l, 2, mod, c_row, norm_g, *wf, mixer=(yf_c, att_c, hcf, hcb, ogc), w_o=wo)
            wf = wf_next
    return x
```

```python
import functools
import math

import jax
import jax.numpy as jnp
from jax import lax
from jax.experimental import pallas as pl
from jax.experimental.pallas import tpu as pltpu

F32 = jnp.float32
BF16 = jnp.bfloat16
HI = lax.Precision.HIGHEST

GRID_W = 64
EPS = 1e-6
NEG = -1e30
LOG2E = math.log2(math.e)
FOURIER_GROUPS = 4
FOURIER_CH = 64
D_FOURIER = FOURIER_GROUPS * FOURIER_CH
HEAD_DIM = 64
ATT_HEADS = 8
ATT_KV_HEADS = 2
D_ATT = ATT_HEADS * HEAD_DIM
D_KV = ATT_KV_HEADS * HEAD_DIM
ATT_BLOCK = 128
WINDOW = 128
ROPE_BASE = 10000.0
MLSTM_HEADS = 4
MLSTM_DIM = 64
D_MLSTM = MLSTM_HEADS * MLSTM_DIM
N_MOD = 9
N_GATES = 4 * MLSTM_HEADS

LANES = 128
SUBLANES = 8
VMEM_LIMIT = 56 * 1024 * 1024

TM = 512
TM_INPROJ = 1024
TM_FFN = 1024
FFN_PART_ROWS = 256
TK_DFT = 512
DFT_K0 = 64
DFT_TAIL = 16
TQ_ATT = 512
ATT_SKEW = (1, 2, 3)
MLSTM_L = 128

C_P = 0
C_Q = C_P + 2 * D_FOURIER
C_K = C_Q + D_ATT
C_V = C_K + D_KV
C_MQ = C_V + D_KV
C_MK = C_MQ + D_MLSTM
C_MV = C_MK + D_MLSTM
C_MO = C_MV + D_MLSTM
C_END = C_MO + D_MLSTM


def _cparams(sem):
    return pltpu.CompilerParams(dimension_semantics=sem, vmem_limit_bytes=VMEM_LIMIT)


def _dot(a, b):
    return jnp.dot(a, b, preferred_element_type=F32)


def _dot_nt(a, b):
    return lax.dot_general(a, b, (((1,), (1,)), ((), ())), preferred_element_type=F32)


def _dot_hi(a, b):
    return jnp.dot(a, b, preferred_element_type=F32, precision=HI)


def _dot_split3(x, onehot):
    rows = x.shape[0]
    hi = x.astype(BF16)
    r1 = x - hi.astype(F32)
    mid = r1.astype(BF16)
    lo = (r1 - mid.astype(F32)).astype(BF16)
    y = _dot(jnp.concatenate([hi, mid, lo], axis=0), onehot)
    return y[0:rows] + y[rows:2 * rows] + y[2 * rows:3 * rows]


def _rms_mod(xt, g, shift, scale):
    ms = jnp.mean(xt * xt, axis=-1, keepdims=True)
    y = xt * lax.rsqrt(ms + EPS) * g
    return y * (1.0 + scale) + shift


def _silu(t):
    return t * jax.nn.sigmoid(t)


def _log_sigmoid(t):
    return jnp.minimum(t, 0.0) - jnp.log1p(jnp.exp(-jnp.abs(t)))


def _mod_kernel(c_ref, w_ref, b_ref, o_ref):
    sc = _silu(c_ref[...])
    o_ref[...] = _dot(sc.astype(BF16), w_ref[...].astype(BF16)) + b_ref[...]


def _modulation(c_all, w_ada, b_ada):
    depth, d, nd = w_ada.shape
    r = c_all.shape[0]
    tn = 1024
    return pl.pallas_call(
        _mod_kernel,
        grid=(depth, nd // tn),
        in_specs=[pl.BlockSpec((r, d), lambda l, j: (0, 0)),
                  pl.BlockSpec((None, d, tn), lambda l, j: (l, 0, j)),
                  pl.BlockSpec((None, 1, tn), lambda l, j: (l, 0, j))],
        out_specs=pl.BlockSpec((None, r, tn), lambda l, j: (l, 0, j)),
        out_shape=jax.ShapeDtypeStruct((depth, r, nd), F32),
        compiler_params=_cparams(("parallel", "parallel")),
        name="modulation",
    )(c_all, w_ada, b_ada.reshape(depth, 1, nd))


def _fold_kernel(wa_ref, wf_ref, cc_ref, sc_ref, o_ref, *, scale):
    wf = wf_ref[...]
    mc = _dot_hi(cc_ref[...], wf) * scale
    ms = _dot_hi(sc_ref[...], wf) * scale
    wa = wa_ref[...]
    o_ref[:, :D_FOURIER] = _dot_hi(wa, mc)
    o_ref[:, D_FOURIER:] = _dot_hi(wa, ms)


def _fourier_fold(wa, wf_bd, cc_bd, sc_bd, scale):
    depth, d, _ = wa.shape
    return pl.pallas_call(
        functools.partial(_fold_kernel, scale=scale),
        grid=(depth,),
        in_specs=[pl.BlockSpec((None, d, D_FOURIER), lambda l: (l, 0, 0)),
                  pl.BlockSpec((None, D_FOURIER, D_FOURIER), lambda l: (l, 0, 0)),
                  pl.BlockSpec((D_FOURIER, D_FOURIER), lambda l: (0, 0)),
                  pl.BlockSpec((D_FOURIER, D_FOURIER), lambda l: (0, 0))],
        out_specs=pl.BlockSpec((None, d, 2 * D_FOURIER), lambda l: (l, 0, 0)),
        out_shape=jax.ShapeDtypeStruct((depth, d, 2 * D_FOURIER), F32),
        compiler_params=_cparams(("parallel",)),
        name="fourier_fold",
    )(wa, wf_bd, cc_bd, sc_bd)


def _ffn_kernel(*refs, sub, d_ff, mix, split_yf, final, cast_next, split):
    x_ref, mod_ref, g_ref, win_ref, wout_ref = refs[:5]
    if cast_next:
        o_ref, nin_o_ref, nout_o_ref = refs[-3:]
        nin_ref, nout_ref = refs[-5:-3]
        nin_o_ref[...] = nin_ref[...].astype(BF16)
        nout_o_ref[...] = nout_ref[...].astype(BF16)
        rest = list(refs[5:-5])
    else:
        o_ref = refs[-1]
        rest = list(refs[5:-1])
    shift = mod_ref[3 * sub:3 * sub + 1, :]
    scale = mod_ref[3 * sub + 1:3 * sub + 2, :]
    gate = mod_ref[3 * sub + 2:3 * sub + 3, :]
    tm = x_ref.shape[0]
    rs = tm // split
    def stages(part):
        rows = slice(part * rs, (part + 1) * rs)
        v = {}

        def s_norm():
            xt = x_ref[rows, :]
            if mix:
                if split_yf:
                    yt_ref, yb_ref, att_ref, hf_ref, hb_ref, og_ref, wo_ref = rest[:7]
                    in_top = pl.program_id(1) < pl.num_programs(1) // 2
                    yf = jnp.where(in_top, yt_ref[rows, :], yb_ref[rows, :])
                else:
                    yf_ref, att_ref, hf_ref, hb_ref, og_ref, wo_ref = rest[:6]
                    yf = yf_ref[rows, :]
                mh = (og_ref[rows, :].astype(F32) * (hf_ref[rows, :] + hb_ref[rows, :])).astype(BF16)
                a0, a1 = D_FOURIER, D_FOURIER + D_ATT
                y = (_dot(yf, wo_ref[:a0, :]) + _dot(att_ref[rows, :], wo_ref[a0:a1, :])
                     + _dot(mh, wo_ref[a1:, :]))
                xt = xt + mod_ref[5:6, :] * y
            v['xt'] = xt
            v['h'] = _rms_mod(xt, g_ref[sub:sub + 1, :], shift, scale).astype(BF16)

        def s_up():
            v['gu'] = _dot(v.pop('h'), win_ref[...])

        def s_act():
            gu = v.pop('gu')
            v['act'] = (_silu(gu[:, :d_ff]) * gu[:, d_ff:]).astype(BF16)

        def s_down():
            y = v.pop('xt') + (0.5 * gate) * _dot(v.pop('act'), wout_ref[...])
            if final:
                gf_ref = rest[-1]
                ms = jnp.mean(y * y, axis=-1, keepdims=True)
                y = y * lax.rsqrt(ms + EPS) * gf_ref[...]
            o_ref[rows, :] = y

        return s_norm, s_up, s_act, s_down

    parts = [stages(part) for part in range(split)]
    for step in range(split + 3):
        for stage in range(4):
            part = step - stage
            if 0 <= part < split:
                parts[part][stage]()


def _slab_count(rows, steps):
    for nblk in range(steps, 0, -1):
        if rows % nblk == 0 and (rows // nblk) % (2 * SUBLANES) == 0:
            return nblk
    return 1


def _ffn(x, layer, sub, mod, mod_row, norm_g, w_in, w_out, mixer=None, w_o=None, g_final=None, cast_next=None):
    b, n, d = x.shape
    d_ff = w_out.shape[0]
    mix = mixer is not None
    tm = min(TM if mix else TM_FFN, n)
    nt = n // tm
    split = max(tm // FFN_PART_ROWS, 1)
    final = g_final is not None
    tok = lambda w: pl.BlockSpec((None, tm, w), lambda bi, t: (bi, t, 0))
    whole = lambda r, c: pl.BlockSpec((r, c), lambda bi, t: (0, 0), pipeline_mode=pl.Buffered(1))
    in_specs = [tok(d),
                pl.BlockSpec((None, None, N_MOD, d), lambda bi, t: (layer, mod_row(bi), 0, 0)),
                pl.BlockSpec((None, 3, d), lambda bi, t: (layer, 0, 0)),
                whole(d, 2 * d_ff), whole(d_ff, d)]
    args = [x, mod, norm_g, w_in, w_out]
    split_yf = mix and isinstance(mixer[0], tuple)
    if mix:
        if split_yf:
            hh = nt // 2
            in_specs += [pl.BlockSpec((None, tm, D_FOURIER), lambda bi, t: (bi, jnp.minimum(t, hh - 1), 0)),
                         pl.BlockSpec((None, tm, D_FOURIER), lambda bi, t: (bi, jnp.maximum(t - hh, 0), 0))]
            args += list(mixer[0])
        else:
            in_specs.append(tok(D_FOURIER))
            args.append(mixer[0])
        in_specs += [tok(D_ATT), tok(D_MLSTM), tok(D_MLSTM), tok(D_MLSTM),
                     pl.BlockSpec((None, d, d), lambda bi, t: (layer, 0, 0), pipeline_mode=pl.Buffered(1))]
        args += list(mixer[1:]) + [w_o]
    if final:
        in_specs.append(pl.BlockSpec((1, d), lambda bi, t: (0, 0)))
        args.append(g_final.reshape(1, d))
    out_specs = [tok(d)]
    out_shape = [jax.ShapeDtypeStruct((b, n, d), F32)]
    if cast_next is not None:
        nin, nout, nl, nh_ = cast_next
        steps = b * nt
        for arr, rows, cols in ((nin, d, 2 * d_ff), (nout, d_ff, d)):
            nblk = _slab_count(rows, steps)
            slab = lambda bi, t, nblk=nblk: jnp.minimum(bi * nt + t, nblk - 1)
            in_specs.append(pl.BlockSpec((None, None, rows // nblk, cols),
                                         lambda bi, t, slab=slab: (nl, nh_, slab(bi, t), 0)))
            out_specs.append(pl.BlockSpec((rows // nblk, cols), lambda bi, t, slab=slab: (slab(bi, t), 0)))
            out_shape.append(jax.ShapeDtypeStruct((rows, cols), BF16))
            args.append(arr)
    res = pl.pallas_call(
        functools.partial(_ffn_kernel, sub=sub, d_ff=d_ff, mix=mix, split_yf=split_yf, final=final, split=split,
                          cast_next=cast_next is not None),
        grid=(b, nt),
        in_specs=in_specs,
        out_specs=out_specs,
        out_shape=out_shape,
        compiler_params=_cparams(("arbitrary", "arbitrary")),
        name="ffn_mix" if mix else "ffn",
    )(*args)
    return res if cast_next is not None else res[0]


def _rope128(t, cos, sin_signed, low16):
    fwd = pltpu.roll(t, LANES - 16, axis=1)
    bwd = pltpu.roll(t, 16, axis=1)
    return t * cos + jnp.where(low16, fwd, bwd) * sin_signed


def _inproj_kernel(x_ref, xp_ref, xn_ref, mod_ref, g_ref, w_ref, wgt_ref, gbr_ref, cw_ref,
                   cos_ref, sin_ref,
                   p_ref, qt_ref, k_ref, vt_ref, mk_ref, mqt_ref, mvt_ref, og_ref, gr_ref, *, rope):
    t = pl.program_id(1)
    nt = pl.num_programs(1)
    tm = x_ref.shape[0]
    shift = mod_ref[3:4, :]
    scale = mod_ref[4:5, :]
    g = g_ref[1:2, :]
    h = _rms_mod(x_ref[...], g, shift, scale).astype(BF16)
    seg = lambda c0, c1: _dot(h, w_ref[:, c0:c1])
    lane = lax.broadcasted_iota(jnp.int32, (1, LANES), 1)
    low16 = (lane & 16) == 0
    qscale = HEAD_DIM ** -0.5 * LOG2E
    if rope:
        cos = cos_ref[...]
        sin = sin_ref[...]

    up = seg(C_P, C_Q)
    uq = seg(C_Q, C_K)
    p_ref[...] = up.astype(BF16)

    ukv = seg(C_K, C_MQ)
    for cb in range(D_ATT // LANES):
        qb = uq[:, cb * LANES:(cb + 1) * LANES]
        if rope:
            qb = _rope128(qb, cos, sin, low16)
        qt_ref[cb * LANES:(cb + 1) * LANES, :] = jnp.transpose(qb * qscale).astype(BF16)

    uqk = seg(C_MQ, C_MV)
    xh = jnp.concatenate([xp_ref[...], xn_ref[...]], axis=0)
    hh = _rms_mod(xh, g, shift, scale).astype(BF16)
    uh = _dot(hh, w_ref[:, C_MQ:C_MV])
    kb = ukv[:, :D_KV]
    if rope:
        kb = _rope128(kb, cos, sin, low16)
    k_ref[...] = kb.astype(BF16)
    vt_ref[...] = jnp.transpose(ukv[:, D_KV:]).astype(BF16)

    uvo = seg(C_MV, C_END)
    prev_row = jnp.where(t > 0, uh[SUBLANES - 1:SUBLANES, :], 0.0)
    next_row = jnp.where(t < nt - 1, uh[SUBLANES:SUBLANES + 1, :], 0.0)
    row = lax.broadcasted_iota(jnp.int32, (tm, 1), 0)
    u_prev = jnp.where(row == 0, prev_row, pltpu.roll(uqk, 1, axis=0))
    u_next = jnp.where(row == tm - 1, next_row, pltpu.roll(uqk, tm - 1, axis=0))
    cv = u_prev * cw_ref[0:1, :] + uqk * cw_ref[1:2, :] + u_next * cw_ref[2:3, :]
    qk = _silu(cv)
    mqt_ref[...] = jnp.transpose(qk[:, :D_MLSTM]).astype(BF16)
    mk_ref[...] = (qk[:, D_MLSTM:] * (MLSTM_DIM ** -0.5)).astype(BF16)

    half = N_GATES // 2
    zr = _dot_nt(wgt_ref[...], h) + gbr_ref[...]
    mvt_ref[...] = jnp.transpose(uvo[:, :D_MLSTM]).astype(BF16)
    og_ref[...] = jax.nn.sigmoid(uvo[:, D_MLSTM:]).astype(BF16)
    rowg = lax.broadcasted_iota(jnp.int32, (N_GATES, 1), 0)
    gr_ref[...] = jnp.where(rowg >= half, _log_sigmoid(zr), zr)


def _inproj(x, layer, mod, mod_row, norm_g, w_all, wg_t, gb_row, conv_w, cos128, sin128, rope):
    b, n, d = x.shape
    tm = min(TM_INPROJ, n)
    nt = n // tm
    r8 = tm // SUBLANES
    n8 = n // SUBLANES
    lay = lambda *blk: pl.BlockSpec((None,) + blk, lambda bi, t: (layer,) + (0,) * len(blk))
    tok = lambda w: pl.BlockSpec((None, tm, w), lambda bi, t: (bi, t, 0))
    trn = lambda rows: pl.BlockSpec((None, rows, tm), lambda bi, t: (bi, 0, t))
    in_specs = [tok(d),
                pl.BlockSpec((None, SUBLANES, d), lambda bi, t: (bi, jnp.maximum(t * r8 - 1, 0), 0)),
                pl.BlockSpec((None, SUBLANES, d), lambda bi, t: (bi, jnp.minimum((t + 1) * r8, n8 - 1), 0)),
                pl.BlockSpec((None, None, N_MOD, d), lambda bi, t: (layer, mod_row(bi), 0, 0)),
                lay(3, d), lay(d, C_END), lay(N_GATES, d), lay(N_GATES, 1), lay(3, 2 * D_MLSTM),
                pl.BlockSpec((tm, LANES), lambda bi, t: (t, 0)),
                pl.BlockSpec((tm, LANES), lambda bi, t: (t, 0))]
    out_specs = [tok(2 * D_FOURIER), trn(D_ATT), tok(D_KV), trn(D_KV),
                 tok(D_MLSTM), trn(D_MLSTM), trn(D_MLSTM), tok(D_MLSTM), trn(N_GATES)]
    shp = lambda w, dt: jax.ShapeDtypeStruct((b, n, w), dt)
    shpt = lambda rows, dt: jax.ShapeDtypeStruct((b, rows, n), dt)
    out_shape = [shp(2 * D_FOURIER, BF16), shpt(D_ATT, BF16), shp(D_KV, BF16), shpt(D_KV, BF16),
                 shp(D_MLSTM, BF16), shpt(D_MLSTM, BF16), shpt(D_MLSTM, BF16), shp(D_MLSTM, BF16),
                 shpt(N_GATES, F32)]
    return pl.pallas_call(
        functools.partial(_inproj_kernel, rope=rope),
        grid=(b, nt),
        in_specs=in_specs,
        out_specs=out_specs,
        out_shape=out_shape,
        compiler_params=_cparams(("parallel", "parallel")),
        name="inproj",
    )(x, x, x, mod, norm_g, w_all, wg_t, gb_row, conv_w, cos128, sin128)


def _dft_kernel(cn_ref, sn_ref, p_ref, y_ref):
    y = _dot(cn_ref[...], p_ref[:, :D_FOURIER]) + _dot(sn_ref[...], p_ref[:, D_FOURIER:])
    y_ref[...] = y.astype(BF16)


def _dft(cn, sn, p):
    b, n, _ = p.shape
    tk = min(TK_DFT, n)
    return pl.pallas_call(
        _dft_kernel,
        grid=(n // tk, b),
        in_specs=[pl.BlockSpec((tk, n), lambda i, bi: (i, 0)),
                  pl.BlockSpec((tk, n), lambda i, bi: (i, 0)),
                  pl.BlockSpec((None, n, 2 * D_FOURIER), lambda i, bi: (bi, 0, 0))],
        out_specs=pl.BlockSpec((None, tk, D_FOURIER), lambda i, bi: (bi, i, 0)),
        out_shape=jax.ShapeDtypeStruct((b, n, D_FOURIER), BF16),
        compiler_params=_cparams(("parallel", "parallel")),
        name="dft",
    )(cn, sn, p)


def _anti_identity(rows, cols, offset):
    r = lax.broadcasted_iota(jnp.int32, (rows, cols), 0)
    c = lax.broadcasted_iota(jnp.int32, (rows, cols), 1)
    return jnp.where(r + c == offset, 1.0, 0.0).astype(BF16)


def _dft_fold_kernel(p_ref, pf_ref, mid_ref):
    n = p_ref.shape[0]
    nh = n // 2
    tk = TK_DFT
    flip = _anti_identity(tk, tk, tk - 1)
    nblk = nh // tk
    rev = jnp.concatenate([_dot(flip, p_ref[n - (i + 1) * tk:n - i * tk, :]) for i in range(nblk)], axis=0)
    row = lax.broadcasted_iota(jnp.int32, (nh, 1), 0)
    mirror = jnp.where(row == 0, 0.0, pltpu.roll(rev, 1, axis=0))
    top = p_ref[0:nh, :].astype(F32)
    pf_ref[:, :D_FOURIER] = (top[:, :D_FOURIER] + mirror[:, :D_FOURIER]).astype(BF16)
    pf_ref[:, D_FOURIER:] = (top[:, D_FOURIER:] - mirror[:, D_FOURIER:]).astype(BF16)
    mid_ref[...] = jnp.broadcast_to(p_ref[nh:nh + 1, :D_FOURIER].astype(F32), mid_ref.shape)


def _dft_half_kernel(ca_ref, sa_ref, cb_ref, sb_ref, pf_ref, mid_ref, top_ref, bot_ref, cn_scr, sn_scr):
    i = pl.program_id(0)
    tk = top_ref.shape[0]
    groups = tk // DFT_K0

    @pl.when(pl.program_id(1) == 0)
    def _():
        cb = cb_ref[...]
        sb = sb_ref[...]
        for r in range(groups + 1):
            nr = DFT_K0 if r < groups else DFT_TAIL
            ca = ca_ref[pl.ds(i * groups + r, 1), :]
            sa = sa_ref[pl.ds(i * groups + r, 1), :]
            cn_scr[r * DFT_K0:r * DFT_K0 + nr, :] = (ca * cb[:nr] - sa * sb[:nr]).astype(BF16)
            sn_scr[r * DFT_K0:r * DFT_K0 + nr, :] = (sa * cb[:nr] + ca * sb[:nr]).astype(BF16)

    e = _dot(cn_scr[...], pf_ref[:, :D_FOURIER])
    o = _dot(sn_scr[...], pf_ref[:, D_FOURIER:])
    rows = lax.broadcasted_iota(jnp.int32, (tk + DFT_TAIL, 1), 0)
    mid = mid_ref[0:1, :]
    e = e + jnp.where((rows & 1) == 0, mid, -mid)
    top_ref[...] = (e + o)[:tk].astype(BF16)
    bot_ref[...] = _dot(_anti_identity(tk, tk + DFT_TAIL, tk), (e - o).astype(BF16)).astype(BF16)


def _dft_half(tables, p):
    ca, sa, cb, sb = tables
    b, n, _ = p.shape
    nh = n // 2
    tk = TK_DFT
    nt = nh // tk
    pf, mid = pl.pallas_call(
        _dft_fold_kernel,
        grid=(b,),
        in_specs=[pl.BlockSpec((None, n, 2 * D_FOURIER), lambda bi: (bi, 0, 0))],
        out_specs=[pl.BlockSpec((None, nh, 2 * D_FOURIER), lambda bi: (bi, 0, 0)),
                   pl.BlockSpec((None, SUBLANES, D_FOURIER), lambda bi: (bi, 0, 0))],
        out_shape=[jax.ShapeDtypeStruct((b, nh, 2 * D_FOURIER), BF16),
                   jax.ShapeDtypeStruct((b, SUBLANES, D_FOURIER), F32)],
        compiler_params=_cparams(("parallel",)),
        name="dft_fold",
    )(p)
    whole = lambda a: pl.BlockSpec(a.shape, lambda i, bi: (0, 0))
    return pl.pallas_call(
        _dft_half_kernel,
        grid=(nt, b),
        in_specs=[whole(ca), whole(sa), whole(cb), whole(sb),
                  pl.BlockSpec((None, nh, 2 * D_FOURIER), lambda i, bi: (bi, 0, 0)),
                  pl.BlockSpec((None, SUBLANES, D_FOURIER), lambda i, bi: (bi, 0, 0))],
        out_specs=[pl.BlockSpec((None, tk, D_FOURIER), lambda i, bi: (bi, i, 0)),
                   pl.BlockSpec((None, tk, D_FOURIER), lambda i, bi: (bi, nt - 1 - i, 0))],
        out_shape=[jax.ShapeDtypeStruct((b, nh, D_FOURIER), BF16), jax.ShapeDtypeStruct((b, nh, D_FOURIER), BF16)],
        scratch_shapes=[pltpu.VMEM((tk + DFT_TAIL, nh), BF16), pltpu.VMEM((tk + DFT_TAIL, nh), BF16)],
        compiler_params=_cparams(("parallel", "arbitrary")),
        name="dft_half",
    )(ca, sa, cb, sb, pf, mid)


def _attn_block(qt_blk, kall, vtall, biases, sinks, cl):
    blk = ATT_BLOCK
    group = ATT_HEADS // ATT_KV_HEADS
    gw = group * blk
    zq = jnp.zeros((HEAD_DIM, gw), BF16)
    cols = []
    for kv in range(ATT_KV_HEADS):
        r0 = kv * group * HEAD_DIM
        qblk = jnp.concatenate([qt_blk[r0 + i * HEAD_DIM:r0 + (i + 1) * HEAD_DIM, :] for i in range(group)], axis=1)
        qt_g = jnp.concatenate([qblk, zq] if kv == 0 else [zq, qblk], axis=0)
        s = _dot(kall, qt_g)
        sink = sinks[kv]
        if biases is not None:
            bias_p, bias_n = biases
            add = lambda sg, bias: jnp.concatenate(
                [sg[:, i * blk:(i + 1) * blk] + bias for i in range(group)], axis=1)
            segs = [s[:cl], add(s[cl:cl + blk], bias_p), s[cl + blk:cl + 2 * blk], add(s[cl + 2 * blk:], bias_n)]
        else:
            segs = [s]
        m = sink
        for sg in segs:
            m = jnp.maximum(m, jnp.max(sg, axis=0, keepdims=True))
        den = jnp.exp2(sink - m)
        ps = []
        for sg in segs:
            p = jnp.exp2(sg - m)
            den = den + jnp.sum(p, axis=0, keepdims=True)
            ps.append(p.astype(BF16))
        p_all = jnp.concatenate(ps, axis=0) if len(ps) > 1 else ps[0]
        vt_g = vtall[kv * HEAD_DIM:(kv + 1) * HEAD_DIM, :]
        ot = _dot(vt_g, p_all) * (1.0 / den)
        for cb in range(group // 2):
            pair = jnp.concatenate([ot[:, (2 * cb) * blk:(2 * cb + 1) * blk],
                                    ot[:, (2 * cb + 1) * blk:(2 * cb + 2) * blk]], axis=0)
            cols.append(jnp.transpose(pair).astype(BF16))
    return cols


def _attn_kernel(sink_ref, qt_ref, *rest, layer, local):
    if local:
        kp_ref, kc_ref, kn_ref, vp_ref, vc_ref, vn_ref, kx_ref, vxt_ref, o_ref = rest
    else:
        kx_ref, vxt_ref, o_ref = rest
    blk = ATT_BLOCK
    group = ATT_HEADS // ATT_KV_HEADS
    gw = group * blk
    cl = kx_ref.shape[0]
    nq = qt_ref.shape[1] // blk
    lane_head = lax.broadcasted_iota(jnp.int32, (1, gw), 1) // blk
    sinks = []
    for kv in range(ATT_KV_HEADS):
        sink = jnp.zeros((1, gw), F32)
        for i in range(group):
            sink = jnp.where(lane_head == i, sink_ref[layer, kv * group + i] * LOG2E, sink)
        sinks.append(sink)
    kx = kx_ref[...]
    vxt = vxt_ref[...]
    if local:
        j = pl.program_id(1)
        nb = pl.num_programs(1) * nq
        ks = lax.broadcasted_iota(jnp.int32, (blk, blk), 0)
        qi = lax.broadcasted_iota(jnp.int32, (blk, blk), 1)
        in_p = jnp.abs(ks - blk - qi) <= WINDOW
        in_n = jnp.abs(ks + blk - qi) <= WINDOW
        kloc = jnp.concatenate([kp_ref[...], kc_ref[...], kn_ref[...]], axis=0)
        vloc = jnp.concatenate([vp_ref[...], vc_ref[...], vn_ref[...]], axis=1)
    zq = jnp.zeros((HEAD_DIM, gw), BF16)

    def scores(i, kv):
        qt_blk = qt_ref[:, i * blk:(i + 1) * blk]
        r0 = kv * group * HEAD_DIM
        qblk = jnp.concatenate([qt_blk[r0 + h * HEAD_DIM:r0 + (h + 1) * HEAD_DIM, :] for h in range(group)], axis=1)
        qt_g = jnp.concatenate([qblk, zq] if kv == 0 else [zq, qblk], axis=0)
        if not local:
            return [_dot(kx, qt_g)]
        jb = j * nq + i
        bias_p = jnp.where(in_p & (jb > 0), 0.0, NEG)
        bias_n = jnp.where(in_n & (jb < nb - 1), 0.0, NEG)
        s = _dot(jnp.concatenate([kx, kloc[i * blk:(i + 3) * blk, :]], axis=0), qt_g)
        add = lambda sg, bias: jnp.concatenate([sg[:, h * blk:(h + 1) * blk] + bias for h in range(group)], axis=1)
        return [s[:cl], add(s[cl:cl + blk], bias_p), s[cl + blk:cl + 2 * blk], add(s[cl + 2 * blk:], bias_n)]

    def colmax(segs, kv):
        m = sinks[kv]
        for sg in segs:
            m = jnp.maximum(m, jnp.max(sg, axis=0, keepdims=True))
        return m

    def softmax(segs, kv, m):
        den = jnp.exp2(sinks[kv] - m)
        ps = []
        for sg in segs:
            p = jnp.exp2(sg - m)
            den = den + jnp.sum(p, axis=0, keepdims=True)
            ps.append(p.astype(BF16))
        return (jnp.concatenate(ps, axis=0) if len(ps) > 1 else ps[0]), den

    def values(i, kv, p_all, den):
        vtall = jnp.concatenate([vxt, vloc[:, i * blk:(i + 3) * blk]], axis=1) if local else vxt
        ot = _dot(vtall[kv * HEAD_DIM:(kv + 1) * HEAD_DIM, :], p_all) * (1.0 / den)
        for cb in range(group // 2):
            pair = jnp.concatenate([ot[:, (2 * cb) * blk:(2 * cb + 1) * blk],
                                    ot[:, (2 * cb + 1) * blk:(2 * cb + 2) * blk]], axis=0)
            c0 = (kv * (group // 2) + cb) * LANES
            o_ref[i * blk:(i + 1) * blk, c0:c0 + LANES] = jnp.transpose(pair).astype(BF16)

    chains = [(i, kv) for i in range(nq) for kv in range(ATT_KV_HEADS)]
    segs_q, max_q, soft_q = {}, {}, {}
    lag_max, lag_soft, lag_val = ATT_SKEW
    for step in range(len(chains) + lag_val):
        if step < len(chains):
            segs_q[step] = scores(*chains[step])
        c = step - lag_max
        if 0 <= c < len(chains):
            max_q[c] = colmax(segs_q[c], chains[c][1])
        c = step - lag_soft
        if 0 <= c < len(chains):
            soft_q[c] = softmax(segs_q.pop(c), chains[c][1], max_q.pop(c))
        c = step - lag_val
        if 0 <= c < len(chains):
            values(*chains[c], *soft_q.pop(c))


def _attention(sink, layer, qt, k, vt, kx, vxt, local):
    b, _, n = qt.shape
    cl = kx.shape[1]
    blk = ATT_BLOCK
    tq = min(TQ_ATT, n)
    nq = tq // blk
    nb = n // blk
    in_specs = [pl.BlockSpec(memory_space=pltpu.SMEM),
                pl.BlockSpec((None, D_ATT, tq), lambda bi, j: (bi, 0, j))]
    args = [sink, qt]
    if local:
        pj = lambda j: jnp.maximum(j * nq - 1, 0)
        nj = lambda j: jnp.minimum((j + 1) * nq, nb - 1)
        in_specs += [pl.BlockSpec((None, blk, D_KV), lambda bi, j: (bi, pj(j), 0)),
                     pl.BlockSpec((None, tq, D_KV), lambda bi, j: (bi, j, 0)),
                     pl.BlockSpec((None, blk, D_KV), lambda bi, j: (bi, nj(j), 0)),
                     pl.BlockSpec((None, D_KV, blk), lambda bi, j: (bi, 0, pj(j))),
                     pl.BlockSpec((None, D_KV, tq), lambda bi, j: (bi, 0, j)),
                     pl.BlockSpec((None, D_KV, blk), lambda bi, j: (bi, 0, nj(j)))]
        args += [k, k, k, vt, vt, vt]
    in_specs += [pl.BlockSpec((None, cl, D_KV), lambda bi, j: (bi, 0, 0)),
                 pl.BlockSpec((None, D_KV, cl), lambda bi, j: (bi, 0, 0))]
    args += [kx, vxt]
    return pl.pallas_call(
        functools.partial(_attn_kernel, layer=layer, local=local),
        grid=(b, n // tq),
        in_specs=in_specs,
        out_specs=pl.BlockSpec((None, tq, D_ATT), lambda bi, j: (bi, j, 0)),
        out_shape=jax.ShapeDtypeStruct((b, n, D_ATT), BF16),
        compiler_params=_cparams(("parallel", "parallel")),
        name="attention",
    )(*args)


def _mlstm_stages(d, k_ref, qt_ref, vt_ref, gr_ref, h_ref, c_scr, n_scr, m_scr, consts):
    L = MLSTM_L
    H = MLSTM_HEADS
    DH = MLSTM_DIM
    tri, valid_t, lane_h, row_h, blockdiag, nmask = consts
    r0 = d * H
    v = {}

    def stage_in():
        k = k_ref[...]
        qt = qt_ref[...]
        gr = gr_ref[...]
        cum = _dot_split3(gr, tri)
        li = gr[r0:r0 + H, :]
        a = cum[2 * H + r0:2 * H + r0 + H, :]
        a_tot = a[:, L - 1:L] if d == 0 else a[:, 0:1]
        m_in = m_scr[r0:r0 + H, 0:1]
        c_old = c_scr[d]
        n_old = n_scr[d]
        kst = jnp.concatenate([jnp.where(lane_h == hd, k, jnp.zeros_like(k)) for hd in range(H)], axis=0)
        v.update(k=k, li=li, a=a, a_tot=a_tot, m_in=m_in, c_old=c_old, n_old=n_old,
                 g_prev=a + m_in,
                 b_t=jnp.transpose(jnp.concatenate([li - a, jnp.zeros((L - H, L), F32)], axis=0)),
                 inter=_dot(jnp.concatenate([c_old, n_old], axis=0).astype(BF16), qt),
                 s_t=_dot(kst, qt))

    def stage_weights():
        p_parts, m_rows, wp_rows, den_rows = [], [], [], []
        for hd in range(H):
            d_t = jnp.where(valid_t, v['b_t'][:, hd:hd + 1] + v['a'][hd:hd + 1, :], NEG)
            g_prev = v['g_prev'][hd:hd + 1, :]
            m_t = jnp.maximum(g_prev, jnp.max(d_t, axis=0, keepdims=True))
            p_t = v['s_t'][hd * L:(hd + 1) * L, :] * jnp.exp(d_t - m_t)
            w_prev = jnp.exp(g_prev - m_t)
            den = jnp.sum(p_t, axis=0, keepdims=True) + w_prev * v['inter'][D_MLSTM + hd:D_MLSTM + hd + 1, :]
            p_parts.append(p_t.astype(BF16))
            m_rows.append(m_t)
            wp_rows.append(w_prev)
            den_rows.append(den)
        v.update(p=jnp.concatenate(p_parts, axis=0), m_rows=m_rows, wp_rows=wp_rows, den_rows=den_rows)

    def stage_out():
        vt = vt_ref[...]
        k = v['k']
        vbd = jnp.concatenate([jnp.where(row_h == hd, vt, jnp.zeros_like(vt)) for hd in range(H)], axis=1)
        num_t = _dot(vbd, v['p'])
        outs = []
        for hd in range(H):
            inv = 1.0 / jnp.maximum(jnp.abs(v['den_rows'][hd]), jnp.exp(-v['m_rows'][hd]))
            sl = slice(hd * DH, (hd + 1) * DH)
            outs.append((num_t[sl, :] + v['wp_rows'][hd] * v['inter'][sl, :]) * inv)
        h_ref[...] = jnp.transpose(jnp.concatenate(outs, axis=0))
        a_tot, m_in = v['a_tot'], v['m_in']
        dl = a_tot - v['a'] + v['li']
        m_new = jnp.maximum(a_tot + m_in, jnp.max(dl, axis=1, keepdims=True))
        w = jnp.exp(dl - m_new)
        decay = jnp.exp(a_tot + m_in - m_new)
        wexp = jnp.concatenate([jnp.broadcast_to(w[hd:hd + 1, :], (DH, L)) for hd in range(H)], axis=0)
        dcol = jnp.concatenate([jnp.broadcast_to(decay[hd:hd + 1, :], (DH, 1)) for hd in range(H)], axis=0)
        upd = _dot((vt.astype(F32) * wexp).astype(BF16), k)
        c_scr[d] = dcol * v['c_old'] + jnp.where(blockdiag, upd, 0.0)
        n_upd = _dot(jnp.concatenate([w, jnp.zeros((H, L), F32)], axis=0).astype(BF16), k)
        decay8 = jnp.concatenate([decay, jnp.zeros((H, 1), F32)], axis=0)
        n_scr[d] = jnp.where(nmask, decay8 * v['n_old'] + n_upd, 0.0)
        m_scr[r0:r0 + H, :] = jnp.broadcast_to(m_new, (H, LANES))

    return stage_in, stage_weights, stage_out


def _mlstm_kernel(kf_ref, qtf_ref, vtf_ref, grf_ref, kb_ref, qtb_ref, vtb_ref, grb_ref,
                  c0_ref, n0_ref, m0_ref, hf_ref, hb_ref, cfin_ref, nfin_ref, mfin_ref, c_scr, n_scr, m_scr):
    c = pl.program_id(0)
    nc = pl.num_programs(0)
    L = MLSTM_L
    H = MLSTM_HEADS
    nb = kf_ref.shape[0]

    @pl.when(c == 0)
    def _():
        c_scr[...] = c0_ref[...]
        n_scr[...] = n0_ref[...]
        m_scr[...] = m0_ref[...]

    ri = lax.broadcasted_iota(jnp.int32, (L, L), 0)
    ci = lax.broadcasted_iota(jnp.int32, (L, L), 1)
    upper = ci >= ri
    lower = ci <= ri
    triu = jnp.where(upper, 1.0, 0.0).astype(BF16)
    tril = jnp.where(lower, 1.0, 0.0).astype(BF16)
    lane_h = lax.broadcasted_iota(jnp.int32, (1, D_MLSTM), 1) // MLSTM_DIM
    row_h = lax.broadcasted_iota(jnp.int32, (D_MLSTM, 1), 0) // MLSTM_DIM
    blockdiag = row_h == lane_h
    nmask = lax.broadcasted_iota(jnp.int32, (2 * H, 1), 0) == lane_h
    chains = []
    for bi in range(nb):
        st = (c_scr.at[bi], n_scr.at[bi], m_scr.at[bi])
        chains.append(_mlstm_stages(0, kf_ref.at[bi], qtf_ref.at[bi], vtf_ref.at[bi], grf_ref.at[bi],
                                    hf_ref.at[bi], *st, (triu, upper, lane_h, row_h, blockdiag, nmask)))
        chains.append(_mlstm_stages(1, kb_ref.at[bi], qtb_ref.at[bi], vtb_ref.at[bi], grb_ref.at[bi],
                                    hb_ref.at[bi], *st, (tril, lower, lane_h, row_h, blockdiag, nmask)))
    for step in range(len(chains) + 2):
        for stage in range(3):
            ch = step - stage
            if 0 <= ch < len(chains):
                chains[ch][stage]()

    @pl.when(c == nc - 1)
    def _():
        cfin_ref[...] = c_scr[...]
        nfin_ref[...] = n_scr[...]
        mfin_ref[...] = m_scr[...]


def _mlstm(mk, mqt, mvt, grow, state):
    b, n, _ = mk.shape
    L = MLSTM_L
    nc = n // L
    nh = 2 * MLSTM_HEADS
    tok = lambda cm: pl.BlockSpec((b, L, D_MLSTM), lambda c: (0, cm(c), 0))
    tr = lambda rows, cm: pl.BlockSpec((b, rows, L), lambda c: (0, 0, cm(c)))
    fw = lambda c: c
    bw = lambda c: nc - 1 - c
    cspec = pl.BlockSpec((b, 2, D_MLSTM, D_MLSTM), lambda c: (0, 0, 0, 0))
    nspec = pl.BlockSpec((b, 2, nh, D_MLSTM), lambda c: (0, 0, 0, 0))
    mspec = pl.BlockSpec((b, nh, LANES), lambda c: (0, 0, 0))
    return pl.pallas_call(
        _mlstm_kernel,
        grid=(nc,),
        in_specs=[tok(fw), tr(D_MLSTM, fw), tr(D_MLSTM, fw), tr(N_GATES, fw),
                  tok(bw), tr(D_MLSTM, bw), tr(D_MLSTM, bw), tr(N_GATES, bw), cspec, nspec, mspec],
        out_specs=[tok(fw), tok(bw), cspec, nspec, mspec],
        out_shape=[jax.ShapeDtypeStruct((b, n, D_MLSTM), F32), jax.ShapeDtypeStruct((b, n, D_MLSTM), F32),
                   jax.ShapeDtypeStruct((b, 2, D_MLSTM, D_MLSTM), F32),
                   jax.ShapeDtypeStruct((b, 2, nh, D_MLSTM), F32),
                   jax.ShapeDtypeStruct((b, nh, LANES), F32)],
        scratch_shapes=[pltpu.VMEM((b, 2, D_MLSTM, D_MLSTM), F32), pltpu.VMEM((b, 2, nh, D_MLSTM), F32),
                        pltpu.VMEM((b, nh, LANES), F32)],
        compiler_params=_cparams(("arbitrary",)),
        name="mlstm",
    )(mk, mqt, mvt, grow, mk, mqt, mvt, grow, *state)


def _rope_tables(n):
    rows = n // GRID_W
    row = jnp.broadcast_to(jnp.arange(rows, dtype=F32)[:, None], (rows, GRID_W)).reshape(n)
    colp = jnp.broadcast_to(jnp.arange(GRID_W, dtype=F32)[None, :], (rows, GRID_W)).reshape(n)
    nf = HEAD_DIM // 4
    inv = ROPE_BASE ** (-jnp.arange(nf, dtype=F32) / nf)
    ar = row[:, None] * inv
    ac = colp[:, None] * inv
    ang = jnp.concatenate([ar, ar, ac, ac], axis=-1)
    cos = jnp.cos(ang)
    sin = jnp.sin(ang)
    sign = jnp.where((jnp.arange(HEAD_DIM) & 16) == 0, -1.0, 1.0).astype(F32)
    reps = LANES // HEAD_DIM
    return jnp.tile(cos, (1, reps)), jnp.tile(sin * sign, (1, reps))


def _dft_factor_tables(n):
    nk1 = n // DFT_K0
    j = jnp.arange(n, dtype=jnp.int32)
    k1 = jnp.arange(nk1, dtype=jnp.int32) * DFT_K0
    k0 = jnp.arange(DFT_K0, dtype=jnp.int32)
    w = 2.0 * math.pi / n
    ang_a = ((k1[:, None] * j[None, :]) % n).astype(F32) * w
    ang_b = ((k0[:, None] * j[None, :]) % n).astype(F32) * w
    return jnp.cos(ang_a), jnp.sin(ang_a), jnp.cos(ang_b), jnp.sin(ang_b)


def _dft_half_tables(n):
    nh = n // 2
    nk1 = -(-(nh // DFT_K0 + 1) // SUBLANES) * SUBLANES
    j = jnp.arange(nh, dtype=jnp.int32)
    k1 = jnp.arange(nk1, dtype=jnp.int32) * DFT_K0
    k0 = jnp.arange(DFT_K0, dtype=jnp.int32)
    w = 2.0 * math.pi / n
    ang_a = ((k1[:, None] * j[None, :]) % n).astype(F32) * w
    ang_b = ((k0[:, None] * j[None, :]) % n).astype(F32) * w
    return jnp.cos(ang_a), jnp.sin(ang_a), jnp.cos(ang_b), jnp.sin(ang_b)


def _dft_tables(n):
    ca, sa, cb, sb = _dft_factor_tables(n)
    cn = (ca[:, None, :] * cb[None] - sa[:, None, :] * sb[None]).reshape(n, n)
    sn = (sa[:, None, :] * cb[None] + ca[:, None, :] * sb[None]).reshape(n, n)
    return cn.astype(BF16), sn.astype(BF16)


def _blockdiag(blocks):
    g, c = blocks.shape[-3], blocks.shape[-1]
    eye = jnp.eye(g, dtype=blocks.dtype)
    out = jnp.einsum('...gce,gh->...gche', blocks, eye)
    return out.reshape(blocks.shape[:-3] + (g * c, g * c))


def _channel_dft_blockdiag():
    e = jnp.arange(FOURIER_CH, dtype=jnp.int32)
    ang = ((e[:, None] * e[None, :]) % FOURIER_CH).astype(F32) * (2.0 * math.pi / FOURIER_CH)
    reps = (FOURIER_GROUPS, 1, 1)
    return _blockdiag(jnp.tile(jnp.cos(ang)[None], reps)), _blockdiag(jnp.tile(-jnp.sin(ang)[None], reps))


def kernel(x, c, ctx, c_ctx, w_ada, b_ada, norm_g, w_ffn_in, w_ffn_out, w_in, w_out, w_fourier, attn_sink,
           conv_qk, b_gate_i, b_gate_f, g_final):
    b, n, d = x.shape
    cl = ctx.shape[1]
    depth = w_ada.shape[0]
    nh = 2 * MLSTM_HEADS

    rows = -(-(b + 1) // SUBLANES) * SUBLANES
    c_all = jnp.zeros((rows, d), F32).at[:b].set(c).at[b].set(c_ctx)
    mod = _modulation(c_all, w_ada, b_ada).reshape(depth, rows, N_MOD, d)
    x_row = lambda bi: bi
    c_row = lambda bi: b

    offs = [0]
    for s in (D_FOURIER, D_ATT, D_KV, D_KV, D_MLSTM, D_MLSTM, D_MLSTM, D_MLSTM, nh, nh):
        offs.append(offs[-1] + s)
    w_a = w_in[:, :, offs[0]:offs[1]]
    w_gate = w_in[:, :, offs[8]:offs[10]]
    cc_bd, sc_bd = _channel_dft_blockdiag()
    w_p = _fourier_fold(w_a, _blockdiag(w_fourier), cc_bd, sc_bd, 1.0 / math.sqrt(n * FOURIER_CH))
    w_p_ctx = w_p * math.sqrt(n / cl)
    w_mid = w_in[:, :, offs[1]:offs[8]]
    w_all = jnp.concatenate([w_p, w_mid], axis=-1).astype(BF16)
    w_all_ctx = jnp.concatenate([w_p_ctx, w_mid], axis=-1).astype(BF16)
    wg_t = jnp.swapaxes(w_gate, 1, 2).astype(BF16)
    gb_row = jnp.concatenate([b_gate_i.reshape(depth, nh), b_gate_f.reshape(depth, nh)], axis=-1)[:, :, None]
    wf = (w_ffn_in[0, 0].astype(BF16), w_ffn_out[0, 0].astype(BF16))
    wo = w_out.astype(BF16)

    cos128, sin128 = _rope_tables(n)
    half_spectrum = n % (2 * TK_DFT) == 0 and TK_DFT == TM
    dft_x = _dft_half_tables(n) if half_spectrum else _dft_tables(n)
    cnc, snc = _dft_tables(cl)
    zeros_c = jnp.zeros((cl, LANES), F32)
    state0 = (jnp.zeros((b, 2, D_MLSTM, D_MLSTM), F32), jnp.zeros((b, 2, nh, D_MLSTM), F32),
              jnp.zeros((b, nh, LANES), F32))

    xc = ctx
    for l in range(depth):
        last = l == depth - 1
        x, *wf_next = _ffn(x, l, 0, mod, x_row, norm_g, *wf, cast_next=(w_ffn_in, w_ffn_out, l, 1))
        xc = _ffn(xc, l, 0, mod, c_row, norm_g, *wf)
        wf = wf_next

        px, qtx, kx, vtx, mkx, mqtx, mvtx, ogx, grx = _inproj(
            x, l, mod, x_row, norm_g, w_all, wg_t, gb_row, conv_qk, cos128, sin128, True)
        pc, qtc, kc, vtc, mkc, mqtc, mvtc, ogc, grc = _inproj(
            xc, l, mod, c_row, norm_g, w_all_ctx, wg_t, gb_row, conv_qk, zeros_c, zeros_c, False)

        hcf, hcb, *state_c = _mlstm(mkc, mqtc, mvtc, grc, state0)
        hxf, hxb, *_ = _mlstm(mkx, mqtx, mvtx, grx, state_c)
        att_x = _attention(attn_sink, l, qtx, kx, vtx, kc, vtc, True)
        yf_x = tuple(_dft_half(dft_x, px)) if half_spectrum else _dft(*dft_x, px)
        mix_x = (yf_x, att_x, hxf, hxb, ogx)
        if last:
            x = _ffn(x, l, 2, mod, x_row, norm_g, *wf, mixer=mix_x, w_o=wo, g_final=g_final)
        else:
            x, *wf_next = _ffn(x, l, 2, mod, x_row, norm_g, *wf, mixer=mix_x, w_o=wo,
                               cast_next=(w_ffn_in, w_ffn_out, l + 1, 0))
            att_c = _attention(attn_sink, l, qtc, None, None, kc, vtc, False)
            yf_c = _dft(cnc, snc, pc)
            xc = _ffn(xc, l, 2, mod, c_row, norm_g, *wf, mixer=(yf_c, att_c, hcf, hcb, ogc), w_o=wo)
            wf = wf_next
    return x
```

```python
import functools
import math

import jax
import jax.numpy as jnp
from jax import lax
from jax.experimental import pallas as pl
from jax.experimental.pallas import tpu as pltpu

F32 = jnp.float32
BF16 = jnp.bfloat16
HI = lax.Precision.HIGHEST

GRID_W = 64
EPS = 1e-6
NEG = -1e30
LOG2E = math.log2(math.e)
FOURIER_GROUPS = 4
FOURIER_CH = 64
D_FOURIER = FOURIER_GROUPS * FOURIER_CH
HEAD_DIM = 64
ATT_HEADS = 8
ATT_KV_HEADS = 2
D_ATT = ATT_HEADS * HEAD_DIM
D_KV = ATT_KV_HEADS * HEAD_DIM
ATT_BLOCK = 128
WINDOW = 128
ROPE_BASE = 10000.0
MLSTM_HEADS = 4
MLSTM_DIM = 64
D_MLSTM = MLSTM_HEADS * MLSTM_DIM
N_MOD = 9
N_GATES = 4 * MLSTM_HEADS

LANES = 128
SUBLANES = 8
VMEM_LIMIT = 56 * 1024 * 1024

TM = 512
TM_INPROJ = 1024
TM_FFN = 1024
FFN_PART_ROWS = 256
TK_DFT = 512
DFT_K0 = 64
DFT_TAIL = 16
TQ_ATT = 1024
ATT_SKEW = (1, 2, 3)
MLSTM_L = 128
MLSTM_CHUNKS_PER_STEP = 1

C_P = 0
C_Q = C_P + 2 * D_FOURIER
C_K = C_Q + D_ATT
C_V = C_K + D_KV
C_MQ = C_V + D_KV
C_MK = C_MQ + D_MLSTM
C_MV = C_MK + D_MLSTM
C_MO = C_MV + D_MLSTM
C_END = C_MO + D_MLSTM


def _cparams(sem):
    return pltpu.CompilerParams(dimension_semantics=sem, vmem_limit_bytes=VMEM_LIMIT)


def _dot(a, b):
    return jnp.dot(a, b, preferred_element_type=F32)


def _dot_nt(a, b):
    return lax.dot_general(a, b, (((1,), (1,)), ((), ())), preferred_element_type=F32)


def _dot_hi(a, b):
    return jnp.dot(a, b, preferred_element_type=F32, precision=HI)


def _dot_split3(x, onehot):
    rows = x.shape[0]
    hi = x.astype(BF16)
    r1 = x - hi.astype(F32)
    mid = r1.astype(BF16)
    lo = (r1 - mid.astype(F32)).astype(BF16)
    y = _dot(jnp.concatenate([hi, mid, lo], axis=0), onehot)
    return y[0:rows] + y[rows:2 * rows] + y[2 * rows:3 * rows]


def _rms_mod(xt, g, shift, scale):
    ms = jnp.mean(xt * xt, axis=-1, keepdims=True)
    y = xt * lax.rsqrt(ms + EPS) * g
    return y * (1.0 + scale) + shift


def _silu(t):
    return t * jax.nn.sigmoid(t)


def _log_sigmoid(t):
    return jnp.minimum(t, 0.0) - jnp.log1p(jnp.exp(-jnp.abs(t)))


def _mod_kernel(c_ref, w_ref, b_ref, o_ref):
    sc = _silu(c_ref[...])
    o_ref[...] = _dot(sc.astype(BF16), w_ref[...].astype(BF16)) + b_ref[...]


def _modulation(c_all, w_ada, b_ada):
    depth, d, nd = w_ada.shape
    r = c_all.shape[0]
    tn = 1024
    return pl.pallas_call(
        _mod_kernel,
        grid=(depth, nd // tn),
        in_specs=[pl.BlockSpec((r, d), lambda l, j: (0, 0)),
                  pl.BlockSpec((None, d, tn), lambda l, j: (l, 0, j)),
                  pl.BlockSpec((None, 1, tn), lambda l, j: (l, 0, j))],
        out_specs=pl.BlockSpec((None, r, tn), lambda l, j: (l, 0, j)),
        out_shape=jax.ShapeDtypeStruct((depth, r, nd), F32),
        compiler_params=_cparams(("parallel", "parallel")),
        name="modulation",
    )(c_all, w_ada, b_ada.reshape(depth, 1, nd))


def _fold_kernel(win_ref, wf_ref, cc_ref, sc_ref, o_ref, oc_ref, rest_ref, *, scale, scale_ctx):
    rest_ref[...] = win_ref[:, D_FOURIER:D_FOURIER + C_END - C_Q].astype(BF16)
    wf = wf_ref[...]
    mc = _dot_hi(cc_ref[...], wf)
    ms = _dot_hi(sc_ref[...], wf)
    wa = win_ref[:, :D_FOURIER]
    wc = _dot_hi(wa, mc)
    ws = _dot_hi(wa, ms)
    o_ref[:, :D_FOURIER] = (wc * scale).astype(BF16)
    o_ref[:, D_FOURIER:] = (ws * scale).astype(BF16)
    oc_ref[:, :D_FOURIER] = (wc * scale_ctx).astype(BF16)
    oc_ref[:, D_FOURIER:] = (ws * scale_ctx).astype(BF16)


def _fourier_fold(w_in, wf_bd, cc_bd, sc_bd, scale, scale_ctx):
    depth, d, d_in = w_in.shape
    out = pl.BlockSpec((None, d, 2 * D_FOURIER), lambda l: (l, 0, 0))
    n_rest = C_END - C_Q
    return pl.pallas_call(
        functools.partial(_fold_kernel, scale=scale, scale_ctx=scale_ctx),
        grid=(depth,),
        in_specs=[pl.BlockSpec((None, d, d_in), lambda l: (l, 0, 0)),
                  pl.BlockSpec((None, D_FOURIER, D_FOURIER), lambda l: (l, 0, 0)),
                  pl.BlockSpec((D_FOURIER, D_FOURIER), lambda l: (0, 0)),
                  pl.BlockSpec((D_FOURIER, D_FOURIER), lambda l: (0, 0))],
        out_specs=[out, out, pl.BlockSpec((None, d, n_rest), lambda l: (l, 0, 0))],
        out_shape=[jax.ShapeDtypeStruct((depth, d, 2 * D_FOURIER), BF16)] * 2
        + [jax.ShapeDtypeStruct((depth, d, n_rest), BF16)],
        compiler_params=_cparams(("parallel",)),
        name="fourier_fold",
    )(w_in, wf_bd, cc_bd, sc_bd)


def _ffn_kernel(*refs, sub, d_ff, mix, split_yf, final, cast_next, split):
    x_ref, mod_ref, g_ref, win_ref, wout_ref = refs[:5]
    if cast_next:
        o_ref, nin_o_ref, nout_o_ref = refs[-3:]
        nin_ref, nout_ref = refs[-5:-3]
        nin_o_ref[...] = nin_ref[...].astype(BF16)
        nout_o_ref[...] = nout_ref[...].astype(BF16)
        rest = list(refs[5:-5])
    else:
        o_ref = refs[-1]
        rest = list(refs[5:-1])
    shift = mod_ref[3 * sub:3 * sub + 1, :]
    scale = mod_ref[3 * sub + 1:3 * sub + 2, :]
    gate = mod_ref[3 * sub + 2:3 * sub + 3, :]
    tm = x_ref.shape[0]
    rs = tm // split
    def stages(part):
        rows = slice(part * rs, (part + 1) * rs)
        v = {}

        def s_norm():
            xt = x_ref[rows, :]
            if mix:
                if split_yf:
                    yt_ref, yb_ref, att_ref, hf_ref, hb_ref, og_ref, wo_ref = rest[:7]
                    in_top = pl.program_id(1) < pl.num_programs(1) // 2
                    yf = jnp.where(in_top, yt_ref[rows, :], yb_ref[rows, :])
                else:
                    yf_ref, att_ref, hf_ref, hb_ref, og_ref, wo_ref = rest[:6]
                    yf = yf_ref[rows, :]
                mh = (og_ref[rows, :].astype(F32) * (hf_ref[rows, :] + hb_ref[rows, :])).astype(BF16)
                a0, a1 = D_FOURIER, D_FOURIER + D_ATT
                y = (_dot(yf, wo_ref[:a0, :]) + _dot(att_ref[rows, :], wo_ref[a0:a1, :])
                     + _dot(mh, wo_ref[a1:, :]))
                xt = xt + mod_ref[5:6, :] * y
            v['xt'] = xt
            v['h'] = _rms_mod(xt, g_ref[sub:sub + 1, :], shift, scale).astype(BF16)

        def s_up():
            v['gu'] = _dot(v.pop('h'), win_ref[...])

        def s_act():
            gu = v.pop('gu')
            v['act'] = (_silu(gu[:, :d_ff]) * gu[:, d_ff:]).astype(BF16)

        def s_down():
            y = v.pop('xt') + (0.5 * gate) * _dot(v.pop('act'), wout_ref[...])
            if final:
                gf_ref = rest[-1]
                ms = jnp.mean(y * y, axis=-1, keepdims=True)
                y = y * lax.rsqrt(ms + EPS) * gf_ref[...]
            o_ref[rows, :] = y

        return s_norm, s_up, s_act, s_down

    parts = [stages(part) for part in range(split)]
    for step in range(split + 3):
        for stage in range(4):
            part = step - stage
            if 0 <= part < split:
                parts[part][stage]()


def _slab_count(rows, steps):
    for nblk in range(steps, 0, -1):
        if rows % nblk == 0 and (rows // nblk) % (2 * SUBLANES) == 0:
            return nblk
    return 1


def _ffn(x, layer, sub, mod, mod_row, norm_g, w_in, w_out, mixer=None, w_o=None, g_final=None, cast_next=None):
    b, n, d = x.shape
    d_ff = w_out.shape[0]
    mix = mixer is not None
    tm = min(TM if mix else TM_FFN, n)
    nt = n // tm
    split = max(tm // FFN_PART_ROWS, 1)
    final = g_final is not None
    tok = lambda w: pl.BlockSpec((None, tm, w), lambda bi, t: (bi, t, 0))
    whole = lambda r, c: pl.BlockSpec((r, c), lambda bi, t: (0, 0), pipeline_mode=pl.Buffered(1))
    in_specs = [tok(d),
                pl.BlockSpec((None, None, N_MOD, d), lambda bi, t: (layer, mod_row(bi), 0, 0)),
                pl.BlockSpec((None, 3, d), lambda bi, t: (layer, 0, 0)),
                whole(d, 2 * d_ff), whole(d_ff, d)]
    args = [x, mod, norm_g, w_in, w_out]
    split_yf = mix and isinstance(mixer[0], tuple)
    if mix:
        if split_yf:
            hh = nt // 2
            in_specs += [pl.BlockSpec((None, tm, D_FOURIER), lambda bi, t: (bi, jnp.minimum(t, hh - 1), 0)),
                         pl.BlockSpec((None, tm, D_FOURIER), lambda bi, t: (bi, jnp.maximum(t - hh, 0), 0))]
            args += list(mixer[0])
        else:
            in_specs.append(tok(D_FOURIER))
            args.append(mixer[0])
        in_specs += [tok(D_ATT), tok(D_MLSTM), tok(D_MLSTM), tok(D_MLSTM),
                     pl.BlockSpec((None, d, d), lambda bi, t: (layer, 0, 0), pipeline_mode=pl.Buffered(1))]
        args += list(mixer[1:]) + [w_o]
    if final:
        in_specs.append(pl.BlockSpec((1, d), lambda bi, t: (0, 0)))
        args.append(g_final.reshape(1, d))
    out_specs = [tok(d)]
    out_shape = [jax.ShapeDtypeStruct((b, n, d), F32)]
    if cast_next is not None:
        nin, nout, nl, nh_ = cast_next
        steps = b * nt
        for arr, rows, cols in ((nin, d, 2 * d_ff), (nout, d_ff, d)):
            nblk = _slab_count(rows, steps)
            slab = lambda bi, t, nblk=nblk: jnp.minimum(bi * nt + t, nblk - 1)
            in_specs.append(pl.BlockSpec((None, None, rows // nblk, cols),
                                         lambda bi, t, slab=slab: (nl, nh_, slab(bi, t), 0)))
            out_specs.append(pl.BlockSpec((rows // nblk, cols), lambda bi, t, slab=slab: (slab(bi, t), 0)))
            out_shape.append(jax.ShapeDtypeStruct((rows, cols), BF16))
            args.append(arr)
    res = pl.pallas_call(
        functools.partial(_ffn_kernel, sub=sub, d_ff=d_ff, mix=mix, split_yf=split_yf, final=final, split=split,
                          cast_next=cast_next is not None),
        grid=(b, nt),
        in_specs=in_specs,
        out_specs=out_specs,
        out_shape=out_shape,
        compiler_params=_cparams(("arbitrary", "arbitrary")),
        name="ffn_mix" if mix else "ffn",
    )(*args)
    return res if cast_next is not None else res[0]


def _store_chunks(ref, t):
    for c in range(ref.shape[0]):
        ref[c] = t[:, c * MLSTM_L:(c + 1) * MLSTM_L]


def _rope128(t, cos, sin_signed, low16):
    fwd = pltpu.roll(t, LANES - 16, axis=1)
    bwd = pltpu.roll(t, 16, axis=1)
    return t * cos + jnp.where(low16, fwd, bwd) * sin_signed


def _inproj_kernel(x_ref, xp_ref, xn_ref, mod_ref, g_ref, wp_ref, w_ref, wgt_ref, gbr_ref, cw_ref,
                   cos_ref, sin_ref,
                   p_ref, qt_ref, k_ref, vt_ref, mk_ref, mqt_ref, mvt_ref, og_ref, gr_ref, *, rope):
    t = pl.program_id(1)
    nt = pl.num_programs(1)
    tm = x_ref.shape[0]
    shift = mod_ref[3:4, :]
    scale = mod_ref[4:5, :]
    g = g_ref[1:2, :]
    h = _rms_mod(x_ref[...], g, shift, scale).astype(BF16)
    seg = lambda c0, c1: _dot(h, w_ref[:, c0 - C_Q:c1 - C_Q])
    lane = lax.broadcasted_iota(jnp.int32, (1, LANES), 1)
    low16 = (lane & 16) == 0
    qscale = HEAD_DIM ** -0.5 * LOG2E
    if rope:
        cos = cos_ref[...]
        sin = sin_ref[...]

    up = _dot(h, wp_ref[...])
    uq = seg(C_Q, C_K)
    p_ref[...] = up.astype(BF16)

    ukv = seg(C_K, C_MQ)
    for cb in range(D_ATT // LANES):
        qb = uq[:, cb * LANES:(cb + 1) * LANES]
        if rope:
            qb = _rope128(qb, cos, sin, low16)
        qt_ref[cb * LANES:(cb + 1) * LANES, :] = jnp.transpose(qb * qscale).astype(BF16)

    uqk = seg(C_MQ, C_MV)
    xh = jnp.concatenate([xp_ref[...], xn_ref[...]], axis=0)
    hh = _rms_mod(xh, g, shift, scale).astype(BF16)
    uh = _dot(hh, w_ref[:, C_MQ - C_Q:C_MV - C_Q])
    kb = ukv[:, :D_KV]
    if rope:
        kb = _rope128(kb, cos, sin, low16)
    k_ref[...] = kb.astype(BF16)
    vt_ref[...] = jnp.transpose(ukv[:, D_KV:]).astype(BF16)

    uvo = seg(C_MV, C_END)
    prev_row = jnp.where(t > 0, uh[SUBLANES - 1:SUBLANES, :], 0.0)
    next_row = jnp.where(t < nt - 1, uh[SUBLANES:SUBLANES + 1, :], 0.0)
    row = lax.broadcasted_iota(jnp.int32, (tm, 1), 0)
    u_prev = jnp.where(row == 0, prev_row, pltpu.roll(uqk, 1, axis=0))
    u_next = jnp.where(row == tm - 1, next_row, pltpu.roll(uqk, tm - 1, axis=0))
    cv = u_prev * cw_ref[0:1, :] + uqk * cw_ref[1:2, :] + u_next * cw_ref[2:3, :]
    qk = _silu(cv)
    _store_chunks(mqt_ref, jnp.transpose(qk[:, :D_MLSTM]).astype(BF16))
    mk_ref[...] = (qk[:, D_MLSTM:] * (MLSTM_DIM ** -0.5)).astype(BF16)

    half = N_GATES // 2
    zr = _dot_nt(wgt_ref[...], h) + gbr_ref[...]
    _store_chunks(mvt_ref, jnp.transpose(uvo[:, :D_MLSTM]).astype(BF16))
    og_ref[...] = jax.nn.sigmoid(uvo[:, D_MLSTM:]).astype(BF16)
    rowg = lax.broadcasted_iota(jnp.int32, (N_GATES, 1), 0)
    gr_ref[...] = jnp.where(rowg >= half, _log_sigmoid(zr), zr)


def _inproj(x, layer, mod, mod_row, norm_g, w_p, w_rest, wg_t, gb_row, conv_w, cos128, sin128, rope):
    b, n, d = x.shape
    tm = min(TM_INPROJ, n)
    nt = n // tm
    r8 = tm // SUBLANES
    n8 = n // SUBLANES
    lay = lambda *blk: pl.BlockSpec((None,) + blk, lambda bi, t: (layer,) + (0,) * len(blk))
    tok = lambda w: pl.BlockSpec((None, tm, w), lambda bi, t: (bi, t, 0))
    trn = lambda rows: pl.BlockSpec((None, rows, tm), lambda bi, t: (bi, 0, t))
    in_specs = [tok(d),
                pl.BlockSpec((None, SUBLANES, d), lambda bi, t: (bi, jnp.maximum(t * r8 - 1, 0), 0)),
                pl.BlockSpec((None, SUBLANES, d), lambda bi, t: (bi, jnp.minimum((t + 1) * r8, n8 - 1), 0)),
                pl.BlockSpec((None, None, N_MOD, d), lambda bi, t: (layer, mod_row(bi), 0, 0)),
                lay(3, d), lay(d, C_Q), lay(d, C_END - C_Q), lay(N_GATES, d), lay(N_GATES, 1), lay(3, 2 * D_MLSTM),
                pl.BlockSpec((tm, LANES), lambda bi, t: (t, 0)),
                pl.BlockSpec((tm, LANES), lambda bi, t: (t, 0))]
    chk = pl.BlockSpec((None, tm // MLSTM_L, D_MLSTM, MLSTM_L), lambda bi, t: (bi, t, 0, 0))
    chk_shape = jax.ShapeDtypeStruct((b, n // MLSTM_L, D_MLSTM, MLSTM_L), BF16)
    out_specs = [tok(2 * D_FOURIER), trn(D_ATT), tok(D_KV), trn(D_KV),
                 tok(D_MLSTM), chk, chk, tok(D_MLSTM), trn(N_GATES)]
    shp = lambda w, dt: jax.ShapeDtypeStruct((b, n, w), dt)
    shpt = lambda rows, dt: jax.ShapeDtypeStruct((b, rows, n), dt)
    out_shape = [shp(2 * D_FOURIER, BF16), shpt(D_ATT, BF16), shp(D_KV, BF16), shpt(D_KV, BF16),
                 shp(D_MLSTM, BF16), chk_shape, chk_shape, shp(D_MLSTM, BF16),
                 shpt(N_GATES, F32)]
    return pl.pallas_call(
        functools.partial(_inproj_kernel, rope=rope),
        grid=(b, nt),
        in_specs=in_specs,
        out_specs=out_specs,
        out_shape=out_shape,
        compiler_params=_cparams(("parallel", "parallel")),
        name="inproj",
    )(x, x, x, mod, norm_g, w_p, w_rest, wg_t, gb_row, conv_w, cos128, sin128)


def _dft_kernel(cn_ref, sn_ref, p_ref, y_ref):
    y = _dot(cn_ref[...], p_ref[:, :D_FOURIER]) + _dot(sn_ref[...], p_ref[:, D_FOURIER:])
    y_ref[...] = y.astype(BF16)


def _dft(cn, sn, p):
    b, n, _ = p.shape
    tk = min(TK_DFT, n)
    return pl.pallas_call(
        _dft_kernel,
        grid=(n // tk, b),
        in_specs=[pl.BlockSpec((tk, n), lambda i, bi: (i, 0)),
                  pl.BlockSpec((tk, n), lambda i, bi: (i, 0)),
                  pl.BlockSpec((None, n, 2 * D_FOURIER), lambda i, bi: (bi, 0, 0))],
        out_specs=pl.BlockSpec((None, tk, D_FOURIER), lambda i, bi: (bi, i, 0)),
        out_shape=jax.ShapeDtypeStruct((b, n, D_FOURIER), BF16),
        compiler_params=_cparams(("parallel", "parallel")),
        name="dft",
    )(cn, sn, p)


def _anti_identity(rows, cols, offset):
    r = lax.broadcasted_iota(jnp.int32, (rows, cols), 0)
    c = lax.broadcasted_iota(jnp.int32, (rows, cols), 1)
    return jnp.where(r + c == offset, 1.0, 0.0).astype(BF16)


def _dft_fold(p_ref, pf_ref, mid_ref):
    n = p_ref.shape[0]
    nh = n // 2
    tk = TK_DFT
    flip = _anti_identity(tk, tk, tk - 1)
    nblk = nh // tk
    rev = jnp.concatenate([_dot(flip, p_ref[n - (i + 1) * tk:n - i * tk, :]) for i in range(nblk)], axis=0)
    row = lax.broadcasted_iota(jnp.int32, (nh, 1), 0)
    mirror = jnp.where(row == 0, 0.0, pltpu.roll(rev, 1, axis=0))
    top = p_ref[0:nh, :].astype(F32)
    pf_ref[:, :D_FOURIER] = (top[:, :D_FOURIER] + mirror[:, :D_FOURIER]).astype(BF16)
    pf_ref[:, D_FOURIER:] = (top[:, D_FOURIER:] - mirror[:, D_FOURIER:]).astype(BF16)
    mid_ref[...] = jnp.broadcast_to(p_ref[nh:nh + 1, :D_FOURIER].astype(F32), mid_ref.shape)


def _dft_half_kernel(ca_ref, sa_ref, cb_ref, sb_ref, p_ref, top_ref, bot_ref, cn_scr, sn_scr, pf_scr, mid_scr):
    i = pl.program_id(0)
    bi = pl.program_id(1)
    tk = top_ref.shape[0]
    groups = tk // DFT_K0

    @pl.when(i == 0)
    def _():
        _dft_fold(p_ref, pf_scr.at[bi], mid_scr.at[bi])

    pf_ref = pf_scr.at[bi]
    mid_ref = mid_scr.at[bi]

    @pl.when(bi == 0)
    def _():
        cb = cb_ref[...]
        sb = sb_ref[...]
        for r in range(groups + 1):
            nr = DFT_K0 if r < groups else DFT_TAIL
            ca = ca_ref[pl.ds(i * groups + r, 1), :]
            sa = sa_ref[pl.ds(i * groups + r, 1), :]
            cn_scr[r * DFT_K0:r * DFT_K0 + nr, :] = (ca * cb[:nr] - sa * sb[:nr]).astype(BF16)
            sn_scr[r * DFT_K0:r * DFT_K0 + nr, :] = (sa * cb[:nr] + ca * sb[:nr]).astype(BF16)

    e = _dot(cn_scr[...], pf_ref[:, :D_FOURIER])
    o = _dot(sn_scr[...], pf_ref[:, D_FOURIER:])
    rows = lax.broadcasted_iota(jnp.int32, (tk + DFT_TAIL, 1), 0)
    mid = mid_ref[0:1, :]
    e = e + jnp.where((rows & 1) == 0, mid, -mid)
    top_ref[...] = (e + o)[:tk].astype(BF16)
    bot_ref[...] = _dot(_anti_identity(tk, tk + DFT_TAIL, tk), (e - o).astype(BF16)).astype(BF16)


def _dft_half(tables, p):
    ca, sa, cb, sb = tables
    b, n, _ = p.shape
    nh = n // 2
    tk = TK_DFT
    nt = nh // tk
    whole = lambda a: pl.BlockSpec(a.shape, lambda i, bi: (0, 0))
    return pl.pallas_call(
        _dft_half_kernel,
        grid=(nt, b),
        in_specs=[whole(ca), whole(sa), whole(cb), whole(sb),
                  pl.BlockSpec((None, n, 2 * D_FOURIER), lambda i, bi: (jnp.where(i == 0, bi, b - 1), 0, 0))],
        out_specs=[pl.BlockSpec((None, tk, D_FOURIER), lambda i, bi: (bi, i, 0)),
                   pl.BlockSpec((None, tk, D_FOURIER), lambda i, bi: (bi, nt - 1 - i, 0))],
        out_shape=[jax.ShapeDtypeStruct((b, nh, D_FOURIER), BF16), jax.ShapeDtypeStruct((b, nh, D_FOURIER), BF16)],
        scratch_shapes=[pltpu.VMEM((tk + DFT_TAIL, nh), BF16), pltpu.VMEM((tk + DFT_TAIL, nh), BF16),
                        pltpu.VMEM((b, nh, 2 * D_FOURIER), BF16), pltpu.VMEM((b, SUBLANES, D_FOURIER), F32)],
        compiler_params=_cparams(("arbitrary", "arbitrary")),
        name="dft_half",
    )(ca, sa, cb, sb, p)


def _attn_kernel(sink_ref, qt_ref, *rest, layer, local):
    if local:
        kp_ref, kc_ref, kn_ref, vp_ref, vc_ref, vn_ref, kx_ref, vxt_ref, o_ref = rest
    else:
        kx_ref, vxt_ref, o_ref = rest
    blk = ATT_BLOCK
    group = ATT_HEADS // ATT_KV_HEADS
    gw = group * blk
    cl = kx_ref.shape[0]
    nq = qt_ref.shape[1] // blk
    lane_head = lax.broadcasted_iota(jnp.int32, (1, gw), 1) // blk
    sinks = []
    for kv in range(ATT_KV_HEADS):
        sink = jnp.zeros((1, gw), F32)
        for i in range(group):
            sink = jnp.where(lane_head == i, sink_ref[layer, kv * group + i] * LOG2E, sink)
        sinks.append(sink)
    kx = kx_ref[...]
    vxt = vxt_ref[...]
    if local:
        j = pl.program_id(1)
        nb = pl.num_programs(1) * nq
        ks = lax.broadcasted_iota(jnp.int32, (blk, blk), 0)
        qi = lax.broadcasted_iota(jnp.int32, (blk, blk), 1)
        in_p = jnp.abs(ks - blk - qi) <= WINDOW
        in_n = jnp.abs(ks + blk - qi) <= WINDOW
        kloc = jnp.concatenate([kp_ref[...], kc_ref[...], kn_ref[...]], axis=0)
        vloc = jnp.concatenate([vp_ref[...], vc_ref[...], vn_ref[...]], axis=1)
    zq = jnp.zeros((HEAD_DIM, gw), BF16)

    def scores(i, kv):
        qt_blk = qt_ref[:, i * blk:(i + 1) * blk]
        r0 = kv * group * HEAD_DIM
        qblk = jnp.concatenate([qt_blk[r0 + h * HEAD_DIM:r0 + (h + 1) * HEAD_DIM, :] for h in range(group)], axis=1)
        qt_g = jnp.concatenate([qblk, zq] if kv == 0 else [zq, qblk], axis=0)
        if not local:
            return [_dot(kx, qt_g)]
        jb = j * nq + i
        bias_p = jnp.where(in_p & (jb > 0), 0.0, NEG)
        bias_n = jnp.where(in_n & (jb < nb - 1), 0.0, NEG)
        s = _dot(jnp.concatenate([kx, kloc[i * blk:(i + 3) * blk, :]], axis=0), qt_g)
        add = lambda sg, bias: jnp.concatenate([sg[:, h * blk:(h + 1) * blk] + bias for h in range(group)], axis=1)
        return [s[:cl], add(s[cl:cl + blk], bias_p), s[cl + blk:cl + 2 * blk], add(s[cl + 2 * blk:], bias_n)]

    def colmax(segs, kv):
        m = sinks[kv]
        for sg in segs:
            m = jnp.maximum(m, jnp.max(sg, axis=0, keepdims=True))
        return m

    def softmax(segs, kv, m):
        den = jnp.exp2(sinks[kv] - m)
        ps = []
        for sg in segs:
            p = jnp.exp2(sg - m)
            den = den + jnp.sum(p, axis=0, keepdims=True)
            ps.append(p.astype(BF16))
        return (jnp.concatenate(ps, axis=0) if len(ps) > 1 else ps[0]), den

    def values(i, kv, p_all, den):
        vtall = jnp.concatenate([vxt, vloc[:, i * blk:(i + 3) * blk]], axis=1) if local else vxt
        ot = _dot(vtall[kv * HEAD_DIM:(kv + 1) * HEAD_DIM, :], p_all) * (1.0 / den)
        for cb in range(group // 2):
            pair = jnp.concatenate([ot[:, (2 * cb) * blk:(2 * cb + 1) * blk],
                                    ot[:, (2 * cb + 1) * blk:(2 * cb + 2) * blk]], axis=0)
            c0 = (kv * (group // 2) + cb) * LANES
            o_ref[i * blk:(i + 1) * blk, c0:c0 + LANES] = jnp.transpose(pair).astype(BF16)

    chains = [(i, kv) for i in range(nq) for kv in range(ATT_KV_HEADS)]
    segs_q, max_q, soft_q = {}, {}, {}
    lag_max, lag_soft, lag_val = ATT_SKEW
    for step in range(len(chains) + lag_val):
        if step < len(chains):
            segs_q[step] = scores(*chains[step])
        c = step - lag_max
        if 0 <= c < len(chains):
            max_q[c] = colmax(segs_q[c], chains[c][1])
        c = step - lag_soft
        if 0 <= c < len(chains):
            soft_q[c] = softmax(segs_q.pop(c), chains[c][1], max_q.pop(c))
        c = step - lag_val
        if 0 <= c < len(chains):
            values(*chains[c], *soft_q.pop(c))


def _attention(sink, layer, qt, k, vt, kx, vxt, local):
    b, _, n = qt.shape
    cl = kx.shape[1]
    blk = ATT_BLOCK
    tq = min(TQ_ATT, n)
    nq = tq // blk
    nb = n // blk
    in_specs = [pl.BlockSpec(memory_space=pltpu.SMEM),
                pl.BlockSpec((None, D_ATT, tq), lambda bi, j: (bi, 0, j))]
    args = [sink, qt]
    if local:
        pj = lambda j: jnp.maximum(j * nq - 1, 0)
        nj = lambda j: jnp.minimum((j + 1) * nq, nb - 1)
        in_specs += [pl.BlockSpec((None, blk, D_KV), lambda bi, j: (bi, pj(j), 0)),
                     pl.BlockSpec((None, tq, D_KV), lambda bi, j: (bi, j, 0)),
                     pl.BlockSpec((None, blk, D_KV), lambda bi, j: (bi, nj(j), 0)),
                     pl.BlockSpec((None, D_KV, blk), lambda bi, j: (bi, 0, pj(j))),
                     pl.BlockSpec((None, D_KV, tq), lambda bi, j: (bi, 0, j)),
                     pl.BlockSpec((None, D_KV, blk), lambda bi, j: (bi, 0, nj(j)))]
        args += [k, k, k, vt, vt, vt]
    in_specs += [pl.BlockSpec((None, cl, D_KV), lambda bi, j: (bi, 0, 0)),
                 pl.BlockSpec((None, D_KV, cl), lambda bi, j: (bi, 0, 0))]
    args += [kx, vxt]
    return pl.pallas_call(
        functools.partial(_attn_kernel, layer=layer, local=local),
        grid=(b, n // tq),
        in_specs=in_specs,
        out_specs=pl.BlockSpec((None, tq, D_ATT), lambda bi, j: (bi, j, 0)),
        out_shape=jax.ShapeDtypeStruct((b, n, D_ATT), BF16),
        compiler_params=_cparams(("parallel", "parallel")),
        name="attention",
    )(*args)


def _mlstm_stages(d, k_ref, qt_ref, vt_ref, gr_ref, h_ref, c_scr, n_scr, m_scr, consts):
    L = MLSTM_L
    H = MLSTM_HEADS
    DH = MLSTM_DIM
    tri, valid_t, lane_h, row_h, blockdiag, nmask = consts
    r0 = d * H
    v = {}

    def stage_in():
        k = k_ref[...]
        qt = qt_ref[...]
        gr = gr_ref[...]
        cum = _dot_split3(gr, tri)
        li = gr[r0:r0 + H, :]
        a = cum[2 * H + r0:2 * H + r0 + H, :]
        a_tot = a[:, L - 1:L] if d == 0 else a[:, 0:1]
        m_in = m_scr[r0:r0 + H, 0:1]
        c_old = c_scr[d]
        n_old = n_scr[d]
        kst = jnp.concatenate([jnp.where(lane_h == hd, k, jnp.zeros_like(k)) for hd in range(H)], axis=0)
        v.update(k=k, li=li, a=a, a_tot=a_tot, m_in=m_in, c_old=c_old, n_old=n_old,
                 g_prev=a + m_in,
                 b_t=jnp.transpose(jnp.concatenate([li - a, jnp.zeros((L - H, L), F32)], axis=0)),
                 inter=_dot(jnp.concatenate([c_old, n_old], axis=0).astype(BF16), qt),
                 s_t=_dot(kst, qt))

    def stage_weights():
        p_parts, m_rows, wp_rows, den_rows = [], [], [], []
        for hd in range(H):
            d_t = jnp.where(valid_t, v['b_t'][:, hd:hd + 1] + v['a'][hd:hd + 1, :], NEG)
            g_prev = v['g_prev'][hd:hd + 1, :]
            m_t = jnp.maximum(g_prev, jnp.max(d_t, axis=0, keepdims=True))
            p_t = v['s_t'][hd * L:(hd + 1) * L, :] * jnp.exp(d_t - m_t)
            w_prev = jnp.exp(g_prev - m_t)
            den = jnp.sum(p_t, axis=0, keepdims=True) + w_prev * v['inter'][D_MLSTM + hd:D_MLSTM + hd + 1, :]
            p_parts.append(p_t.astype(BF16))
            m_rows.append(m_t)
            wp_rows.append(w_prev)
            den_rows.append(den)
        v.update(p=jnp.concatenate(p_parts, axis=0), m_rows=m_rows, wp_rows=wp_rows, den_rows=den_rows)

    def stage_out():
        vt = vt_ref[...]
        k = v['k']
        vbd = jnp.concatenate([jnp.where(row_h == hd, vt, jnp.zeros_like(vt)) for hd in range(H)], axis=1)
        num_t = _dot(vbd, v['p'])
        outs = []
        for hd in range(H):
            inv = 1.0 / jnp.maximum(jnp.abs(v['den_rows'][hd]), jnp.exp(-v['m_rows'][hd]))
            sl = slice(hd * DH, (hd + 1) * DH)
            outs.append((num_t[sl, :] + v['wp_rows'][hd] * v['inter'][sl, :]) * inv)
        h_ref[...] = jnp.transpose(jnp.concatenate(outs, axis=0))
        a_tot, m_in = v['a_tot'], v['m_in']
        dl = a_tot - v['a'] + v['li']
        m_new = jnp.maximum(a_tot + m_in, jnp.max(dl, axis=1, keepdims=True))
        w = jnp.exp(dl - m_new)
        decay = jnp.exp(a_tot + m_in - m_new)
        wexp = jnp.concatenate([jnp.broadcast_to(w[hd:hd + 1, :], (DH, L)) for hd in range(H)], axis=0)
        dcol = jnp.concatenate([jnp.broadcast_to(decay[hd:hd + 1, :], (DH, 1)) for hd in range(H)], axis=0)
        upd = _dot((vt.astype(F32) * wexp).astype(BF16), k)
        c_scr[d] = dcol * v['c_old'] + jnp.where(blockdiag, upd, 0.0)
        n_upd = _dot(jnp.concatenate([w, jnp.zeros((H, L), F32)], axis=0).astype(BF16), k)
        decay8 = jnp.concatenate([decay, jnp.zeros((H, 1), F32)], axis=0)
        n_scr[d] = jnp.where(nmask, decay8 * v['n_old'] + n_upd, 0.0)
        m_scr[r0:r0 + H, :] = jnp.broadcast_to(m_new, (H, LANES))

    return stage_in, stage_weights, stage_out


def _mlstm_kernel(kf_ref, qtf_ref, vtf_ref, grf_ref, kb_ref, qtb_ref, vtb_ref, grb_ref,
                  c0_ref, n0_ref, m0_ref, hf_ref, hb_ref, cfin_ref, nfin_ref, mfin_ref, c_scr, n_scr, m_scr):
    c = pl.program_id(0)
    nc = pl.num_programs(0)
    L = MLSTM_L
    H = MLSTM_HEADS
    nb = kf_ref.shape[0]

    @pl.when(c == 0)
    def _():
        c_scr[...] = c0_ref[...]
        n_scr[...] = n0_ref[...]
        m_scr[...] = m0_ref[...]

    ri = lax.broadcasted_iota(jnp.int32, (L, L), 0)
    ci = lax.broadcasted_iota(jnp.int32, (L, L), 1)
    upper = ci >= ri
    lower = ci <= ri
    triu = jnp.where(upper, 1.0, 0.0).astype(BF16)
    tril = jnp.where(lower, 1.0, 0.0).astype(BF16)
    lane_h = lax.broadcasted_iota(jnp.int32, (1, D_MLSTM), 1) // MLSTM_DIM
    row_h = lax.broadcasted_iota(jnp.int32, (D_MLSTM, 1), 0) // MLSTM_DIM
    blockdiag = row_h == lane_h
    nmask = lax.broadcasted_iota(jnp.int32, (2 * H, 1), 0) == lane_h
    cps = kf_ref.shape[1] // L
    chains = []
    for j in range(cps):
        for bi in range(nb):
            st = (c_scr.at[bi], n_scr.at[bi], m_scr.at[bi])
            for d, refs, consts in ((0, (kf_ref, qtf_ref, vtf_ref, grf_ref, hf_ref), (triu, upper)),
                                    (1, (kb_ref, qtb_ref, vtb_ref, grb_ref, hb_ref), (tril, lower))):
                ck = j if d == 0 else cps - 1 - j
                rows = pl.ds(ck * L, L)
                k_ref, qt_ref, vt_ref, gr_ref, h_ref = refs
                chains.append(_mlstm_stages(d, k_ref.at[bi, rows], qt_ref.at[bi, ck], vt_ref.at[bi, ck],
                                            gr_ref.at[bi, :, rows], h_ref.at[bi, rows], *st,
                                            consts + (lane_h, row_h, blockdiag, nmask)))
    stage_order = (0, 1, 2) if cps == 1 else (2, 1, 0)
    for step in range(len(chains) + 2):
        for stage in stage_order:
            ch = step - stage
            if 0 <= ch < len(chains):
                chains[ch][stage]()

    @pl.when(c == nc - 1)
    def _():
        cfin_ref[...] = c_scr[...]
        nfin_ref[...] = n_scr[...]
        mfin_ref[...] = m_scr[...]


def _mlstm(mk, mqt, mvt, grow, state):
    b, n, _ = mk.shape
    cps = max(c for c in range(1, MLSTM_CHUNKS_PER_STEP + 1) if n % (c * MLSTM_L) == 0)
    L = cps * MLSTM_L
    nc = n // L
    nh = 2 * MLSTM_HEADS
    tok = lambda cm: pl.BlockSpec((b, L, D_MLSTM), lambda c: (0, cm(c), 0))
    tr = lambda rows, cm: pl.BlockSpec((b, rows, L), lambda c: (0, 0, cm(c)))
    chk = lambda cm: pl.BlockSpec((b, cps, D_MLSTM, MLSTM_L), lambda c: (0, cm(c), 0, 0))
    fw = lambda c: c
    bw = lambda c: nc - 1 - c
    cspec = pl.BlockSpec((b, 2, D_MLSTM, D_MLSTM), lambda c: (0, 0, 0, 0))
    nspec = pl.BlockSpec((b, 2, nh, D_MLSTM), lambda c: (0, 0, 0, 0))
    mspec = pl.BlockSpec((b, nh, LANES), lambda c: (0, 0, 0))
    return pl.pallas_call(
        _mlstm_kernel,
        grid=(nc,),
        in_specs=[tok(fw), chk(fw), chk(fw), tr(N_GATES, fw),
                  tok(bw), chk(bw), chk(bw), tr(N_GATES, bw), cspec, nspec, mspec],
        out_specs=[tok(fw), tok(bw), cspec, nspec, mspec],
        out_shape=[jax.ShapeDtypeStruct((b, n, D_MLSTM), F32), jax.ShapeDtypeStruct((b, n, D_MLSTM), F32),
                   jax.ShapeDtypeStruct((b, 2, D_MLSTM, D_MLSTM), F32),
                   jax.ShapeDtypeStruct((b, 2, nh, D_MLSTM), F32),
                   jax.ShapeDtypeStruct((b, nh, LANES), F32)],
        scratch_shapes=[pltpu.VMEM((b, 2, D_MLSTM, D_MLSTM), F32), pltpu.VMEM((b, 2, nh, D_MLSTM), F32),
                        pltpu.VMEM((b, nh, LANES), F32)],
        compiler_params=_cparams(("arbitrary",)),
        name="mlstm",
    )(mk, mqt, mvt, grow, mk, mqt, mvt, grow, *state)


def _rope_tables(n):
    rows = n // GRID_W
    row = jnp.broadcast_to(jnp.arange(rows, dtype=F32)[:, None], (rows, GRID_W)).reshape(n)
    colp = jnp.broadcast_to(jnp.arange(GRID_W, dtype=F32)[None, :], (rows, GRID_W)).reshape(n)
    nf = HEAD_DIM // 4
    inv = ROPE_BASE ** (-jnp.arange(nf, dtype=F32) / nf)
    ar = row[:, None] * inv
    ac = colp[:, None] * inv
    ang = jnp.concatenate([ar, ar, ac, ac], axis=-1)
    cos = jnp.cos(ang)
    sin = jnp.sin(ang)
    sign = jnp.where((jnp.arange(HEAD_DIM) & 16) == 0, -1.0, 1.0).astype(F32)
    reps = LANES // HEAD_DIM
    return jnp.tile(cos, (1, reps)), jnp.tile(sin * sign, (1, reps))


def _dft_factor_tables(n):
    nk1 = n // DFT_K0
    j = jnp.arange(n, dtype=jnp.int32)
    k1 = jnp.arange(nk1, dtype=jnp.int32) * DFT_K0
    k0 = jnp.arange(DFT_K0, dtype=jnp.int32)
    w = 2.0 * math.pi / n
    ang_a = ((k1[:, None] * j[None, :]) % n).astype(F32) * w
    ang_b = ((k0[:, None] * j[None, :]) % n).astype(F32) * w
    return jnp.cos(ang_a), jnp.sin(ang_a), jnp.cos(ang_b), jnp.sin(ang_b)


def _dft_half_tables(n):
    nh = n // 2
    nk1 = -(-(nh // DFT_K0 + 1) // SUBLANES) * SUBLANES
    j = jnp.arange(nh, dtype=jnp.int32)
    k1 = jnp.arange(nk1, dtype=jnp.int32) * DFT_K0
    k0 = jnp.arange(DFT_K0, dtype=jnp.int32)
    w = 2.0 * math.pi / n
    ang_a = ((k1[:, None] * j[None, :]) % n).astype(F32) * w
    ang_b = ((k0[:, None] * j[None, :]) % n).astype(F32) * w
    return jnp.cos(ang_a), jnp.sin(ang_a), jnp.cos(ang_b), jnp.sin(ang_b)


def _dft_tables(n):
    ca, sa, cb, sb = _dft_factor_tables(n)
    cn = (ca[:, None, :] * cb[None] - sa[:, None, :] * sb[None]).reshape(n, n)
    sn = (sa[:, None, :] * cb[None] + ca[:, None, :] * sb[None]).reshape(n, n)
    return cn.astype(BF16), sn.astype(BF16)


def _blockdiag(blocks):
    g, c = blocks.shape[-3], blocks.shape[-1]
    eye = jnp.eye(g, dtype=blocks.dtype)
    out = jnp.einsum('...gce,gh->...gche', blocks, eye)
    return out.reshape(blocks.shape[:-3] + (g * c, g * c))


def _channel_dft_blockdiag():
    e = jnp.arange(FOURIER_CH, dtype=jnp.int32)
    ang = ((e[:, None] * e[None, :]) % FOURIER_CH).astype(F32) * (2.0 * math.pi / FOURIER_CH)
    reps = (FOURIER_GROUPS, 1, 1)
    return _blockdiag(jnp.tile(jnp.cos(ang)[None], reps)), _blockdiag(jnp.tile(-jnp.sin(ang)[None], reps))


def kernel(x, c, ctx, c_ctx, w_ada, b_ada, norm_g, w_ffn_in, w_ffn_out, w_in, w_out, w_fourier, attn_sink,
           conv_qk, b_gate_i, b_gate_f, g_final):
    b, n, d = x.shape
    cl = ctx.shape[1]
    depth = w_ada.shape[0]
    nh = 2 * MLSTM_HEADS

    rows = -(-(b + 1) // SUBLANES) * SUBLANES
    c_all = jnp.zeros((rows, d), F32).at[:b].set(c).at[b].set(c_ctx)
    mod = _modulation(c_all, w_ada, b_ada).reshape(depth, rows, N_MOD, d)
    x_row = lambda bi: bi
    c_row = lambda bi: b

    offs = [0]
    for s in (D_FOURIER, D_ATT, D_KV, D_KV, D_MLSTM, D_MLSTM, D_MLSTM, D_MLSTM, nh, nh):
        offs.append(offs[-1] + s)
    assert offs[1] == D_FOURIER and offs[8] - offs[1] == C_END - C_Q
    w_gate = w_in[:, :, offs[8]:offs[10]]
    cc_bd, sc_bd = _channel_dft_blockdiag()
    w_p, w_p_ctx, w_rest = _fourier_fold(w_in, _blockdiag(w_fourier), cc_bd, sc_bd,
                                         1.0 / math.sqrt(n * FOURIER_CH), 1.0 / math.sqrt(cl * FOURIER_CH))
    wg_t = jnp.swapaxes(w_gate, 1, 2).astype(BF16)
    gb_row = jnp.concatenate([b_gate_i.reshape(depth, nh), b_gate_f.reshape(depth, nh)], axis=-1)[:, :, None]
    wf = (w_ffn_in[0, 0].astype(BF16), w_ffn_out[0, 0].astype(BF16))
    wo = w_out.astype(BF16)

    cos128, sin128 = _rope_tables(n)
    half_spectrum = n % (2 * TK_DFT) == 0 and TK_DFT == TM
    dft_x = _dft_half_tables(n) if half_spectrum else _dft_tables(n)
    cnc, snc = _dft_tables(cl)
    zeros_c = jnp.zeros((cl, LANES), F32)
    state0 = (jnp.zeros((b, 2, D_MLSTM, D_MLSTM), F32), jnp.zeros((b, 2, nh, D_MLSTM), F32),
              jnp.zeros((b, nh, LANES), F32))

    xc = ctx
    for l in range(depth):
        last = l == depth - 1
        x, *wf_next = _ffn(x, l, 0, mod, x_row, norm_g, *wf, cast_next=(w_ffn_in, w_ffn_out, l, 1))
        xc = _ffn(xc, l, 0, mod, c_row, norm_g, *wf)
        wf = wf_next

        px, qtx, kx, vtx, mkx, mqtx, mvtx, ogx, grx = _inproj(
            x, l, mod, x_row, norm_g, w_p, w_rest, wg_t, gb_row, conv_qk, cos128, sin128, True)
        pc, qtc, kc, vtc, mkc, mqtc, mvtc, ogc, grc = _inproj(
            xc, l, mod, c_row, norm_g, w_p_ctx, w_rest, wg_t, gb_row, conv_qk, zeros_c, zeros_c, False)

        hcf, hcb, *state_c = _mlstm(mkc, mqtc, mvtc, grc, state0)
        hxf, hxb, *_ = _mlstm(mkx, mqtx, mvtx, grx, state_c)
        att_x = _attention(attn_sink, l, qtx, kx, vtx, kc, vtc, True)
        yf_x = tuple(_dft_half(dft_x, px)) if half_spectrum else _dft(*dft_x, px)
        mix_x = (yf_x, att_x, hxf, hxb, ogx)
        if last:
            x = _ffn(x, l, 2, mod, x_row, norm_g, *wf, mixer=mix_x, w_o=wo, g_final=g_final)
        else:
            x, *wf_next = _ffn(x, l, 2, mod, x_row, norm_g, *wf, mixer=mix_x, w_o=wo,
                               cast_next=(w_ffn_in, w_ffn_out, l + 1, 0))
            att_c = _attention(attn_sink, l, qtc, None, None, kc, vtc, False)
            yf_c = _dft(cnc, snc, pc)
            xc = _ffn(xc, l, 2, mod, c_row, norm_g, *wf, mixer=(yf_c, att_c, hcf, hcb, ogc), w_o=wo)
            wf = wf_next
    return x
```

```python
import functools
import math

import jax
import jax.numpy as jnp
from jax import lax
from jax.experimental import pallas as pl
from jax.experimental.pallas import tpu as pltpu

F32 = jnp.float32
BF16 = jnp.bfloat16
HI = lax.Precision.HIGHEST

GRID_W = 64
EPS = 1e-6
NEG = -1e30
LOG2E = math.log2(math.e)
FOURIER_GROUPS = 4
FOURIER_CH = 64
D_FOURIER = FOURIER_GROUPS * FOURIER_CH
HEAD_DIM = 64
ATT_HEADS = 8
ATT_KV_HEADS = 2
D_ATT = ATT_HEADS * HEAD_DIM
D_KV = ATT_KV_HEADS * HEAD_DIM
ATT_BLOCK = 128
WINDOW = 128
ROPE_BASE = 10000.0
MLSTM_HEADS = 4
MLSTM_DIM = 64
D_MLSTM = MLSTM_HEADS * MLSTM_DIM
N_MOD = 9
N_GATES = 4 * MLSTM_HEADS

LANES = 128
SUBLANES = 8
VMEM_LIMIT = 56 * 1024 * 1024

TM = 512
TM_INPROJ = 1024
TM_FFN = 1024
FFN_PART_ROWS = 256
TK_DFT = 512
DFT_K0 = 64
DFT_TAIL = 16
TQ_ATT = 1024
ATT_SKEW = (1, 2, 3)
MLSTM_L = 128
MLSTM_CHUNKS_PER_STEP = 1

C_P = 0
C_Q = C_P + 2 * D_FOURIER
C_K = C_Q + D_ATT
C_V = C_K + D_KV
C_MQ = C_V + D_KV
C_MK = C_MQ + D_MLSTM
C_MV = C_MK + D_MLSTM
C_MO = C_MV + D_MLSTM
C_END = C_MO + D_MLSTM


def _cparams(sem):
    return pltpu.CompilerParams(dimension_semantics=sem, vmem_limit_bytes=VMEM_LIMIT)


def _dot(a, b):
    return jnp.dot(a, b, preferred_element_type=F32)


def _dot_nt(a, b):
    return lax.dot_general(a, b, (((1,), (1,)), ((), ())), preferred_element_type=F32)


def _dot_hi(a, b):
    return jnp.dot(a, b, preferred_element_type=F32, precision=HI)


def _dot_split3(x, onehot):
    rows = x.shape[0]
    hi = x.astype(BF16)
    r1 = x - hi.astype(F32)
    mid = r1.astype(BF16)
    lo = (r1 - mid.astype(F32)).astype(BF16)
    y = _dot(jnp.concatenate([hi, mid, lo], axis=0), onehot)
    return y[0:rows] + y[rows:2 * rows] + y[2 * rows:3 * rows]


def _rms_mod(xt, g, shift, scale):
    ms = jnp.mean(xt * xt, axis=-1, keepdims=True)
    y = xt * lax.rsqrt(ms + EPS) * g
    return y * (1.0 + scale) + shift


def _silu(t):
    return t * jax.nn.sigmoid(t)


def _log_sigmoid(t):
    return jnp.minimum(t, 0.0) - jnp.log1p(jnp.exp(-jnp.abs(t)))


def _mod_kernel(c_ref, w_ref, b_ref, o_ref):
    sc = _silu(c_ref[...])
    o_ref[...] = _dot(sc.astype(BF16), w_ref[...].astype(BF16)) + b_ref[...]


def _modulation(c_all, w_ada, b_ada):
    depth, d, nd = w_ada.shape
    r = c_all.shape[0]
    tn = 1024
    return pl.pallas_call(
        _mod_kernel,
        grid=(depth, nd // tn),
        in_specs=[pl.BlockSpec((r, d), lambda l, j: (0, 0)),
                  pl.BlockSpec((None, d, tn), lambda l, j: (l, 0, j)),
                  pl.BlockSpec((None, 1, tn), lambda l, j: (l, 0, j))],
        out_specs=pl.BlockSpec((None, r, tn), lambda l, j: (l, 0, j)),
        out_shape=jax.ShapeDtypeStruct((depth, r, nd), F32),
        compiler_params=_cparams(("parallel", "parallel")),
        name="modulation",
    )(c_all, w_ada, b_ada.reshape(depth, 1, nd))


def _fold_kernel(win_ref, wf_ref, cc_ref, sc_ref, o_ref, oc_ref, rest_ref, *, scale, scale_ctx):
    rest_ref[...] = win_ref[:, D_FOURIER:D_FOURIER + C_END - C_Q].astype(BF16)
    wf = wf_ref[...]
    mc = _dot_hi(cc_ref[...], wf)
    ms = _dot_hi(sc_ref[...], wf)
    wa = win_ref[:, :D_FOURIER]
    wc = _dot_hi(wa, mc)
    ws = _dot_hi(wa, ms)
    o_ref[:, :D_FOURIER] = (wc * scale).astype(BF16)
    o_ref[:, D_FOURIER:] = (ws * scale).astype(BF16)
    oc_ref[:, :D_FOURIER] = (wc * scale_ctx).astype(BF16)
    oc_ref[:, D_FOURIER:] = (ws * scale_ctx).astype(BF16)


def _fourier_fold(w_in, wf_bd, cc_bd, sc_bd, scale, scale_ctx):
    depth, d, d_in = w_in.shape
    out = pl.BlockSpec((None, d, 2 * D_FOURIER), lambda l: (l, 0, 0))
    n_rest = C_END - C_Q
    return pl.pallas_call(
        functools.partial(_fold_kernel, scale=scale, scale_ctx=scale_ctx),
        grid=(depth,),
        in_specs=[pl.BlockSpec((None, d, d_in), lambda l: (l, 0, 0)),
                  pl.BlockSpec((None, D_FOURIER, D_FOURIER), lambda l: (l, 0, 0)),
                  pl.BlockSpec((D_FOURIER, D_FOURIER), lambda l: (0, 0)),
                  pl.BlockSpec((D_FOURIER, D_FOURIER), lambda l: (0, 0))],
        out_specs=[out, out, pl.BlockSpec((None, d, n_rest), lambda l: (l, 0, 0))],
        out_shape=[jax.ShapeDtypeStruct((depth, d, 2 * D_FOURIER), BF16)] * 2
        + [jax.ShapeDtypeStruct((depth, d, n_rest), BF16)],
        compiler_params=_cparams(("parallel",)),
        name="fourier_fold",
    )(w_in, wf_bd, cc_bd, sc_bd)


def _ffn_kernel(*refs, sub, d_ff, mix, split_yf, final, cast_next, split):
    x_ref, mod_ref, g_ref, win_ref, wout_ref = refs[:5]
    if cast_next:
        o_ref, nin_o_ref, nout_o_ref = refs[-3:]
        nin_ref, nout_ref = refs[-5:-3]
        nin_o_ref[...] = nin_ref[...].astype(BF16)
        nout_o_ref[...] = nout_ref[...].astype(BF16)
        rest = list(refs[5:-5])
    else:
        o_ref = refs[-1]
        rest = list(refs[5:-1])
    shift = mod_ref[3 * sub:3 * sub + 1, :]
    scale = mod_ref[3 * sub + 1:3 * sub + 2, :]
    gate = mod_ref[3 * sub + 2:3 * sub + 3, :]
    tm = x_ref.shape[0]
    rs = tm // split
    def stages(part):
        rows = slice(part * rs, (part + 1) * rs)
        v = {}

        def s_norm():
            xt = x_ref[rows, :]
            if mix:
                if split_yf:
                    yt_ref, yb_ref, att_ref, hf_ref, hb_ref, og_ref, wo_ref = rest[:7]
                    in_top = pl.program_id(1) < pl.num_programs(1) // 2
                    yf = jnp.where(in_top, yt_ref[rows, :], yb_ref[rows, :])
                else:
                    yf_ref, att_ref, hf_ref, hb_ref, og_ref, wo_ref = rest[:6]
                    yf = yf_ref[rows, :]
                mh = (og_ref[rows, :].astype(F32) * (hf_ref[rows, :] + hb_ref[rows, :])).astype(BF16)
                a0, a1 = D_FOURIER, D_FOURIER + D_ATT
                y = (_dot(yf, wo_ref[:a0, :]) + _dot(att_ref[rows, :], wo_ref[a0:a1, :])
                     + _dot(mh, wo_ref[a1:, :]))
                xt = xt + mod_ref[5:6, :] * y
            v['xt'] = xt
            v['h'] = _rms_mod(xt, g_ref[sub:sub + 1, :], shift, scale).astype(BF16)

        def s_up():
            v['gu'] = _dot(v.pop('h'), win_ref[...])

        def s_act():
            gu = v.pop('gu')
            v['act'] = (_silu(gu[:, :d_ff]) * gu[:, d_ff:]).astype(BF16)

        def s_down():
            y = v.pop('xt') + (0.5 * gate) * _dot(v.pop('act'), wout_ref[...])
            if final:
                gf_ref = rest[-1]
                ms = jnp.mean(y * y, axis=-1, keepdims=True)
                y = y * lax.rsqrt(ms + EPS) * gf_ref[...]
            o_ref[rows, :] = y

        return s_norm, s_up, s_act, s_down

    parts = [stages(part) for part in range(split)]
    for step in range(split + 3):
        for stage in range(4):
            part = step - stage
            if 0 <= part < split:
                parts[part][stage]()


def _slab_count(rows, steps):
    for nblk in range(steps, 0, -1):
        if rows % nblk == 0 and (rows // nblk) % (2 * SUBLANES) == 0:
            return nblk
    return 1


def _ffn(x, layer, sub, mod, mod_row, norm_g, w_in, w_out, mixer=None, w_o=None, g_final=None, cast_next=None):
    b, n, d = x.shape
    d_ff = w_out.shape[0]
    mix = mixer is not None
    tm = min(TM if mix else TM_FFN, n)
    nt = n // tm
    split = max(tm // FFN_PART_ROWS, 1)
    final = g_final is not None
    tok = lambda w: pl.BlockSpec((None, tm, w), lambda bi, t: (bi, t, 0))
    whole = lambda r, c: pl.BlockSpec((r, c), lambda bi, t: (0, 0), pipeline_mode=pl.Buffered(1))
    in_specs = [tok(d),
                pl.BlockSpec((None, None, N_MOD, d), lambda bi, t: (layer, mod_row(bi), 0, 0)),
                pl.BlockSpec((None, 3, d), lambda bi, t: (layer, 0, 0)),
                whole(d, 2 * d_ff), whole(d_ff, d)]
    args = [x, mod, norm_g, w_in, w_out]
    split_yf = mix and isinstance(mixer[0], tuple)
    if mix:
        if split_yf:
            hh = nt // 2
            in_specs += [pl.BlockSpec((None, tm, D_FOURIER), lambda bi, t: (bi, jnp.minimum(t, hh - 1), 0)),
                         pl.BlockSpec((None, tm, D_FOURIER), lambda bi, t: (bi, jnp.maximum(t - hh, 0), 0))]
            args += list(mixer[0])
        else:
            in_specs.append(tok(D_FOURIER))
            args.append(mixer[0])
        in_specs += [tok(D_ATT), tok(D_MLSTM), tok(D_MLSTM), tok(D_MLSTM),
                     pl.BlockSpec((None, d, d), lambda bi, t: (layer, 0, 0), pipeline_mode=pl.Buffered(1))]
        args += list(mixer[1:]) + [w_o]
    if final:
        in_specs.append(pl.BlockSpec((1, d), lambda bi, t: (0, 0)))
        args.append(g_final.reshape(1, d))
    out_specs = [tok(d)]
    out_shape = [jax.ShapeDtypeStruct((b, n, d), F32)]
    if cast_next is not None:
        nin, nout, nl, nh_ = cast_next
        steps = b * nt
        for arr, rows, cols in ((nin, d, 2 * d_ff), (nout, d_ff, d)):
            nblk = _slab_count(rows, steps)
            slab = lambda bi, t, nblk=nblk: jnp.minimum(bi * nt + t, nblk - 1)
            in_specs.append(pl.BlockSpec((None, None, rows // nblk, cols),
                                         lambda bi, t, slab=slab: (nl, nh_, slab(bi, t), 0)))
            out_specs.append(pl.BlockSpec((rows // nblk, cols), lambda bi, t, slab=slab: (slab(bi, t), 0)))
            out_shape.append(jax.ShapeDtypeStruct((rows, cols), BF16))
            args.append(arr)
    res = pl.pallas_call(
        functools.partial(_ffn_kernel, sub=sub, d_ff=d_ff, mix=mix, split_yf=split_yf, final=final, split=split,
                          cast_next=cast_next is not None),
        grid=(b, nt),
        in_specs=in_specs,
        out_specs=out_specs,
        out_shape=out_shape,
        compiler_params=_cparams(("arbitrary", "arbitrary")),
        name="ffn_mix" if mix else "ffn",
    )(*args)
    return res if cast_next is not None else res[0]


def _store_chunks(ref, t):
    for c in range(ref.shape[0]):
        ref[c] = t[:, c * MLSTM_L:(c + 1) * MLSTM_L]


def _rope128(t, cos, sin_signed, low16):
    fwd = pltpu.roll(t, LANES - 16, axis=1)
    bwd = pltpu.roll(t, 16, axis=1)
    return t * cos + jnp.where(low16, fwd, bwd) * sin_signed


def _inproj_kernel(x_ref, xp_ref, xn_ref, mod_ref, g_ref, wp_ref, w_ref, wgt_ref, gbr_ref, cw_ref,
                   cos_ref, sin_ref,
                   p_ref, qt_ref, k_ref, vt_ref, mk_ref, mqt_ref, mvt_ref, og_ref, gr_ref, *, rope):
    t = pl.program_id(1)
    nt = pl.num_programs(1)
    tm = x_ref.shape[0]
    shift = mod_ref[3:4, :]
    scale = mod_ref[4:5, :]
    g = g_ref[1:2, :]
    h = _rms_mod(x_ref[...], g, shift, scale).astype(BF16)
    seg = lambda c0, c1: _dot(h, w_ref[:, c0 - C_Q:c1 - C_Q])
    lane = lax.broadcasted_iota(jnp.int32, (1, LANES), 1)
    low16 = (lane & 16) == 0
    qscale = HEAD_DIM ** -0.5 * LOG2E
    if rope:
        cos = cos_ref[...]
        sin = sin_ref[...]

    up = _dot(h, wp_ref[...])
    uq = seg(C_Q, C_K)
    p_ref[...] = up.astype(BF16)

    ukv = seg(C_K, C_MQ)
    for cb in range(D_ATT // LANES):
        qb = uq[:, cb * LANES:(cb + 1) * LANES]
        if rope:
            qb = _rope128(qb, cos, sin, low16)
        qt_ref[cb * LANES:(cb + 1) * LANES, :] = jnp.transpose(qb * qscale).astype(BF16)

    uqk = seg(C_MQ, C_MV)
    xh = jnp.concatenate([xp_ref[...], xn_ref[...]], axis=0)
    hh = _rms_mod(xh, g, shift, scale).astype(BF16)
    uh = _dot(hh, w_ref[:, C_MQ - C_Q:C_MV - C_Q])
    kb = ukv[:, :D_KV]
    if rope:
        kb = _rope128(kb, cos, sin, low16)
    k_ref[...] = kb.astype(BF16)
    vt_ref[...] = jnp.transpose(ukv[:, D_KV:]).astype(BF16)

    uvo = seg(C_MV, C_END)
    prev_row = jnp.where(t > 0, uh[SUBLANES - 1:SUBLANES, :], 0.0)
    next_row = jnp.where(t < nt - 1, uh[SUBLANES:SUBLANES + 1, :], 0.0)
    row = lax.broadcasted_iota(jnp.int32, (tm, 1), 0)
    u_prev = jnp.where(row == 0, prev_row, pltpu.roll(uqk, 1, axis=0))
    u_next = jnp.where(row == tm - 1, next_row, pltpu.roll(uqk, tm - 1, axis=0))
    cv = u_prev * cw_ref[0:1, :] + uqk * cw_ref[1:2, :] + u_next * cw_ref[2:3, :]
    qk = _silu(cv)
    _store_chunks(mqt_ref, jnp.transpose(qk[:, :D_MLSTM]).astype(BF16))
    mk_ref[...] = (qk[:, D_MLSTM:] * (MLSTM_DIM ** -0.5)).astype(BF16)

    half = N_GATES // 2
    zr = _dot_nt(wgt_ref[...], h) + gbr_ref[...]
    _store_chunks(mvt_ref, jnp.transpose(uvo[:, :D_MLSTM]).astype(BF16))
    og_ref[...] = jax.nn.sigmoid(uvo[:, D_MLSTM:]).astype(BF16)
    rowg = lax.broadcasted_iota(jnp.int32, (N_GATES, 1), 0)
    gr_ref[...] = jnp.where(rowg >= half, _log_sigmoid(zr), zr)


def _inproj(x, layer, mod, mod_row, norm_g, w_p, w_rest, wg_t, gb_row, conv_w, cos128, sin128, rope):
    b, n, d = x.shape
    tm = min(TM_INPROJ, n)
    nt = n // tm
    r8 = tm // SUBLANES
    n8 = n // SUBLANES
    lay = lambda *blk: pl.BlockSpec((None,) + blk, lambda bi, t: (layer,) + (0,) * len(blk))
    tok = lambda w: pl.BlockSpec((None, tm, w), lambda bi, t: (bi, t, 0))
    trn = lambda rows: pl.BlockSpec((None, rows, tm), lambda bi, t: (bi, 0, t))
    in_specs = [tok(d),
                pl.BlockSpec((None, SUBLANES, d), lambda bi, t: (bi, jnp.maximum(t * r8 - 1, 0), 0)),
                pl.BlockSpec((None, SUBLANES, d), lambda bi, t: (bi, jnp.minimum((t + 1) * r8, n8 - 1), 0)),
                pl.BlockSpec((None, None, N_MOD, d), lambda bi, t: (layer, mod_row(bi), 0, 0)),
                lay(3, d), lay(d, C_Q), lay(d, C_END - C_Q), lay(N_GATES, d), lay(N_GATES, 1), lay(3, 2 * D_MLSTM),
                pl.BlockSpec((tm, LANES), lambda bi, t: (t, 0)),
                pl.BlockSpec((tm, LANES), lambda bi, t: (t, 0))]
    chk = pl.BlockSpec((None, tm // MLSTM_L, D_MLSTM, MLSTM_L), lambda bi, t: (bi, t, 0, 0))
    chk_shape = jax.ShapeDtypeStruct((b, n // MLSTM_L, D_MLSTM, MLSTM_L), BF16)
    out_specs = [tok(2 * D_FOURIER), trn(D_ATT), tok(D_KV), trn(D_KV),
                 tok(D_MLSTM), chk, chk, tok(D_MLSTM), trn(N_GATES)]
    shp = lambda w, dt: jax.ShapeDtypeStruct((b, n, w), dt)
    shpt = lambda rows, dt: jax.ShapeDtypeStruct((b, rows, n), dt)
    out_shape = [shp(2 * D_FOURIER, BF16), shpt(D_ATT, BF16), shp(D_KV, BF16), shpt(D_KV, BF16),
                 shp(D_MLSTM, BF16), chk_shape, chk_shape, shp(D_MLSTM, BF16),
                 shpt(N_GATES, F32)]
    return pl.pallas_call(
        functools.partial(_inproj_kernel, rope=rope),
        grid=(b, nt),
        in_specs=in_specs,
        out_specs=out_specs,
        out_shape=out_shape,
        compiler_params=_cparams(("parallel", "parallel")),
        name="inproj",
    )(x, x, x, mod, norm_g, w_p, w_rest, wg_t, gb_row, conv_w, cos128, sin128)


def _dft_kernel(cn_ref, sn_ref, p_ref, y_ref):
    y = _dot(cn_ref[...], p_ref[:, :D_FOURIER]) + _dot(sn_ref[...], p_ref[:, D_FOURIER:])
    y_ref[...] = y.astype(BF16)


def _dft(cn, sn, p):
    b, n, _ = p.shape
    tk = min(TK_DFT, n)
    return pl.pallas_call(
        _dft_kernel,
        grid=(n // tk, b),
        in_specs=[pl.BlockSpec((tk, n), lambda i, bi: (i, 0)),
                  pl.BlockSpec((tk, n), lambda i, bi: (i, 0)),
                  pl.BlockSpec((None, n, 2 * D_FOURIER), lambda i, bi: (bi, 0, 0))],
        out_specs=pl.BlockSpec((None, tk, D_FOURIER), lambda i, bi: (bi, i, 0)),
        out_shape=jax.ShapeDtypeStruct((b, n, D_FOURIER), BF16),
        compiler_params=_cparams(("parallel", "parallel")),
        name="dft",
    )(cn, sn, p)


def _anti_identity(rows, cols, offset):
    r = lax.broadcasted_iota(jnp.int32, (rows, cols), 0)
    c = lax.broadcasted_iota(jnp.int32, (rows, cols), 1)
    return jnp.where(r + c == offset, 1.0, 0.0).astype(BF16)


def _dft_fold(p_ref, pf_ref, mid_ref):
    n = p_ref.shape[0]
    nh = n // 2
    tk = TK_DFT
    flip = _anti_identity(tk, tk, tk - 1)
    nblk = nh // tk
    rev = jnp.concatenate([_dot(flip, p_ref[n - (i + 1) * tk:n - i * tk, :]) for i in range(nblk)], axis=0)
    row = lax.broadcasted_iota(jnp.int32, (nh, 1), 0)
    mirror = jnp.where(row == 0, 0.0, pltpu.roll(rev, 1, axis=0))
    top = p_ref[0:nh, :].astype(F32)
    pf_ref[:, :D_FOURIER] = (top[:, :D_FOURIER] + mirror[:, :D_FOURIER]).astype(BF16)
    pf_ref[:, D_FOURIER:] = (top[:, D_FOURIER:] - mirror[:, D_FOURIER:]).astype(BF16)
    mid_ref[...] = jnp.broadcast_to(p_ref[nh:nh + 1, :D_FOURIER].astype(F32), mid_ref.shape)


def _dft_half_kernel(ca_ref, sa_ref, cb_ref, sb_ref, p_ref, top_ref, bot_ref, cn_scr, sn_scr, pf_scr, mid_scr):
    i = pl.program_id(0)
    bi = pl.program_id(1)
    tk = top_ref.shape[0]
    groups = tk // DFT_K0

    @pl.when(i == 0)
    def _():
        _dft_fold(p_ref, pf_scr.at[bi], mid_scr.at[bi])

    pf_ref = pf_scr.at[bi]
    mid_ref = mid_scr.at[bi]

    @pl.when(bi == 0)
    def _():
        cb = cb_ref[...]
        sb = sb_ref[...]
        for r in range(groups + 1):
            nr = DFT_K0 if r < groups else DFT_TAIL
            ca = ca_ref[pl.ds(i * groups + r, 1), :]
            sa = sa_ref[pl.ds(i * groups + r, 1), :]
            cn_scr[r * DFT_K0:r * DFT_K0 + nr, :] = (ca * cb[:nr] - sa * sb[:nr]).astype(BF16)
            sn_scr[r * DFT_K0:r * DFT_K0 + nr, :] = (sa * cb[:nr] + ca * sb[:nr]).astype(BF16)

    e = _dot(cn_scr[...], pf_ref[:, :D_FOURIER])
    o = _dot(sn_scr[...], pf_ref[:, D_FOURIER:])
    rows = lax.broadcasted_iota(jnp.int32, (tk + DFT_TAIL, 1), 0)
    mid = mid_ref[0:1, :]
    e = e + jnp.where((rows & 1) == 0, mid, -mid)
    top_ref[...] = (e + o)[:tk].astype(BF16)
    bot_ref[...] = _dot(_anti_identity(tk, tk + DFT_TAIL, tk), (e - o).astype(BF16)).astype(BF16)


def _dft_half(tables, p):
    ca, sa, cb, sb = tables
    b, n, _ = p.shape
    nh = n // 2
    tk = TK_DFT
    nt = nh // tk
    whole = lambda a: pl.BlockSpec(a.shape, lambda i, bi: (0, 0))
    return pl.pallas_call(
        _dft_half_kernel,
        grid=(nt, b),
        in_specs=[whole(ca), whole(sa), whole(cb), whole(sb),
                  pl.BlockSpec((None, n, 2 * D_FOURIER), lambda i, bi: (jnp.where(i == 0, bi, b - 1), 0, 0))],
        out_specs=[pl.BlockSpec((None, tk, D_FOURIER), lambda i, bi: (bi, i, 0)),
                   pl.BlockSpec((None, tk, D_FOURIER), lambda i, bi: (bi, nt - 1 - i, 0))],
        out_shape=[jax.ShapeDtypeStruct((b, nh, D_FOURIER), BF16), jax.ShapeDtypeStruct((b, nh, D_FOURIER), BF16)],
        scratch_shapes=[pltpu.VMEM((tk + DFT_TAIL, nh), BF16), pltpu.VMEM((tk + DFT_TAIL, nh), BF16),
                        pltpu.VMEM((b, nh, 2 * D_FOURIER), BF16), pltpu.VMEM((b, SUBLANES, D_FOURIER), F32)],
        compiler_params=_cparams(("arbitrary", "arbitrary")),
        name="dft_half",
    )(ca, sa, cb, sb, p)


def _attn_kernel(sink_ref, qt_ref, *rest, layer, local):
    if local:
        kp_ref, kc_ref, kn_ref, vp_ref, vc_ref, vn_ref, kx_ref, vxt_ref, o_ref = rest
    else:
        kx_ref, vxt_ref, o_ref = rest
    blk = ATT_BLOCK
    group = ATT_HEADS // ATT_KV_HEADS
    gw = group * blk
    cl = kx_ref.shape[0]
    nq = qt_ref.shape[1] // blk
    lane_head = lax.broadcasted_iota(jnp.int32, (1, gw), 1) // blk
    sinks = []
    for kv in range(ATT_KV_HEADS):
        sink = jnp.zeros((1, gw), F32)
        for i in range(group):
            sink = jnp.where(lane_head == i, sink_ref[layer, kv * group + i] * LOG2E, sink)
        sinks.append(sink)
    kx = kx_ref[...]
    vxt = vxt_ref[...]
    if local:
        j = pl.program_id(1)
        nb = pl.num_programs(1) * nq
        ks = lax.broadcasted_iota(jnp.int32, (blk, blk), 0)
        qi = lax.broadcasted_iota(jnp.int32, (blk, blk), 1)
        in_p = jnp.abs(ks - blk - qi) <= WINDOW
        in_n = jnp.abs(ks + blk - qi) <= WINDOW
        kloc = jnp.concatenate([kp_ref[...], kc_ref[...], kn_ref[...]], axis=0)
        vloc = jnp.concatenate([vp_ref[...], vc_ref[...], vn_ref[...]], axis=1)
    zq = jnp.zeros((HEAD_DIM, gw), BF16)

    def scores(i, kv):
        qt_blk = qt_ref[:, i * blk:(i + 1) * blk]
        r0 = kv * group * HEAD_DIM
        qblk = jnp.concatenate([qt_blk[r0 + h * HEAD_DIM:r0 + (h + 1) * HEAD_DIM, :] for h in range(group)], axis=1)
        qt_g = jnp.concatenate([qblk, zq] if kv == 0 else [zq, qblk], axis=0)
        if not local:
            return [_dot(kx, qt_g)]
        jb = j * nq + i
        bias_p = jnp.where(in_p & (jb > 0), 0.0, NEG)
        bias_n = jnp.where(in_n & (jb < nb - 1), 0.0, NEG)
        s = _dot(jnp.concatenate([kx, kloc[i * blk:(i + 3) * blk, :]], axis=0), qt_g)
        add = lambda sg, bias: jnp.concatenate([sg[:, h * blk:(h + 1) * blk] + bias for h in range(group)], axis=1)
        return [s[:cl], add(s[cl:cl + blk], bias_p), s[cl + blk:cl + 2 * blk], add(s[cl + 2 * blk:], bias_n)]

    def colmax(segs, kv):
        m = sinks[kv]
        for sg in segs:
            m = jnp.maximum(m, jnp.max(sg, axis=0, keepdims=True))
        return m

    def softmax(segs, kv, m):
        den = jnp.exp2(sinks[kv] - m)
        ps = []
        for sg in segs:
            p = jnp.exp2(sg - m)
            den = den + jnp.sum(p, axis=0, keepdims=True)
            ps.append(p.astype(BF16))
        return (jnp.concatenate(ps, axis=0) if len(ps) > 1 else ps[0]), den

    def values(i, kv, p_all, den):
        vtall = jnp.concatenate([vxt, vloc[:, i * blk:(i + 3) * blk]], axis=1) if local else vxt
        ot = _dot(vtall[kv * HEAD_DIM:(kv + 1) * HEAD_DIM, :], p_all) * (1.0 / den)
        for cb in range(group // 2):
            pair = jnp.concatenate([ot[:, (2 * cb) * blk:(2 * cb + 1) * blk],
                                    ot[:, (2 * cb + 1) * blk:(2 * cb + 2) * blk]], axis=0)
            c0 = (kv * (group // 2) + cb) * LANES
            o_ref[i * blk:(i + 1) * blk, c0:c0 + LANES] = jnp.transpose(pair).astype(BF16)

    chains = [(i, kv) for i in range(nq) for kv in range(ATT_KV_HEADS)]
    segs_q, max_q, soft_q = {}, {}, {}
    lag_max, lag_soft, lag_val = ATT_SKEW
    for step in range(len(chains) + lag_val):
        if step < len(chains):
            segs_q[step] = scores(*chains[step])
        c = step - lag_max
        if 0 <= c < len(chains):
            max_q[c] = colmax(segs_q[c], chains[c][1])
        c = step - lag_soft
        if 0 <= c < len(chains):
            soft_q[c] = softmax(segs_q.pop(c), chains[c][1], max_q.pop(c))
        c = step - lag_val
        if 0 <= c < len(chains):
            values(*chains[c], *soft_q.pop(c))


def _attention(sink, layer, qt, k, vt, kx, vxt, local):
    b, _, n = qt.shape
    cl = kx.shape[1]
    blk = ATT_BLOCK
    tq = min(TQ_ATT, n)
    nq = tq // blk
    nb = n // blk
    in_specs = [pl.BlockSpec(memory_space=pltpu.SMEM),
                pl.BlockSpec((None, D_ATT, tq), lambda bi, j: (bi, 0, j))]
    args = [sink, qt]
    if local:
        pj = lambda j: jnp.maximum(j * nq - 1, 0)
        nj = lambda j: jnp.minimum((j + 1) * nq, nb - 1)
        in_specs += [pl.BlockSpec((None, blk, D_KV), lambda bi, j: (bi, pj(j), 0)),
                     pl.BlockSpec((None, tq, D_KV), lambda bi, j: (bi, j, 0)),
                     pl.BlockSpec((None, blk, D_KV), lambda bi, j: (bi, nj(j), 0)),
                     pl.BlockSpec((None, D_KV, blk), lambda bi, j: (bi, 0, pj(j))),
                     pl.BlockSpec((None, D_KV, tq), lambda bi, j: (bi, 0, j)),
                     pl.BlockSpec((None, D_KV, blk), lambda bi, j: (bi, 0, nj(j)))]
        args += [k, k, k, vt, vt, vt]
    in_specs += [pl.BlockSpec((None, cl, D_KV), lambda bi, j: (bi, 0, 0)),
                 pl.BlockSpec((None, D_KV, cl), lambda bi, j: (bi, 0, 0))]
    args += [kx, vxt]
    return pl.pallas_call(
        functools.partial(_attn_kernel, layer=layer, local=local),
        grid=(b, n // tq),
        in_specs=in_specs,
        out_specs=pl.BlockSpec((None, tq, D_ATT), lambda bi, j: (bi, j, 0)),
        out_shape=jax.ShapeDtypeStruct((b, n, D_ATT), BF16),
        compiler_params=_cparams(("parallel", "parallel")),
        name="attention",
    )(*args)


def _mlstm_stages(d, k_ref, qt_ref, vt_ref, gr_ref, h_ref, c_scr, n_scr, m_scr, consts):
    L = MLSTM_L
    H = MLSTM_HEADS
    DH = MLSTM_DIM
    tri, valid_t, lane_h, row_h, blockdiag, nmask = consts
    r0 = d * H
    v = {}

    def stage_in():
        k = k_ref[...]
        qt = qt_ref[...]
        gr = gr_ref[...]
        cum = _dot_split3(gr, tri)
        li = gr[r0:r0 + H, :]
        a = cum[2 * H + r0:2 * H + r0 + H, :]
        a_tot = a[:, L - 1:L] if d == 0 else a[:, 0:1]
        m_in = m_scr[r0:r0 + H, 0:1]
        c_old = c_scr[d]
        n_old = n_scr[d]
        kst = jnp.concatenate([jnp.where(lane_h == hd, k, jnp.zeros_like(k)) for hd in range(H)], axis=0)
        v.update(k=k, li=li, a=a, a_tot=a_tot, m_in=m_in, c_old=c_old, n_old=n_old,
                 g_prev=a + m_in,
                 b_t=jnp.transpose(jnp.concatenate([li - a, jnp.zeros((L - H, L), F32)], axis=0)),
                 inter=_dot(jnp.concatenate([c_old, n_old], axis=0).astype(BF16), qt),
                 s_t=_dot(kst, qt))

    def stage_weights():
        p_parts, m_rows, wp_rows, den_rows = [], [], [], []
        for hd in range(H):
            d_t = jnp.where(valid_t, v['b_t'][:, hd:hd + 1] + v['a'][hd:hd + 1, :], NEG)
            g_prev = v['g_prev'][hd:hd + 1, :]
            m_t = jnp.maximum(g_prev, jnp.max(d_t, axis=0, keepdims=True))
            p_t = v['s_t'][hd * L:(hd + 1) * L, :] * jnp.exp(d_t - m_t)
            w_prev = jnp.exp(g_prev - m_t)
            den = jnp.sum(p_t, axis=0, keepdims=True) + w_prev * v['inter'][D_MLSTM + hd:D_MLSTM + hd + 1, :]
            p_parts.append(p_t.astype(BF16))
            m_rows.append(m_t)
            wp_rows.append(w_prev)
            den_rows.append(den)
        v.update(p=jnp.concatenate(p_parts, axis=0), m_rows=m_rows, wp_rows=wp_rows, den_rows=den_rows)

    def stage_out():
        vt = vt_ref[...]
        k = v['k']
        vbd = jnp.concatenate([jnp.where(row_h == hd, vt, jnp.zeros_like(vt)) for hd in range(H)], axis=1)
        num_t = _dot(vbd, v['p'])
        outs = []
        for hd in range(H):
            inv = 1.0 / jnp.maximum(jnp.abs(v['den_rows'][hd]), jnp.exp(-v['m_rows'][hd]))
            sl = slice(hd * DH, (hd + 1) * DH)
            outs.append((num_t[sl, :] + v['wp_rows'][hd] * v['inter'][sl, :]) * inv)
        h_ref[...] = jnp.transpose(jnp.concatenate(outs, axis=0))

    def stage_state():
        vt = vt_ref[...]
        k = v['k']
        a_tot, m_in = v['a_tot'], v['m_in']
        dl = a_tot - v['a'] + v['li']
        m_new = jnp.maximum(a_tot + m_in, jnp.max(dl, axis=1, keepdims=True))
        w = jnp.exp(dl - m_new)
        decay = jnp.exp(a_tot + m_in - m_new)
        wexp = jnp.concatenate([jnp.broadcast_to(w[hd:hd + 1, :], (DH, L)) for hd in range(H)], axis=0)
        dcol = jnp.concatenate([jnp.broadcast_to(decay[hd:hd + 1, :], (DH, 1)) for hd in range(H)], axis=0)
        upd = _dot((vt.astype(F32) * wexp).astype(BF16), k)
        c_scr[d] = dcol * v['c_old'] + jnp.where(blockdiag, upd, 0.0)
        n_upd = _dot(jnp.concatenate([w, jnp.zeros((H, L), F32)], axis=0).astype(BF16), k)
        decay8 = jnp.concatenate([decay, jnp.zeros((H, 1), F32)], axis=0)
        n_scr[d] = jnp.where(nmask, decay8 * v['n_old'] + n_upd, 0.0)
        m_scr[r0:r0 + H, :] = jnp.broadcast_to(m_new, (H, LANES))

    return stage_in, stage_state, stage_weights, stage_out


def _mlstm_kernel(kf_ref, qtf_ref, vtf_ref, grf_ref, kb_ref, qtb_ref, vtb_ref, grb_ref,
                  c0_ref, n0_ref, m0_ref, hf_ref, hb_ref, cfin_ref, nfin_ref, mfin_ref, c_scr, n_scr, m_scr):
    c = pl.program_id(0)
    nc = pl.num_programs(0)
    L = MLSTM_L
    H = MLSTM_HEADS
    nb = kf_ref.shape[0]

    @pl.when(c == 0)
    def _():
        c_scr[...] = c0_ref[...]
        n_scr[...] = n0_ref[...]
        m_scr[...] = m0_ref[...]

    ri = lax.broadcasted_iota(jnp.int32, (L, L), 0)
    ci = lax.broadcasted_iota(jnp.int32, (L, L), 1)
    upper = ci >= ri
    lower = ci <= ri
    triu = jnp.where(upper, 1.0, 0.0).astype(BF16)
    tril = jnp.where(lower, 1.0, 0.0).astype(BF16)
    lane_h = lax.broadcasted_iota(jnp.int32, (1, D_MLSTM), 1) // MLSTM_DIM
    row_h = lax.broadcasted_iota(jnp.int32, (D_MLSTM, 1), 0) // MLSTM_DIM
    blockdiag = row_h == lane_h
    nmask = lax.broadcasted_iota(jnp.int32, (2 * H, 1), 0) == lane_h
    cps = kf_ref.shape[1] // L
    chains = []
    for j in range(cps):
        for bi in range(nb):
            st = (c_scr.at[bi], n_scr.at[bi], m_scr.at[bi])
            for d, refs, consts in ((0, (kf_ref, qtf_ref, vtf_ref, grf_ref, hf_ref), (triu, upper)),
                                    (1, (kb_ref, qtb_ref, vtb_ref, grb_ref, hb_ref), (tril, lower))):
                ck = j if d == 0 else cps - 1 - j
                rows = pl.ds(ck * L, L)
                k_ref, qt_ref, vt_ref, gr_ref, h_ref = refs
                chains.append(_mlstm_stages(d, k_ref.at[bi, rows], qt_ref.at[bi, ck], vt_ref.at[bi, ck],
                                            gr_ref.at[bi, :, rows], h_ref.at[bi, rows], *st,
                                            consts + (lane_h, row_h, blockdiag, nmask)))
    n_stage = len(chains[0])
    stage_order = range(n_stage) if cps == 1 else range(n_stage - 1, -1, -1)
    for step in range(len(chains) + n_stage - 1):
        for stage in stage_order:
            ch = step - stage
            if 0 <= ch < len(chains):
                chains[ch][stage]()

    @pl.when(c == nc - 1)
    def _():
        cfin_ref[...] = c_scr[...]
        nfin_ref[...] = n_scr[...]
        mfin_ref[...] = m_scr[...]


def _mlstm(mk, mqt, mvt, grow, state):
    b, n, _ = mk.shape
    cps = max(c for c in range(1, MLSTM_CHUNKS_PER_STEP + 1) if n % (c * MLSTM_L) == 0)
    L = cps * MLSTM_L
    nc = n // L
    nh = 2 * MLSTM_HEADS
    tok = lambda cm: pl.BlockSpec((b, L, D_MLSTM), lambda c: (0, cm(c), 0))
    tr = lambda rows, cm: pl.BlockSpec((b, rows, L), lambda c: (0, 0, cm(c)))
    chk = lambda cm: pl.BlockSpec((b, cps, D_MLSTM, MLSTM_L), lambda c: (0, cm(c), 0, 0))
    fw = lambda c: c
    bw = lambda c: nc - 1 - c
    cspec = pl.BlockSpec((b, 2, D_MLSTM, D_MLSTM), lambda c: (0, 0, 0, 0))
    nspec = pl.BlockSpec((b, 2, nh, D_MLSTM), lambda c: (0, 0, 0, 0))
    mspec = pl.BlockSpec((b, nh, LANES), lambda c: (0, 0, 0))
    return pl.pallas_call(
        _mlstm_kernel,
        grid=(nc,),
        in_specs=[tok(fw), chk(fw), chk(fw), tr(N_GATES, fw),
                  tok(bw), chk(bw), chk(bw), tr(N_GATES, bw), cspec, nspec, mspec],
        out_specs=[tok(fw), tok(bw), cspec, nspec, mspec],
        out_shape=[jax.ShapeDtypeStruct((b, n, D_MLSTM), F32), jax.ShapeDtypeStruct((b, n, D_MLSTM), F32),
                   jax.ShapeDtypeStruct((b, 2, D_MLSTM, D_MLSTM), F32),
                   jax.ShapeDtypeStruct((b, 2, nh, D_MLSTM), F32),
                   jax.ShapeDtypeStruct((b, nh, LANES), F32)],
        scratch_shapes=[pltpu.VMEM((b, 2, D_MLSTM, D_MLSTM), F32), pltpu.VMEM((b, 2, nh, D_MLSTM), F32),
                        pltpu.VMEM((b, nh, LANES), F32)],
        compiler_params=_cparams(("arbitrary",)),
        name="mlstm",
    )(mk, mqt, mvt, grow, mk, mqt, mvt, grow, *state)


def _rope_tables(n):
    rows = n // GRID_W
    row = jnp.broadcast_to(jnp.arange(rows, dtype=F32)[:, None], (rows, GRID_W)).reshape(n)
    colp = jnp.broadcast_to(jnp.arange(GRID_W, dtype=F32)[None, :], (rows, GRID_W)).reshape(n)
    nf = HEAD_DIM // 4
    inv = ROPE_BASE ** (-jnp.arange(nf, dtype=F32) / nf)
    ar = row[:, None] * inv
    ac = colp[:, None] * inv
    ang = jnp.concatenate([ar, ar, ac, ac], axis=-1)
    cos = jnp.cos(ang)
    sin = jnp.sin(ang)
    sign = jnp.where((jnp.arange(HEAD_DIM) & 16) == 0, -1.0, 1.0).astype(F32)
    reps = LANES // HEAD_DIM
    return jnp.tile(cos, (1, reps)), jnp.tile(sin * sign, (1, reps))


def _dft_factor_tables(n):
    nk1 = n // DFT_K0
    j = jnp.arange(n, dtype=jnp.int32)
    k1 = jnp.arange(nk1, dtype=jnp.int32) * DFT_K0
    k0 = jnp.arange(DFT_K0, dtype=jnp.int32)
    w = 2.0 * math.pi / n
    ang_a = ((k1[:, None] * j[None, :]) % n).astype(F32) * w
    ang_b = ((k0[:, None] * j[None, :]) % n).astype(F32) * w
    return jnp.cos(ang_a), jnp.sin(ang_a), jnp.cos(ang_b), jnp.sin(ang_b)


def _dft_half_tables(n):
    nh = n // 2
    nk1 = -(-(nh // DFT_K0 + 1) // SUBLANES) * SUBLANES
    j = jnp.arange(nh, dtype=jnp.int32)
    k1 = jnp.arange(nk1, dtype=jnp.int32) * DFT_K0
    k0 = jnp.arange(DFT_K0, dtype=jnp.int32)
    w = 2.0 * math.pi / n
    ang_a = ((k1[:, None] * j[None, :]) % n).astype(F32) * w
    ang_b = ((k0[:, None] * j[None, :]) % n).astype(F32) * w
    return jnp.cos(ang_a), jnp.sin(ang_a), jnp.cos(ang_b), jnp.sin(ang_b)


def _dft_tables(n):
    ca, sa, cb, sb = _dft_factor_tables(n)
    cn = (ca[:, None, :] * cb[None] - sa[:, None, :] * sb[None]).reshape(n, n)
    sn = (sa[:, None, :] * cb[None] + ca[:, None, :] * sb[None]).reshape(n, n)
    return cn.astype(BF16), sn.astype(BF16)


def _blockdiag(blocks):
    g, c = blocks.shape[-3], blocks.shape[-1]
    eye = jnp.eye(g, dtype=blocks.dtype)
    out = jnp.einsum('...gce,gh->...gche', blocks, eye)
    return out.reshape(blocks.shape[:-3] + (g * c, g * c))


def _channel_dft_blockdiag():
    e = jnp.arange(FOURIER_CH, dtype=jnp.int32)
    ang = ((e[:, None] * e[None, :]) % FOURIER_CH).astype(F32) * (2.0 * math.pi / FOURIER_CH)
    reps = (FOURIER_GROUPS, 1, 1)
    return _blockdiag(jnp.tile(jnp.cos(ang)[None], reps)), _blockdiag(jnp.tile(-jnp.sin(ang)[None], reps))


def kernel(x, c, ctx, c_ctx, w_ada, b_ada, norm_g, w_ffn_in, w_ffn_out, w_in, w_out, w_fourier, attn_sink,
           conv_qk, b_gate_i, b_gate_f, g_final):
    b, n, d = x.shape
    cl = ctx.shape[1]
    depth = w_ada.shape[0]
    nh = 2 * MLSTM_HEADS

    rows = -(-(b + 1) // SUBLANES) * SUBLANES
    c_all = jnp.zeros((rows, d), F32).at[:b].set(c).at[b].set(c_ctx)
    mod = _modulation(c_all, w_ada, b_ada).reshape(depth, rows, N_MOD, d)
    x_row = lambda bi: bi
    c_row = lambda bi: b

    offs = [0]
    for s in (D_FOURIER, D_ATT, D_KV, D_KV, D_MLSTM, D_MLSTM, D_MLSTM, D_MLSTM, nh, nh):
        offs.append(offs[-1] + s)
    assert offs[1] == D_FOURIER and offs[8] - offs[1] == C_END - C_Q
    w_gate = w_in[:, :, offs[8]:offs[10]]
    cc_bd, sc_bd = _channel_dft_blockdiag()
    w_p, w_p_ctx, w_rest = _fourier_fold(w_in, _blockdiag(w_fourier), cc_bd, sc_bd,
                                         1.0 / math.sqrt(n * FOURIER_CH), 1.0 / math.sqrt(cl * FOURIER_CH))
    wg_t = jnp.swapaxes(w_gate, 1, 2).astype(BF16)
    gb_row = jnp.concatenate([b_gate_i.reshape(depth, nh), b_gate_f.reshape(depth, nh)], axis=-1)[:, :, None]
    wf = (w_ffn_in[0, 0].astype(BF16), w_ffn_out[0, 0].astype(BF16))
    wo = w_out.astype(BF16)

    cos128, sin128 = _rope_tables(n)
    half_spectrum = n % (2 * TK_DFT) == 0 and TK_DFT == TM
    dft_x = _dft_half_tables(n) if half_spectrum else _dft_tables(n)
    cnc, snc = _dft_tables(cl)
    zeros_c = jnp.zeros((cl, LANES), F32)
    state0 = (jnp.zeros((b, 2, D_MLSTM, D_MLSTM), F32), jnp.zeros((b, 2, nh, D_MLSTM), F32),
              jnp.zeros((b, nh, LANES), F32))

    xc = ctx
    for l in range(depth):
        last = l == depth - 1
        x, *wf_next = _ffn(x, l, 0, mod, x_row, norm_g, *wf, cast_next=(w_ffn_in, w_ffn_out, l, 1))
        xc = _ffn(xc, l, 0, mod, c_row, norm_g, *wf)
        wf = wf_next

        px, qtx, kx, vtx, mkx, mqtx, mvtx, ogx, grx = _inproj(
            x, l, mod, x_row, norm_g, w_p, w_rest, wg_t, gb_row, conv_qk, cos128, sin128, True)
        pc, qtc, kc, vtc, mkc, mqtc, mvtc, ogc, grc = _inproj(
            xc, l, mod, c_row, norm_g, w_p_ctx, w_rest, wg_t, gb_row, conv_qk, zeros_c, zeros_c, False)

        hcf, hcb, *state_c = _mlstm(mkc, mqtc, mvtc, grc, state0)
        hxf, hxb, *_ = _mlstm(mkx, mqtx, mvtx, grx, state_c)
        att_x = _attention(attn_sink, l, qtx, kx, vtx, kc, vtc, True)
        yf_x = tuple(_dft_half(dft_x, px)) if half_spectrum else _dft(*dft_x, px)
        mix_x = (yf_x, att_x, hxf, hxb, ogx)
        if last:
            x = _ffn(x, l, 2, mod, x_row, norm_g, *wf, mixer=mix_x, w_o=wo, g_final=g_final)
        else:
            x, *wf_next = _ffn(x, l, 2, mod, x_row, norm_g, *wf, mixer=mix_x, w_o=wo,
                               cast_next=(w_ffn_in, w_ffn_out, l + 1, 0))
            att_c = _attention(attn_sink, l, qtc, None, None, kc, vtc, False)
            yf_c = _dft(cnc, snc, pc)
            xc = _ffn(xc, l, 2, mod, c_row, norm_g, *wf, mixer=(yf_c, att_c, hcf, hcb, ogc), w_o=wo)
            wf = wf_next
    return x
```

```python
import functools
import math

import jax
import jax.numpy as jnp
from jax import lax
from jax.experimental import pallas as pl
from jax.experimental.pallas import tpu as pltpu

F32 = jnp.float32
BF16 = jnp.bfloat16
HI = lax.Precision.HIGHEST

GRID_W = 64
EPS = 1e-6
NEG = -1e30
LOG2E = math.log2(math.e)
FOURIER_GROUPS = 4
FOURIER_CH = 64
D_FOURIER = FOURIER_GROUPS * FOURIER_CH
HEAD_DIM = 64
ATT_HEADS = 8
ATT_KV_HEADS = 2
D_ATT = ATT_HEADS * HEAD_DIM
D_KV = ATT_KV_HEADS * HEAD_DIM
ATT_BLOCK = 128
WINDOW = 128
ROPE_BASE = 10000.0
MLSTM_HEADS = 4
MLSTM_DIM = 64
D_MLSTM = MLSTM_HEADS * MLSTM_DIM
N_MOD = 9
N_GATES = 4 * MLSTM_HEADS

LANES = 128
SUBLANES = 8
VMEM_LIMIT = 56 * 1024 * 1024

TM = 512
TM_INPROJ = 1024
TM_FFN = 1024
FFN_PART_ROWS = 256
TK_DFT = 512
DFT_K0 = 64
DFT_TAIL = 16
TQ_ATT = 1024
ATT_STAGE_ORDER = (0, 1, 2, 3)
ATT_SKEW = (1, 2, 3)
MLSTM_L = 128
MLSTM_CHUNKS_PER_STEP = 1

C_P = 0
C_Q = C_P + 2 * D_FOURIER
C_K = C_Q + D_ATT
C_V = C_K + D_KV
C_MQ = C_V + D_KV
C_MK = C_MQ + D_MLSTM
C_MV = C_MK + D_MLSTM
C_MO = C_MV + D_MLSTM
C_END = C_MO + D_MLSTM


def _cparams(sem):
    return pltpu.CompilerParams(dimension_semantics=sem, vmem_limit_bytes=VMEM_LIMIT)


def _dot(a, b):
    return jnp.dot(a, b, preferred_element_type=F32)


def _dot_nt(a, b):
    return lax.dot_general(a, b, (((1,), (1,)), ((), ())), preferred_element_type=F32)


def _dot_hi(a, b):
    return jnp.dot(a, b, preferred_element_type=F32, precision=HI)


def _dot_split3(x, onehot):
    rows = x.shape[0]
    hi = x.astype(BF16)
    r1 = x - hi.astype(F32)
    mid = r1.astype(BF16)
    lo = (r1 - mid.astype(F32)).astype(BF16)
    y = _dot(jnp.concatenate([hi, mid, lo], axis=0), onehot)
    return y[0:rows] + y[rows:2 * rows] + y[2 * rows:3 * rows]


def _rms_mod(xt, g, shift, scale):
    ms = jnp.mean(xt * xt, axis=-1, keepdims=True)
    y = xt * lax.rsqrt(ms + EPS) * g
    return y * (1.0 + scale) + shift


def _silu(t):
    return t * jax.nn.sigmoid(t)


def _log_sigmoid(t):
    return jnp.minimum(t, 0.0) - jnp.log1p(jnp.exp(-jnp.abs(t)))


def _mod_kernel(c_ref, w_ref, b_ref, o_ref):
    sc = _silu(c_ref[...])
    o_ref[...] = _dot(sc.astype(BF16), w_ref[...].astype(BF16)) + b_ref[...]


def _modulation(c_all, w_ada, b_ada):
    depth, d, nd = w_ada.shape
    r = c_all.shape[0]
    tn = 1024
    return pl.pallas_call(
        _mod_kernel,
        grid=(depth, nd // tn),
        in_specs=[pl.BlockSpec((r, d), lambda l, j: (0, 0)),
                  pl.BlockSpec((None, d, tn), lambda l, j: (l, 0, j)),
                  pl.BlockSpec((None, 1, tn), lambda l, j: (l, 0, j))],
        out_specs=pl.BlockSpec((None, r, tn), lambda l, j: (l, 0, j)),
        out_shape=jax.ShapeDtypeStruct((depth, r, nd), F32),
        compiler_params=_cparams(("parallel", "parallel")),
        name="modulation",
    )(c_all, w_ada, b_ada.reshape(depth, 1, nd))


def _fold_kernel(win_ref, wf_ref, cc_ref, sc_ref, o_ref, oc_ref, rest_ref, *, scale, scale_ctx):
    rest_ref[...] = win_ref[:, D_FOURIER:D_FOURIER + C_END - C_Q].astype(BF16)
    wf = wf_ref[...]
    mc = _dot_hi(cc_ref[...], wf)
    ms = _dot_hi(sc_ref[...], wf)
    wa = win_ref[:, :D_FOURIER]
    wc = _dot_hi(wa, mc)
    ws = _dot_hi(wa, ms)
    o_ref[:, :D_FOURIER] = (wc * scale).astype(BF16)
    o_ref[:, D_FOURIER:] = (ws * scale).astype(BF16)
    oc_ref[:, :D_FOURIER] = (wc * scale_ctx).astype(BF16)
    oc_ref[:, D_FOURIER:] = (ws * scale_ctx).astype(BF16)


def _fourier_fold(w_in, wf_bd, cc_bd, sc_bd, scale, scale_ctx):
    depth, d, d_in = w_in.shape
    out = pl.BlockSpec((None, d, 2 * D_FOURIER), lambda l: (l, 0, 0))
    n_rest = C_END - C_Q
    return pl.pallas_call(
        functools.partial(_fold_kernel, scale=scale, scale_ctx=scale_ctx),
        grid=(depth,),
        in_specs=[pl.BlockSpec((None, d, d_in), lambda l: (l, 0, 0)),
                  pl.BlockSpec((None, D_FOURIER, D_FOURIER), lambda l: (l, 0, 0)),
                  pl.BlockSpec((D_FOURIER, D_FOURIER), lambda l: (0, 0)),
                  pl.BlockSpec((D_FOURIER, D_FOURIER), lambda l: (0, 0))],
        out_specs=[out, out, pl.BlockSpec((None, d, n_rest), lambda l: (l, 0, 0))],
        out_shape=[jax.ShapeDtypeStruct((depth, d, 2 * D_FOURIER), BF16)] * 2
        + [jax.ShapeDtypeStruct((depth, d, n_rest), BF16)],
        compiler_params=_cparams(("parallel",)),
        name="fourier_fold",
    )(w_in, wf_bd, cc_bd, sc_bd)


def _ffn_kernel(*refs, sub, d_ff, mix, split_yf, final, cast_next, split):
    x_ref, mod_ref, g_ref, win_ref, wout_ref = refs[:5]
    if cast_next:
        o_ref, nin_o_ref, nout_o_ref = refs[-3:]
        nin_ref, nout_ref = refs[-5:-3]
        nin_o_ref[...] = nin_ref[...].astype(BF16)
        nout_o_ref[...] = nout_ref[...].astype(BF16)
        rest = list(refs[5:-5])
    else:
        o_ref = refs[-1]
        rest = list(refs[5:-1])
    shift = mod_ref[3 * sub:3 * sub + 1, :]
    scale = mod_ref[3 * sub + 1:3 * sub + 2, :]
    gate = mod_ref[3 * sub + 2:3 * sub + 3, :]
    tm = x_ref.shape[0]
    rs = tm // split
    def stages(part):
        rows = slice(part * rs, (part + 1) * rs)
        v = {}

        def s_norm():
            xt = x_ref[rows, :]
            if mix:
                if split_yf:
                    yt_ref, yb_ref, att_ref, hf_ref, hb_ref, og_ref, wo_ref = rest[:7]
                    in_top = pl.program_id(1) < pl.num_programs(1) // 2
                    yf = jnp.where(in_top, yt_ref[rows, :], yb_ref[rows, :])
                else:
                    yf_ref, att_ref, hf_ref, hb_ref, og_ref, wo_ref = rest[:6]
                    yf = yf_ref[rows, :]
                mh = (og_ref[rows, :].astype(F32) * (hf_ref[rows, :] + hb_ref[rows, :])).astype(BF16)
                a0, a1 = D_FOURIER, D_FOURIER + D_ATT
                y = (_dot(yf, wo_ref[:a0, :]) + _dot(att_ref[rows, :], wo_ref[a0:a1, :])
                     + _dot(mh, wo_ref[a1:, :]))
                xt = xt + mod_ref[5:6, :] * y
            v['xt'] = xt
            v['h'] = _rms_mod(xt, g_ref[sub:sub + 1, :], shift, scale).astype(BF16)

        def s_up():
            v['gu'] = _dot(v.pop('h'), win_ref[...])

        def s_act():
            gu = v.pop('gu')
            v['act'] = (_silu(gu[:, :d_ff]) * gu[:, d_ff:]).astype(BF16)

        def s_down():
            y = v.pop('xt') + (0.5 * gate) * _dot(v.pop('act'), wout_ref[...])
            if final:
                gf_ref = rest[-1]
                ms = jnp.mean(y * y, axis=-1, keepdims=True)
                y = y * lax.rsqrt(ms + EPS) * gf_ref[...]
            o_ref[rows, :] = y

        return s_norm, s_up, s_act, s_down

    parts = [stages(part) for part in range(split)]
    for step in range(split + 3):
        for stage in range(4):
            part = step - stage
            if 0 <= part < split:
                parts[part][stage]()


def _slab_count(rows, steps):
    for nblk in range(steps, 0, -1):
        if rows % nblk == 0 and (rows // nblk) % (2 * SUBLANES) == 0:
            return nblk
    return 1


def _ffn(x, layer, sub, mod, mod_row, norm_g, w_in, w_out, mixer=None, w_o=None, g_final=None, cast_next=None):
    b, n, d = x.shape
    d_ff = w_out.shape[0]
    mix = mixer is not None
    tm = min(TM if mix else TM_FFN, n)
    nt = n // tm
    split = max(tm // FFN_PART_ROWS, 1)
    final = g_final is not None
    tok = lambda w: pl.BlockSpec((None, tm, w), lambda bi, t: (bi, t, 0))
    whole = lambda r, c: pl.BlockSpec((r, c), lambda bi, t: (0, 0), pipeline_mode=pl.Buffered(1))
    in_specs = [tok(d),
                pl.BlockSpec((None, None, N_MOD, d), lambda bi, t: (layer, mod_row(bi), 0, 0)),
                pl.BlockSpec((None, 3, d), lambda bi, t: (layer, 0, 0)),
                whole(d, 2 * d_ff), whole(d_ff, d)]
    args = [x, mod, norm_g, w_in, w_out]
    split_yf = mix and isinstance(mixer[0], tuple)
    if mix:
        if split_yf:
            hh = nt // 2
            in_specs += [pl.BlockSpec((None, tm, D_FOURIER), lambda bi, t: (bi, jnp.minimum(t, hh - 1), 0)),
                         pl.BlockSpec((None, tm, D_FOURIER), lambda bi, t: (bi, jnp.maximum(t - hh, 0), 0))]
            args += list(mixer[0])
        else:
            in_specs.append(tok(D_FOURIER))
            args.append(mixer[0])
        in_specs += [tok(D_ATT), tok(D_MLSTM), tok(D_MLSTM), tok(D_MLSTM),
                     pl.BlockSpec((None, d, d), lambda bi, t: (layer, 0, 0), pipeline_mode=pl.Buffered(1))]
        args += list(mixer[1:]) + [w_o]
    if final:
        in_specs.append(pl.BlockSpec((1, d), lambda bi, t: (0, 0)))
        args.append(g_final.reshape(1, d))
    out_specs = [tok(d)]
    out_shape = [jax.ShapeDtypeStruct((b, n, d), F32)]
    if cast_next is not None:
        nin, nout, nl, nh_ = cast_next
        steps = b * nt
        for arr, rows, cols in ((nin, d, 2 * d_ff), (nout, d_ff, d)):
            nblk = _slab_count(rows, steps)
            slab = lambda bi, t, nblk=nblk: jnp.minimum(bi * nt + t, nblk - 1)
            in_specs.append(pl.BlockSpec((None, None, rows // nblk, cols),
                                         lambda bi, t, slab=slab: (nl, nh_, slab(bi, t), 0)))
            out_specs.append(pl.BlockSpec((rows // nblk, cols), lambda bi, t, slab=slab: (slab(bi, t), 0)))
            out_shape.append(jax.ShapeDtypeStruct((rows, cols), BF16))
            args.append(arr)
    res = pl.pallas_call(
        functools.partial(_ffn_kernel, sub=sub, d_ff=d_ff, mix=mix, split_yf=split_yf, final=final, split=split,
                          cast_next=cast_next is not None),
        grid=(b, nt),
        in_specs=in_specs,
        out_specs=out_specs,
        out_shape=out_shape,
        compiler_params=_cparams(("arbitrary", "arbitrary")),
        name="ffn_mix" if mix else "ffn",
    )(*args)
    return res if cast_next is not None else res[0]


def _store_chunks(ref, t):
    for c in range(ref.shape[0]):
        ref[c] = t[:, c * MLSTM_L:(c + 1) * MLSTM_L]


def _rope128(t, cos, sin_signed, low16):
    fwd = pltpu.roll(t, LANES - 16, axis=1)
    bwd = pltpu.roll(t, 16, axis=1)
    return t * cos + jnp.where(low16, fwd, bwd) * sin_signed


def _inproj_kernel(x_ref, xp_ref, xn_ref, mod_ref, g_ref, wp_ref, w_ref, wgt_ref, gbr_ref, cw_ref,
                   cos_ref, sin_ref,
                   p_ref, qt_ref, k_ref, vt_ref, mk_ref, mqt_ref, mvt_ref, og_ref, gr_ref, *, rope):
    t = pl.program_id(1)
    nt = pl.num_programs(1)
    tm = x_ref.shape[0]
    shift = mod_ref[3:4, :]
    scale = mod_ref[4:5, :]
    g = g_ref[1:2, :]
    h = _rms_mod(x_ref[...], g, shift, scale).astype(BF16)
    seg = lambda c0, c1: _dot(h, w_ref[:, c0 - C_Q:c1 - C_Q])
    lane = lax.broadcasted_iota(jnp.int32, (1, LANES), 1)
    low16 = (lane & 16) == 0
    qscale = HEAD_DIM ** -0.5 * LOG2E
    if rope:
        cos = cos_ref[...]
        sin = sin_ref[...]

    up = _dot(h, wp_ref[...])
    uq = seg(C_Q, C_K)
    p_ref[...] = up.astype(BF16)

    ukv = seg(C_K, C_MQ)
    for cb in range(D_ATT // LANES):
        qb = uq[:, cb * LANES:(cb + 1) * LANES]
        if rope:
            qb = _rope128(qb, cos, sin, low16)
        qt_ref[cb * LANES:(cb + 1) * LANES, :] = jnp.transpose(qb * qscale).astype(BF16)

    uqk = seg(C_MQ, C_MV)
    xh = jnp.concatenate([xp_ref[...], xn_ref[...]], axis=0)
    hh = _rms_mod(xh, g, shift, scale).astype(BF16)
    uh = _dot(hh, w_ref[:, C_MQ - C_Q:C_MV - C_Q])
    kb = ukv[:, :D_KV]
    if rope:
        kb = _rope128(kb, cos, sin, low16)
    k_ref[...] = kb.astype(BF16)
    vt_ref[...] = jnp.transpose(ukv[:, D_KV:]).astype(BF16)

    uvo = seg(C_MV, C_END)
    prev_row = jnp.where(t > 0, uh[SUBLANES - 1:SUBLANES, :], 0.0)
    next_row = jnp.where(t < nt - 1, uh[SUBLANES:SUBLANES + 1, :], 0.0)
    row = lax.broadcasted_iota(jnp.int32, (tm, 1), 0)
    u_prev = jnp.where(row == 0, prev_row, pltpu.roll(uqk, 1, axis=0))
    u_next = jnp.where(row == tm - 1, next_row, pltpu.roll(uqk, tm - 1, axis=0))
    cv = u_prev * cw_ref[0:1, :] + uqk * cw_ref[1:2, :] + u_next * cw_ref[2:3, :]
    qk = _silu(cv)
    _store_chunks(mqt_ref, jnp.transpose(qk[:, :D_MLSTM]).astype(BF16))
    mk_ref[...] = (qk[:, D_MLSTM:] * (MLSTM_DIM ** -0.5)).astype(BF16)

    half = N_GATES // 2
    zr = _dot_nt(wgt_ref[...], h) + gbr_ref[...]
    _store_chunks(mvt_ref, jnp.transpose(uvo[:, :D_MLSTM]).astype(BF16))
    og_ref[...] = jax.nn.sigmoid(uvo[:, D_MLSTM:]).astype(BF16)
    rowg = lax.broadcasted_iota(jnp.int32, (N_GATES, 1), 0)
    gr_ref[...] = jnp.where(rowg >= half, _log_sigmoid(zr), zr)


def _inproj(x, layer, mod, mod_row, norm_g, w_p, w_rest, wg_t, gb_row, conv_w, cos128, sin128, rope):
    b, n, d = x.shape
    tm = min(TM_INPROJ, n)
    nt = n // tm
    r8 = tm // SUBLANES
    n8 = n // SUBLANES
    lay = lambda *blk: pl.BlockSpec((None,) + blk, lambda bi, t: (layer,) + (0,) * len(blk))
    tok = lambda w: pl.BlockSpec((None, tm, w), lambda bi, t: (bi, t, 0))
    trn = lambda rows: pl.BlockSpec((None, rows, tm), lambda bi, t: (bi, 0, t))
    in_specs = [tok(d),
                pl.BlockSpec((None, SUBLANES, d), lambda bi, t: (bi, jnp.maximum(t * r8 - 1, 0), 0)),
                pl.BlockSpec((None, SUBLANES, d), lambda bi, t: (bi, jnp.minimum((t + 1) * r8, n8 - 1), 0)),
                pl.BlockSpec((None, None, N_MOD, d), lambda bi, t: (layer, mod_row(bi), 0, 0)),
                lay(3, d), lay(d, C_Q), lay(d, C_END - C_Q), lay(N_GATES, d), lay(N_GATES, 1), lay(3, 2 * D_MLSTM),
                pl.BlockSpec((tm, LANES), lambda bi, t: (t, 0)),
                pl.BlockSpec((tm, LANES), lambda bi, t: (t, 0))]
    chk = pl.BlockSpec((None, tm // MLSTM_L, D_MLSTM, MLSTM_L), lambda bi, t: (bi, t, 0, 0))
    chk_shape = jax.ShapeDtypeStruct((b, n // MLSTM_L, D_MLSTM, MLSTM_L), BF16)
    out_specs = [tok(2 * D_FOURIER), trn(D_ATT), tok(D_KV), trn(D_KV),
                 tok(D_MLSTM), chk, chk, tok(D_MLSTM), trn(N_GATES)]
    shp = lambda w, dt: jax.ShapeDtypeStruct((b, n, w), dt)
    shpt = lambda rows, dt: jax.ShapeDtypeStruct((b, rows, n), dt)
    out_shape = [shp(2 * D_FOURIER, BF16), shpt(D_ATT, BF16), shp(D_KV, BF16), shpt(D_KV, BF16),
                 shp(D_MLSTM, BF16), chk_shape, chk_shape, shp(D_MLSTM, BF16),
                 shpt(N_GATES, F32)]
    return pl.pallas_call(
        functools.partial(_inproj_kernel, rope=rope),
        grid=(b, nt),
        in_specs=in_specs,
        out_specs=out_specs,
        out_shape=out_shape,
        compiler_params=_cparams(("parallel", "parallel")),
        name="inproj",
    )(x, x, x, mod, norm_g, w_p, w_rest, wg_t, gb_row, conv_w, cos128, sin128)


def _dft_kernel(cn_ref, sn_ref, p_ref, y_ref):
    y = _dot(cn_ref[...], p_ref[:, :D_FOURIER]) + _dot(sn_ref[...], p_ref[:, D_FOURIER:])
    y_ref[...] = y.astype(BF16)


def _dft(cn, sn, p):
    b, n, _ = p.shape
    tk = min(TK_DFT, n)
    return pl.pallas_call(
        _dft_kernel,
        grid=(n // tk, b),
        in_specs=[pl.BlockSpec((tk, n), lambda i, bi: (i, 0)),
                  pl.BlockSpec((tk, n), lambda i, bi: (i, 0)),
                  pl.BlockSpec((None, n, 2 * D_FOURIER), lambda i, bi: (bi, 0, 0))],
        out_specs=pl.BlockSpec((None, tk, D_FOURIER), lambda i, bi: (bi, i, 0)),
        out_shape=jax.ShapeDtypeStruct((b, n, D_FOURIER), BF16),
        compiler_params=_cparams(("parallel", "parallel")),
        name="dft",
    )(cn, sn, p)


def _anti_identity(rows, cols, offset):
    r = lax.broadcasted_iota(jnp.int32, (rows, cols), 0)
    c = lax.broadcasted_iota(jnp.int32, (rows, cols), 1)
    return jnp.where(r + c == offset, 1.0, 0.0).astype(BF16)


def _dft_fold(p_ref, pf_ref, mid_ref):
    n = p_ref.shape[0]
    nh = n // 2
    tk = TK_DFT
    flip = _anti_identity(tk, tk, tk - 1)
    nblk = nh // tk
    rev = jnp.concatenate([_dot(flip, p_ref[n - (i + 1) * tk:n - i * tk, :]) for i in range(nblk)], axis=0)
    row = lax.broadcasted_iota(jnp.int32, (nh, 1), 0)
    mirror = jnp.where(row == 0, 0.0, pltpu.roll(rev, 1, axis=0))
    top = p_ref[0:nh, :].astype(F32)
    pf_ref[:, :D_FOURIER] = (top[:, :D_FOURIER] + mirror[:, :D_FOURIER]).astype(BF16)
    pf_ref[:, D_FOURIER:] = (top[:, D_FOURIER:] - mirror[:, D_FOURIER:]).astype(BF16)
    mid_ref[...] = jnp.broadcast_to(p_ref[nh:nh + 1, :D_FOURIER].astype(F32), mid_ref.shape)


def _dft_half_kernel(ca_ref, sa_ref, cb_ref, sb_ref, p_ref, top_ref, bot_ref, cn_scr, sn_scr, pf_scr, mid_scr):
    i = pl.program_id(0)
    bi = pl.program_id(1)
    tk = top_ref.shape[0]
    groups = tk // DFT_K0

    @pl.when(i == 0)
    def _():
        _dft_fold(p_ref, pf_scr.at[bi], mid_scr.at[bi])

    pf_ref = pf_scr.at[bi]
    mid_ref = mid_scr.at[bi]

    @pl.when(bi == 0)
    def _():
        cb = cb_ref[...]
        sb = sb_ref[...]
        for r in range(groups + 1):
            nr = DFT_K0 if r < groups else DFT_TAIL
            ca = ca_ref[pl.ds(i * groups + r, 1), :]
            sa = sa_ref[pl.ds(i * groups + r, 1), :]
            cn_scr[r * DFT_K0:r * DFT_K0 + nr, :] = (ca * cb[:nr] - sa * sb[:nr]).astype(BF16)
            sn_scr[r * DFT_K0:r * DFT_K0 + nr, :] = (sa * cb[:nr] + ca * sb[:nr]).astype(BF16)

    e = _dot(cn_scr[...], pf_ref[:, :D_FOURIER])
    o = _dot(sn_scr[...], pf_ref[:, D_FOURIER:])
    rows = lax.broadcasted_iota(jnp.int32, (tk + DFT_TAIL, 1), 0)
    mid = mid_ref[0:1, :]
    e = e + jnp.where((rows & 1) == 0, mid, -mid)
    top_ref[...] = (e + o)[:tk].astype(BF16)
    bot_ref[...] = _dot(_anti_identity(tk, tk + DFT_TAIL, tk), (e - o).astype(BF16)).astype(BF16)


def _dft_half(tables, p):
    ca, sa, cb, sb = tables
    b, n, _ = p.shape
    nh = n // 2
    tk = TK_DFT
    nt = nh // tk
    whole = lambda a: pl.BlockSpec(a.shape, lambda i, bi: (0, 0))
    return pl.pallas_call(
        _dft_half_kernel,
        grid=(nt, b),
        in_specs=[whole(ca), whole(sa), whole(cb), whole(sb),
                  pl.BlockSpec((None, n, 2 * D_FOURIER), lambda i, bi: (jnp.where(i == 0, bi, b - 1), 0, 0))],
        out_specs=[pl.BlockSpec((None, tk, D_FOURIER), lambda i, bi: (bi, i, 0)),
                   pl.BlockSpec((None, tk, D_FOURIER), lambda i, bi: (bi, nt - 1 - i, 0))],
        out_shape=[jax.ShapeDtypeStruct((b, nh, D_FOURIER), BF16), jax.ShapeDtypeStruct((b, nh, D_FOURIER), BF16)],
        scratch_shapes=[pltpu.VMEM((tk + DFT_TAIL, nh), BF16), pltpu.VMEM((tk + DFT_TAIL, nh), BF16),
                        pltpu.VMEM((b, nh, 2 * D_FOURIER), BF16), pltpu.VMEM((b, SUBLANES, D_FOURIER), F32)],
        compiler_params=_cparams(("arbitrary", "arbitrary")),
        name="dft_half",
    )(ca, sa, cb, sb, p)


def _attn_kernel(sink_ref, qt_ref, *rest, layer, local):
    if local:
        kp_ref, kc_ref, kn_ref, vp_ref, vc_ref, vn_ref, kx_ref, vxt_ref, o_ref = rest
    else:
        kx_ref, vxt_ref, o_ref = rest
    blk = ATT_BLOCK
    group = ATT_HEADS // ATT_KV_HEADS
    gw = group * blk
    cl = kx_ref.shape[0]
    nq = qt_ref.shape[1] // blk
    lane_head = lax.broadcasted_iota(jnp.int32, (1, gw), 1) // blk
    sinks = []
    for kv in range(ATT_KV_HEADS):
        sink = jnp.zeros((1, gw), F32)
        for i in range(group):
            sink = jnp.where(lane_head == i, sink_ref[layer, kv * group + i] * LOG2E, sink)
        sinks.append(sink)
    kx = kx_ref[...]
    vxt = vxt_ref[...]
    if local:
        j = pl.program_id(1)
        nb = pl.num_programs(1) * nq
        ks = lax.broadcasted_iota(jnp.int32, (blk, blk), 0)
        qi = lax.broadcasted_iota(jnp.int32, (blk, blk), 1)
        in_p = jnp.abs(ks - blk - qi) <= WINDOW
        in_n = jnp.abs(ks + blk - qi) <= WINDOW
        kloc = jnp.concatenate([kp_ref[...], kc_ref[...], kn_ref[...]], axis=0)
        vloc = jnp.concatenate([vp_ref[...], vc_ref[...], vn_ref[...]], axis=1)
    zq = jnp.zeros((HEAD_DIM, gw), BF16)

    def scores(i, kv):
        qt_blk = qt_ref[:, i * blk:(i + 1) * blk]
        r0 = kv * group * HEAD_DIM
        qblk = jnp.concatenate([qt_blk[r0 + h * HEAD_DIM:r0 + (h + 1) * HEAD_DIM, :] for h in range(group)], axis=1)
        qt_g = jnp.concatenate([qblk, zq] if kv == 0 else [zq, qblk], axis=0)
        if not local:
            return [_dot(kx, qt_g)]
        jb = j * nq + i
        bias_p = jnp.where(in_p & (jb > 0), 0.0, NEG)
        bias_n = jnp.where(in_n & (jb < nb - 1), 0.0, NEG)
        s = _dot(jnp.concatenate([kx, kloc[i * blk:(i + 3) * blk, :]], axis=0), qt_g)
        add = lambda sg, bias: jnp.concatenate([sg[:, h * blk:(h + 1) * blk] + bias for h in range(group)], axis=1)
        return [s[:cl], add(s[cl:cl + blk], bias_p), s[cl + blk:cl + 2 * blk], add(s[cl + 2 * blk:], bias_n)]

    def colmax(segs, kv):
        m = sinks[kv]
        for sg in segs:
            m = jnp.maximum(m, jnp.max(sg, axis=0, keepdims=True))
        return m

    def softmax(segs, kv, m):
        den = jnp.exp2(sinks[kv] - m)
        ps = []
        for sg in segs:
            p = jnp.exp2(sg - m)
            den = den + jnp.sum(p, axis=0, keepdims=True)
            ps.append(p.astype(BF16))
        return (jnp.concatenate(ps, axis=0) if len(ps) > 1 else ps[0]), den

    def values(i, kv, p_all, den):
        vtall = jnp.concatenate([vxt, vloc[:, i * blk:(i + 3) * blk]], axis=1) if local else vxt
        ot = _dot(vtall[kv * HEAD_DIM:(kv + 1) * HEAD_DIM, :], p_all) * (1.0 / den)
        for cb in range(group // 2):
            pair = jnp.concatenate([ot[:, (2 * cb) * blk:(2 * cb + 1) * blk],
                                    ot[:, (2 * cb + 1) * blk:(2 * cb + 2) * blk]], axis=0)
            c0 = (kv * (group // 2) + cb) * LANES
            o_ref[i * blk:(i + 1) * blk, c0:c0 + LANES] = jnp.transpose(pair).astype(BF16)

    chains = [(i, kv) for i in range(nq) for kv in range(ATT_KV_HEADS)]
    segs_q, max_q, soft_q = {}, {}, {}
    lag_max, lag_soft, lag_val = ATT_SKEW
    def run_scores(c):
        segs_q[c] = scores(*chains[c])

    def run_max(c):
        max_q[c] = colmax(segs_q[c], chains[c][1])

    def run_soft(c):
        soft_q[c] = softmax(segs_q.pop(c), chains[c][1], max_q.pop(c))

    def run_values(c):
        values(*chains[c], *soft_q.pop(c))

    stages = ((0, run_scores), (lag_max, run_max), (lag_soft, run_soft), (lag_val, run_values))
    for step in range(len(chains) + lag_val):
        for idx in ATT_STAGE_ORDER:
            lag, fn = stages[idx]
            if 0 <= step - lag < len(chains):
                fn(step - lag)


def _attention(sink, layer, qt, k, vt, kx, vxt, local):
    b, _, n = qt.shape
    cl = kx.shape[1]
    blk = ATT_BLOCK
    tq = min(TQ_ATT, n)
    nq = tq // blk
    nb = n // blk
    in_specs = [pl.BlockSpec(memory_space=pltpu.SMEM),
                pl.BlockSpec((None, D_ATT, tq), lambda bi, j: (bi, 0, j))]
    args = [sink, qt]
    if local:
        pj = lambda j: jnp.maximum(j * nq - 1, 0)
        nj = lambda j: jnp.minimum((j + 1) * nq, nb - 1)
        in_specs += [pl.BlockSpec((None, blk, D_KV), lambda bi, j: (bi, pj(j), 0)),
                     pl.BlockSpec((None, tq, D_KV), lambda bi, j: (bi, j, 0)),
                     pl.BlockSpec((None, blk, D_KV), lambda bi, j: (bi, nj(j), 0)),
                     pl.BlockSpec((None, D_KV, blk), lambda bi, j: (bi, 0, pj(j))),
                     pl.BlockSpec((None, D_KV, tq), lambda bi, j: (bi, 0, j)),
                     pl.BlockSpec((None, D_KV, blk), lambda bi, j: (bi, 0, nj(j)))]
        args += [k, k, k, vt, vt, vt]
    in_specs += [pl.BlockSpec((None, cl, D_KV), lambda bi, j: (bi, 0, 0)),
                 pl.BlockSpec((None, D_KV, cl), lambda bi, j: (bi, 0, 0))]
    args += [kx, vxt]
    return pl.pallas_call(
        functools.partial(_attn_kernel, layer=layer, local=local),
        grid=(b, n // tq),
        in_specs=in_specs,
        out_specs=pl.BlockSpec((None, tq, D_ATT), lambda bi, j: (bi, j, 0)),
        out_shape=jax.ShapeDtypeStruct((b, n, D_ATT), BF16),
        compiler_params=_cparams(("parallel", "parallel")),
        name="attention",
    )(*args)


def _mlstm_stages(d, k_ref, qt_ref, vt_ref, gr_ref, h_ref, c_scr, n_scr, m_scr, consts):
    L = MLSTM_L
    H = MLSTM_HEADS
    DH = MLSTM_DIM
    tri, valid_t, lane_h, row_h, blockdiag, nmask = consts
    r0 = d * H
    v = {}

    def stage_in():
        k = k_ref[...]
        qt = qt_ref[...]
        gr = gr_ref[...]
        cum = _dot_split3(gr, tri)
        li = gr[r0:r0 + H, :]
        a = cum[2 * H + r0:2 * H + r0 + H, :]
        a_tot = a[:, L - 1:L] if d == 0 else a[:, 0:1]
        m_in = m_scr[r0:r0 + H, 0:1]
        c_old = c_scr[d]
        n_old = n_scr[d]
        v.update(k=k, qt=qt, li=li, a=a, a_tot=a_tot, m_in=m_in, c_old=c_old, n_old=n_old,
                 g_prev=a + m_in,
                 b_t=jnp.transpose(jnp.concatenate([li - a, jnp.zeros((L - H, L), F32)], axis=0)),
                 inter=_dot(jnp.concatenate([c_old, n_old], axis=0).astype(BF16), qt))

    def stage_scores():
        k = v['k']
        kst = jnp.concatenate([jnp.where(lane_h == hd, k, jnp.zeros_like(k)) for hd in range(H)], axis=0)
        v['s_t'] = _dot(kst, v.pop('qt'))

    def stage_weights(heads):
        p_parts, m_rows, wp_rows, den_rows = (v.setdefault(key, []) for key in ('p_parts', 'm_rows', 'wp_rows', 'den_rows'))
        for hd in heads:
            d_t = jnp.where(valid_t, v['b_t'][:, hd:hd + 1] + v['a'][hd:hd + 1, :], NEG)
            g_prev = v['g_prev'][hd:hd + 1, :]
            m_t = jnp.maximum(g_prev, jnp.max(d_t, axis=0, keepdims=True))
            p_t = v['s_t'][hd * L:(hd + 1) * L, :] * jnp.exp(d_t - m_t)
            w_prev = jnp.exp(g_prev - m_t)
            den = jnp.sum(p_t, axis=0, keepdims=True) + w_prev * v['inter'][D_MLSTM + hd:D_MLSTM + hd + 1, :]
            p_parts.append(p_t.astype(BF16))
            m_rows.append(m_t)
            wp_rows.append(w_prev)
            den_rows.append(den)

    def stage_out():
        vt = vt_ref[...]
        k = v['k']
        vbd = jnp.concatenate([jnp.where(row_h == hd, vt, jnp.zeros_like(vt)) for hd in range(H)], axis=1)
        num_t = _dot(vbd, jnp.concatenate(v['p_parts'], axis=0))
        outs = []
        for hd in range(H):
            inv = 1.0 / jnp.maximum(jnp.abs(v['den_rows'][hd]), jnp.exp(-v['m_rows'][hd]))
            sl = slice(hd * DH, (hd + 1) * DH)
            outs.append((num_t[sl, :] + v['wp_rows'][hd] * v['inter'][sl, :]) * inv)
        h_ref[...] = jnp.transpose(jnp.concatenate(outs, axis=0))

    def stage_state():
        vt = vt_ref[...]
        k = v['k']
        a_tot, m_in = v['a_tot'], v['m_in']
        dl = a_tot - v['a'] + v['li']
        m_new = jnp.maximum(a_tot + m_in, jnp.max(dl, axis=1, keepdims=True))
        w = jnp.exp(dl - m_new)
        decay = jnp.exp(a_tot + m_in - m_new)
        wexp = jnp.concatenate([jnp.broadcast_to(w[hd:hd + 1, :], (DH, L)) for hd in range(H)], axis=0)
        dcol = jnp.concatenate([jnp.broadcast_to(decay[hd:hd + 1, :], (DH, 1)) for hd in range(H)], axis=0)
        upd = _dot((vt.astype(F32) * wexp).astype(BF16), k)
        c_scr[d] = dcol * v['c_old'] + jnp.where(blockdiag, upd, 0.0)
        n_upd = _dot(jnp.concatenate([w, jnp.zeros((H, L), F32)], axis=0).astype(BF16), k)
        decay8 = jnp.concatenate([decay, jnp.zeros((H, 1), F32)], axis=0)
        n_scr[d] = jnp.where(nmask, decay8 * v['n_old'] + n_upd, 0.0)
        m_scr[r0:r0 + H, :] = jnp.broadcast_to(m_new, (H, LANES))

    return stage_in, stage_scores, stage_state, functools.partial(stage_weights, range(H)), stage_out


def _mlstm_kernel(kf_ref, qtf_ref, vtf_ref, grf_ref, kb_ref, qtb_ref, vtb_ref, grb_ref,
                  c0_ref, n0_ref, m0_ref, hf_ref, hb_ref, cfin_ref, nfin_ref, mfin_ref, c_scr, n_scr, m_scr):
    c = pl.program_id(0)
    nc = pl.num_programs(0)
    L = MLSTM_L
    H = MLSTM_HEADS
    nb = kf_ref.shape[0]

    @pl.when(c == 0)
    def _():
        c_scr[...] = c0_ref[...]
        n_scr[...] = n0_ref[...]
        m_scr[...] = m0_ref[...]

    ri = lax.broadcasted_iota(jnp.int32, (L, L), 0)
    ci = lax.broadcasted_iota(jnp.int32, (L, L), 1)
    upper = ci >= ri
    lower = ci <= ri
    triu = jnp.where(upper, 1.0, 0.0).astype(BF16)
    tril = jnp.where(lower, 1.0, 0.0).astype(BF16)
    lane_h = lax.broadcasted_iota(jnp.int32, (1, D_MLSTM), 1) // MLSTM_DIM
    row_h = lax.broadcasted_iota(jnp.int32, (D_MLSTM, 1), 0) // MLSTM_DIM
    blockdiag = row_h == lane_h
    nmask = lax.broadcasted_iota(jnp.int32, (2 * H, 1), 0) == lane_h
    cps = kf_ref.shape[1] // L
    chains = []
    for j in range(cps):
        for bi in range(nb):
            st = (c_scr.at[bi], n_scr.at[bi], m_scr.at[bi])
            for d, refs, consts in ((0, (kf_ref, qtf_ref, vtf_ref, grf_ref, hf_ref), (triu, upper)),
                                    (1, (kb_ref, qtb_ref, vtb_ref, grb_ref, hb_ref), (tril, lower))):
                ck = j if d == 0 else cps - 1 - j
                rows = pl.ds(ck * L, L)
                k_ref, qt_ref, vt_ref, gr_ref, h_ref = refs
                chains.append(_mlstm_stages(d, k_ref.at[bi, rows], qt_ref.at[bi, ck], vt_ref.at[bi, ck],
                                            gr_ref.at[bi, :, rows], h_ref.at[bi, rows], *st,
                                            consts + (lane_h, row_h, blockdiag, nmask)))
    n_stage = len(chains[0])
    stage_order = range(n_stage) if cps == 1 else range(n_stage - 1, -1, -1)
    for step in range(len(chains) + n_stage - 1):
        for stage in stage_order:
            ch = step - stage
            if 0 <= ch < len(chains):
                chains[ch][stage]()

    @pl.when(c == nc - 1)
    def _():
        cfin_ref[...] = c_scr[...]
        nfin_ref[...] = n_scr[...]
        mfin_ref[...] = m_scr[...]


def _mlstm(mk, mqt, mvt, grow, state):
    b, n, _ = mk.shape
    cps = max(c for c in range(1, MLSTM_CHUNKS_PER_STEP + 1) if n % (c * MLSTM_L) == 0)
    L = cps * MLSTM_L
    nc = n // L
    nh = 2 * MLSTM_HEADS
    tok = lambda cm: pl.BlockSpec((b, L, D_MLSTM), lambda c: (0, cm(c), 0))
    tr = lambda rows, cm: pl.BlockSpec((b, rows, L), lambda c: (0, 0, cm(c)))
    chk = lambda cm: pl.BlockSpec((b, cps, D_MLSTM, MLSTM_L), lambda c: (0, cm(c), 0, 0))
    fw = lambda c: c
    bw = lambda c: nc - 1 - c
    cspec = pl.BlockSpec((b, 2, D_MLSTM, D_MLSTM), lambda c: (0, 0, 0, 0))
    nspec = pl.BlockSpec((b, 2, nh, D_MLSTM), lambda c: (0, 0, 0, 0))
    mspec = pl.BlockSpec((b, nh, LANES), lambda c: (0, 0, 0))
    return pl.pallas_call(
        _mlstm_kernel,
        grid=(nc,),
        in_specs=[tok(fw), chk(fw), chk(fw), tr(N_GATES, fw),
                  tok(bw), chk(bw), chk(bw), tr(N_GATES, bw), cspec, nspec, mspec],
        out_specs=[tok(fw), tok(bw), cspec, nspec, mspec],
        out_shape=[jax.ShapeDtypeStruct((b, n, D_MLSTM), F32), jax.ShapeDtypeStruct((b, n, D_MLSTM), F32),
                   jax.ShapeDtypeStruct((b, 2, D_MLSTM, D_MLSTM), F32),
                   jax.ShapeDtypeStruct((b, 2, nh, D_MLSTM), F32),
                   jax.ShapeDtypeStruct((b, nh, LANES), F32)],
        scratch_shapes=[pltpu.VMEM((b, 2, D_MLSTM, D_MLSTM), F32), pltpu.VMEM((b, 2, nh, D_MLSTM), F32),
                        pltpu.VMEM((b, nh, LANES), F32)],
        compiler_params=_cparams(("arbitrary",)),
        name="mlstm",
    )(mk, mqt, mvt, grow, mk, mqt, mvt, grow, *state)


def _rope_tables(n):
    rows = n // GRID_W
    row = jnp.broadcast_to(jnp.arange(rows, dtype=F32)[:, None], (rows, GRID_W)).reshape(n)
    colp = jnp.broadcast_to(jnp.arange(GRID_W, dtype=F32)[None, :], (rows, GRID_W)).reshape(n)
    nf = HEAD_DIM // 4
    inv = ROPE_BASE ** (-jnp.arange(nf, dtype=F32) / nf)
    ar = row[:, None] * inv
    ac = colp[:, None] * inv
    ang = jnp.concatenate([ar, ar, ac, ac], axis=-1)
    cos = jnp.cos(ang)
    sin = jnp.sin(ang)
    sign = jnp.where((jnp.arange(HEAD_DIM) & 16) == 0, -1.0, 1.0).astype(F32)
    reps = LANES // HEAD_DIM
    return jnp.tile(cos, (1, reps)), jnp.tile(sin * sign, (1, reps))


def _dft_factor_tables(n):
    nk1 = n // DFT_K0
    j = jnp.arange(n, dtype=jnp.int32)
    k1 = jnp.arange(nk1, dtype=jnp.int32) * DFT_K0
    k0 = jnp.arange(DFT_K0, dtype=jnp.int32)
    w = 2.0 * math.pi / n
    ang_a = ((k1[:, None] * j[None, :]) % n).astype(F32) * w
    ang_b = ((k0[:, None] * j[None, :]) % n).astype(F32) * w
    return jnp.cos(ang_a), jnp.sin(ang_a), jnp.cos(ang_b), jnp.sin(ang_b)


def _dft_half_tables(n):
    nh = n // 2
    nk1 = -(-(nh // DFT_K0 + 1) // SUBLANES) * SUBLANES
    j = jnp.arange(nh, dtype=jnp.int32)
    k1 = jnp.arange(nk1, dtype=jnp.int32) * DFT_K0
    k0 = jnp.arange(DFT_K0, dtype=jnp.int32)
    w = 2.0 * math.pi / n
    ang_a = ((k1[:, None] * j[None, :]) % n).astype(F32) * w
    ang_b = ((k0[:, None] * j[None, :]) % n).astype(F32) * w
    return jnp.cos(ang_a), jnp.sin(ang_a), jnp.cos(ang_b), jnp.sin(ang_b)


def _dft_tables(n):
    ca, sa, cb, sb = _dft_factor_tables(n)
    cn = (ca[:, None, :] * cb[None] - sa[:, None, :] * sb[None]).reshape(n, n)
    sn = (sa[:, None, :] * cb[None] + ca[:, None, :] * sb[None]).reshape(n, n)
    return cn.astype(BF16), sn.astype(BF16)


def _blockdiag(blocks):
    g, c = blocks.shape[-3], blocks.shape[-1]
    eye = jnp.eye(g, dtype=blocks.dtype)
    out = jnp.einsum('...gce,gh->...gche', blocks, eye)
    return out.reshape(blocks.shape[:-3] + (g * c, g * c))


def _channel_dft_blockdiag():
    e = jnp.arange(FOURIER_CH, dtype=jnp.int32)
    ang = ((e[:, None] * e[None, :]) % FOURIER_CH).astype(F32) * (2.0 * math.pi / FOURIER_CH)
    reps = (FOURIER_GROUPS, 1, 1)
    return _blockdiag(jnp.tile(jnp.cos(ang)[None], reps)), _blockdiag(jnp.tile(-jnp.sin(ang)[None], reps))


def kernel(x, c, ctx, c_ctx, w_ada, b_ada, norm_g, w_ffn_in, w_ffn_out, w_in, w_out, w_fourier, attn_sink,
           conv_qk, b_gate_i, b_gate_f, g_final):
    b, n, d = x.shape
    cl = ctx.shape[1]
    depth = w_ada.shape[0]
    nh = 2 * MLSTM_HEADS

    rows = -(-(b + 1) // SUBLANES) * SUBLANES
    c_all = jnp.zeros((rows, d), F32).at[:b].set(c).at[b].set(c_ctx)
    mod = _modulation(c_all, w_ada, b_ada).reshape(depth, rows, N_MOD, d)
    x_row = lambda bi: bi
    c_row = lambda bi: b

    offs = [0]
    for s in (D_FOURIER, D_ATT, D_KV, D_KV, D_MLSTM, D_MLSTM, D_MLSTM, D_MLSTM, nh, nh):
        offs.append(offs[-1] + s)
    assert offs[1] == D_FOURIER and offs[8] - offs[1] == C_END - C_Q
    w_gate = w_in[:, :, offs[8]:offs[10]]
    cc_bd, sc_bd = _channel_dft_blockdiag()
    w_p, w_p_ctx, w_rest = _fourier_fold(w_in, _blockdiag(w_fourier), cc_bd, sc_bd,
                                         1.0 / math.sqrt(n * FOURIER_CH), 1.0 / math.sqrt(cl * FOURIER_CH))
    wg_t = jnp.swapaxes(w_gate, 1, 2).astype(BF16)
    gb_row = jnp.concatenate([b_gate_i.reshape(depth, nh), b_gate_f.reshape(depth, nh)], axis=-1)[:, :, None]
    wf = (w_ffn_in[0, 0].astype(BF16), w_ffn_out[0, 0].astype(BF16))
    wo = w_out.astype(BF16)

    cos128, sin128 = _rope_tables(n)
    half_spectrum = n % (2 * TK_DFT) == 0 and TK_DFT == TM
    dft_x = _dft_half_tables(n) if half_spectrum else _dft_tables(n)
    cnc, snc = _dft_tables(cl)
    zeros_c = jnp.zeros((cl, LANES), F32)
    state0 = (jnp.zeros((b, 2, D_MLSTM, D_MLSTM), F32), jnp.zeros((b, 2, nh, D_MLSTM), F32),
              jnp.zeros((b, nh, LANES), F32))

    xc = ctx
    for l in range(depth):
        last = l == depth - 1
        x, *wf_next = _ffn(x, l, 0, mod, x_row, norm_g, *wf, cast_next=(w_ffn_in, w_ffn_out, l, 1))
        xc = _ffn(xc, l, 0, mod, c_row, norm_g, *wf)
        wf = wf_next

        px, qtx, kx, vtx, mkx, mqtx, mvtx, ogx, grx = _inproj(
            x, l, mod, x_row, norm_g, w_p, w_rest, wg_t, gb_row, conv_qk, cos128, sin128, True)
        pc, qtc, kc, vtc, mkc, mqtc, mvtc, ogc, grc = _inproj(
            xc, l, mod, c_row, norm_g, w_p_ctx, w_rest, wg_t, gb_row, conv_qk, zeros_c, zeros_c, False)

        hcf, hcb, *state_c = _mlstm(mkc, mqtc, mvtc, grc, state0)
        hxf, hxb, *_ = _mlstm(mkx, mqtx, mvtx, grx, state_c)
        att_x = _attention(attn_sink, l, qtx, kx, vtx, kc, vtc, True)
        yf_x = tuple(_dft_half(dft_x, px)) if half_spectrum else _dft(*dft_x, px)
        mix_x = (yf_x, att_x, hxf, hxb, ogx)
        if last:
            x = _ffn(x, l, 2, mod, x_row, norm_g, *wf, mixer=mix_x, w_o=wo, g_final=g_final)
        else:
            x, *wf_next = _ffn(x, l, 2, mod, x_row, norm_g, *wf, mixer=mix_x, w_o=wo,
                               cast_next=(w_ffn_in, w_ffn_out, l + 1, 0))
            att_c = _attention(attn_sink, l, qtc, None, None, kc, vtc, False)
            yf_c = _dft(cnc, snc, pc)
            xc = _ffn(xc, l, 2, mod, c_row, norm_g, *wf, mixer=(yf_c, att_c, hcf, hcb, ogc), w_o=wo)
            wf = wf_next
    return x
```

```python
import functools
import math

import jax
import jax.numpy as jnp
from jax import lax
from jax.experimental import pallas as pl
from jax.experimental.pallas import tpu as pltpu

F32 = jnp.float32
BF16 = jnp.bfloat16
HI = lax.Precision.HIGHEST

GRID_W = 64
EPS = 1e-6
NEG = -1e30
LOG2E = math.log2(math.e)
FOURIER_GROUPS = 4
FOURIER_CH = 64
D_FOURIER = FOURIER_GROUPS * FOURIER_CH
HEAD_DIM = 64
ATT_HEADS = 8
ATT_KV_HEADS = 2
D_ATT = ATT_HEADS * HEAD_DIM
D_KV = ATT_KV_HEADS * HEAD_DIM
ATT_BLOCK = 128
WINDOW = 128
ROPE_BASE = 10000.0
MLSTM_HEADS = 4
MLSTM_DIM = 64
D_MLSTM = MLSTM_HEADS * MLSTM_DIM
N_MOD = 9
N_GATES = 4 * MLSTM_HEADS

LANES = 128
SUBLANES = 8
VMEM_LIMIT = 56 * 1024 * 1024

TM = 512
TM_INPROJ = 1024
TM_FFN = 1024
FFN_PART_ROWS = 256
TK_DFT = 512
DFT_K0 = 64
DFT_TAIL = 16
TQ_ATT = 1024
ATT_STAGE_ORDER = (0, 1, 2, 3)
ATT_SKEW = (1, 2, 3)
MLSTM_L = 128
MLSTM_CHUNKS_PER_STEP = 1

C_P = 0
C_Q = C_P + 2 * D_FOURIER
C_K = C_Q + D_ATT
C_V = C_K + D_KV
C_MQ = C_V + D_KV
C_MK = C_MQ + D_MLSTM
C_MV = C_MK + D_MLSTM
C_MO = C_MV + D_MLSTM
C_END = C_MO + D_MLSTM


def _cparams(sem):
    return pltpu.CompilerParams(dimension_semantics=sem, vmem_limit_bytes=VMEM_LIMIT)


def _dot(a, b):
    return jnp.dot(a, b, preferred_element_type=F32)


def _dot_nt(a, b):
    return lax.dot_general(a, b, (((1,), (1,)), ((), ())), preferred_element_type=F32)


def _dot_hi(a, b):
    return jnp.dot(a, b, preferred_element_type=F32, precision=HI)


def _dot_split3(x, onehot):
    rows = x.shape[0]
    hi = x.astype(BF16)
    r1 = x - hi.astype(F32)
    mid = r1.astype(BF16)
    lo = (r1 - mid.astype(F32)).astype(BF16)
    y = _dot(jnp.concatenate([hi, mid, lo], axis=0), onehot)
    return y[0:rows] + y[rows:2 * rows] + y[2 * rows:3 * rows]


def _rms_mod(xt, g, shift, scale):
    ms = jnp.mean(xt * xt, axis=-1, keepdims=True)
    y = xt * lax.rsqrt(ms + EPS) * g
    return y * (1.0 + scale) + shift


def _silu(t):
    return t * jax.nn.sigmoid(t)


def _log_sigmoid(t):
    return jnp.minimum(t, 0.0) - jnp.log1p(jnp.exp(-jnp.abs(t)))


def _mod_kernel(c_ref, w_ref, b_ref, o_ref):
    sc = _silu(c_ref[...])
    o_ref[...] = _dot(sc.astype(BF16), w_ref[...].astype(BF16)) + b_ref[...]


def _modulation(c_all, w_ada, b_ada):
    depth, d, nd = w_ada.shape
    r = c_all.shape[0]
    tn = 1024
    return pl.pallas_call(
        _mod_kernel,
        grid=(depth, nd // tn),
        in_specs=[pl.BlockSpec((r, d), lambda l, j: (0, 0)),
                  pl.BlockSpec((None, d, tn), lambda l, j: (l, 0, j)),
                  pl.BlockSpec((None, 1, tn), lambda l, j: (l, 0, j))],
        out_specs=pl.BlockSpec((None, r, tn), lambda l, j: (l, 0, j)),
        out_shape=jax.ShapeDtypeStruct((depth, r, nd), F32),
        compiler_params=_cparams(("parallel", "parallel")),
        name="modulation",
    )(c_all, w_ada, b_ada.reshape(depth, 1, nd))


def _fold_kernel(win_ref, wf_ref, cc_ref, sc_ref, o_ref, oc_ref, rest_ref, *, scale, scale_ctx):
    rest_ref[...] = win_ref[:, D_FOURIER:D_FOURIER + C_END - C_Q].astype(BF16)
    wf = wf_ref[...]
    mc = _dot_hi(cc_ref[...], wf)
    ms = _dot_hi(sc_ref[...], wf)
    wa = win_ref[:, :D_FOURIER]
    wc = _dot_hi(wa, mc)
    ws = _dot_hi(wa, ms)
    o_ref[:, :D_FOURIER] = (wc * scale).astype(BF16)
    o_ref[:, D_FOURIER:] = (ws * scale).astype(BF16)
    oc_ref[:, :D_FOURIER] = (wc * scale_ctx).astype(BF16)
    oc_ref[:, D_FOURIER:] = (ws * scale_ctx).astype(BF16)


def _fourier_fold(w_in, wf_bd, cc_bd, sc_bd, scale, scale_ctx):
    depth, d, d_in = w_in.shape
    out = pl.BlockSpec((None, d, 2 * D_FOURIER), lambda l: (l, 0, 0))
    n_rest = C_END - C_Q
    return pl.pallas_call(
        functools.partial(_fold_kernel, scale=scale, scale_ctx=scale_ctx),
        grid=(depth,),
        in_specs=[pl.BlockSpec((None, d, d_in), lambda l: (l, 0, 0)),
                  pl.BlockSpec((None, D_FOURIER, D_FOURIER), lambda l: (l, 0, 0)),
                  pl.BlockSpec((D_FOURIER, D_FOURIER), lambda l: (0, 0)),
                  pl.BlockSpec((D_FOURIER, D_FOURIER), lambda l: (0, 0))],
        out_specs=[out, out, pl.BlockSpec((None, d, n_rest), lambda l: (l, 0, 0))],
        out_shape=[jax.ShapeDtypeStruct((depth, d, 2 * D_FOURIER), BF16)] * 2
        + [jax.ShapeDtypeStruct((depth, d, n_rest), BF16)],
        compiler_params=_cparams(("parallel",)),
        name="fourier_fold",
    )(w_in, wf_bd, cc_bd, sc_bd)


def _ffn_kernel(*refs, sub, d_ff, mix, split_yf, final, cast_next, split):
    x_ref, mod_ref, g_ref, win_ref, wout_ref = refs[:5]
    if cast_next:
        o_ref, nin_o_ref, nout_o_ref = refs[-3:]
        nin_ref, nout_ref = refs[-5:-3]
        nin_o_ref[...] = nin_ref[...].astype(BF16)
        nout_o_ref[...] = nout_ref[...].astype(BF16)
        rest = list(refs[5:-5])
    else:
        o_ref = refs[-1]
        rest = list(refs[5:-1])
    shift = mod_ref[3 * sub:3 * sub + 1, :]
    scale = mod_ref[3 * sub + 1:3 * sub + 2, :]
    gate = mod_ref[3 * sub + 2:3 * sub + 3, :]
    tm = x_ref.shape[0]
    rs = tm // split
    def stages(part):
        rows = slice(part * rs, (part + 1) * rs)
        v = {}

        def s_norm():
            xt = x_ref[rows, :]
            if mix:
                if split_yf:
                    yt_ref, yb_ref, att_ref, hf_ref, hb_ref, og_ref, wo_ref = rest[:7]
                    in_top = pl.program_id(1) < pl.num_programs(1) // 2
                    yf = jnp.where(in_top, yt_ref[rows, :], yb_ref[rows, :])
                else:
                    yf_ref, att_ref, hf_ref, hb_ref, og_ref, wo_ref = rest[:6]
                    yf = yf_ref[rows, :]
                mh = (og_ref[rows, :].astype(F32) * (hf_ref[rows, :] + hb_ref[rows, :])).astype(BF16)
                a0, a1 = D_FOURIER, D_FOURIER + D_ATT
                y = (_dot(yf, wo_ref[:a0, :]) + _dot(att_ref[rows, :], wo_ref[a0:a1, :])
                     + _dot(mh, wo_ref[a1:, :]))
                xt = xt + mod_ref[5:6, :] * y
            v['xt'] = xt
            v['h'] = _rms_mod(xt, g_ref[sub:sub + 1, :], shift, scale).astype(BF16)

        def s_up():
            v['gu'] = _dot(v.pop('h'), win_ref[...])

        def s_act():
            gu = v.pop('gu')
            v['act'] = (_silu(gu[:, :d_ff]) * gu[:, d_ff:]).astype(BF16)

        def s_down():
            y = v.pop('xt') + (0.5 * gate) * _dot(v.pop('act'), wout_ref[...])
            if final:
                gf_ref = rest[-1]
                ms = jnp.mean(y * y, axis=-1, keepdims=True)
                y = y * lax.rsqrt(ms + EPS) * gf_ref[...]
            o_ref[rows, :] = y

        return s_norm, s_up, s_act, s_down

    parts = [stages(part) for part in range(split)]
    for step in range(split + 3):
        for stage in range(4):
            part = step - stage
            if 0 <= part < split:
                parts[part][stage]()


def _slab_count(rows, steps):
    for nblk in range(steps, 0, -1):
        if rows % nblk == 0 and (rows // nblk) % (2 * SUBLANES) == 0:
            return nblk
    return 1


def _ffn(x, layer, sub, mod, mod_row, norm_g, w_in, w_out, mixer=None, w_o=None, g_final=None, cast_next=None):
    b, n, d = x.shape
    d_ff = w_out.shape[0]
    mix = mixer is not None
    tm = min(TM if mix else TM_FFN, n)
    nt = n // tm
    split = max(tm // FFN_PART_ROWS, 1)
    final = g_final is not None
    tok = lambda w: pl.BlockSpec((None, tm, w), lambda bi, t: (bi, t, 0))
    whole = lambda r, c: pl.BlockSpec((r, c), lambda bi, t: (0, 0), pipeline_mode=pl.Buffered(1))
    in_specs = [tok(d),
                pl.BlockSpec((None, None, N_MOD, d), lambda bi, t: (layer, mod_row(bi), 0, 0)),
                pl.BlockSpec((None, 3, d), lambda bi, t: (layer, 0, 0)),
                whole(d, 2 * d_ff), whole(d_ff, d)]
    args = [x, mod, norm_g, w_in, w_out]
    split_yf = mix and isinstance(mixer[0], tuple)
    if mix:
        if split_yf:
            hh = nt // 2
            in_specs += [pl.BlockSpec((None, tm, D_FOURIER), lambda bi, t: (bi, jnp.minimum(t, hh - 1), 0)),
                         pl.BlockSpec((None, tm, D_FOURIER), lambda bi, t: (bi, jnp.maximum(t - hh, 0), 0))]
            args += list(mixer[0])
        else:
            in_specs.append(tok(D_FOURIER))
            args.append(mixer[0])
        in_specs += [tok(D_ATT), tok(D_MLSTM), tok(D_MLSTM), tok(D_MLSTM),
                     pl.BlockSpec((None, d, d), lambda bi, t: (layer, 0, 0), pipeline_mode=pl.Buffered(1))]
        args += list(mixer[1:]) + [w_o]
    if final:
        in_specs.append(pl.BlockSpec((1, d), lambda bi, t: (0, 0)))
        args.append(g_final.reshape(1, d))
    out_specs = [tok(d)]
    out_shape = [jax.ShapeDtypeStruct((b, n, d), F32)]
    if cast_next is not None:
        nin, nout, nl, nh_ = cast_next
        steps = b * nt
        for arr, rows, cols in ((nin, d, 2 * d_ff), (nout, d_ff, d)):
            nblk = _slab_count(rows, steps)
            slab = lambda bi, t, nblk=nblk: jnp.minimum(bi * nt + t, nblk - 1)
            in_specs.append(pl.BlockSpec((None, None, rows // nblk, cols),
                                         lambda bi, t, slab=slab: (nl, nh_, slab(bi, t), 0)))
            out_specs.append(pl.BlockSpec((rows // nblk, cols), lambda bi, t, slab=slab: (slab(bi, t), 0)))
            out_shape.append(jax.ShapeDtypeStruct((rows, cols), BF16))
            args.append(arr)
    res = pl.pallas_call(
        functools.partial(_ffn_kernel, sub=sub, d_ff=d_ff, mix=mix, split_yf=split_yf, final=final, split=split,
                          cast_next=cast_next is not None),
        grid=(b, nt),
        in_specs=in_specs,
        out_specs=out_specs,
        out_shape=out_shape,
        compiler_params=_cparams(("arbitrary", "arbitrary")),
        name="ffn_mix" if mix else "ffn",
    )(*args)
    return res if cast_next is not None else res[0]


def _store_chunks(ref, t):
    for c in range(ref.shape[0]):
        ref[c] = t[:, c * MLSTM_L:(c + 1) * MLSTM_L]


def _rope128(t, cos, sin_signed, low16):
    fwd = pltpu.roll(t, LANES - 16, axis=1)
    bwd = pltpu.roll(t, 16, axis=1)
    return t * cos + jnp.where(low16, fwd, bwd) * sin_signed


def _inproj_kernel(x_ref, xp_ref, xn_ref, mod_ref, g_ref, wp_ref, w_ref, wgt_ref, gbr_ref, cw_ref,
                   cos_ref, sin_ref,
                   p_ref, qt_ref, k_ref, vt_ref, mk_ref, mqt_ref, mvt_ref, og_ref, gr_ref, *, rope):
    t = pl.program_id(1)
    nt = pl.num_programs(1)
    tm = x_ref.shape[0]
    shift = mod_ref[3:4, :]
    scale = mod_ref[4:5, :]
    g = g_ref[1:2, :]
    h = _rms_mod(x_ref[...], g, shift, scale).astype(BF16)
    seg = lambda c0, c1: _dot(h, w_ref[:, c0 - C_Q:c1 - C_Q])
    lane = lax.broadcasted_iota(jnp.int32, (1, LANES), 1)
    low16 = (lane & 16) == 0
    qscale = HEAD_DIM ** -0.5 * LOG2E
    if rope:
        cos = cos_ref[...]
        sin = sin_ref[...]

    up = _dot(h, wp_ref[...])
    uq = seg(C_Q, C_K)
    p_ref[...] = up.astype(BF16)

    ukv = seg(C_K, C_MQ)
    for cb in range(D_ATT // LANES):
        qb = uq[:, cb * LANES:(cb + 1) * LANES]
        if rope:
            qb = _rope128(qb, cos, sin, low16)
        qt_ref[cb * LANES:(cb + 1) * LANES, :] = jnp.transpose(qb * qscale).astype(BF16)

    uqk = seg(C_MQ, C_MV)
    xh = jnp.concatenate([xp_ref[...], xn_ref[...]], axis=0)
    hh = _rms_mod(xh, g, shift, scale).astype(BF16)
    uh = _dot(hh, w_ref[:, C_MQ - C_Q:C_MV - C_Q])
    kb = ukv[:, :D_KV]
    if rope:
        kb = _rope128(kb, cos, sin, low16)
    k_ref[...] = kb.astype(BF16)
    vt_ref[...] = jnp.transpose(ukv[:, D_KV:]).astype(BF16)

    uvo = seg(C_MV, C_END)
    prev_row = jnp.where(t > 0, uh[SUBLANES - 1:SUBLANES, :], 0.0)
    next_row = jnp.where(t < nt - 1, uh[SUBLANES:SUBLANES + 1, :], 0.0)
    row = lax.broadcasted_iota(jnp.int32, (tm, 1), 0)
    u_prev = jnp.where(row == 0, prev_row, pltpu.roll(uqk, 1, axis=0))
    u_next = jnp.where(row == tm - 1, next_row, pltpu.roll(uqk, tm - 1, axis=0))
    cv = u_prev * cw_ref[0:1, :] + uqk * cw_ref[1:2, :] + u_next * cw_ref[2:3, :]
    qk = _silu(cv)
    _store_chunks(mqt_ref, jnp.transpose(qk[:, :D_MLSTM]).astype(BF16))
    mk_ref[...] = (qk[:, D_MLSTM:] * (MLSTM_DIM ** -0.5)).astype(BF16)

    half = N_GATES // 2
    zr = _dot_nt(wgt_ref[...], h) + gbr_ref[...]
    _store_chunks(mvt_ref, jnp.transpose(uvo[:, :D_MLSTM]).astype(BF16))
    og_ref[...] = jax.nn.sigmoid(uvo[:, D_MLSTM:]).astype(BF16)
    rowg = lax.broadcasted_iota(jnp.int32, (N_GATES, 1), 0)
    gr_ref[...] = jnp.where(rowg >= half, _log_sigmoid(zr), zr)


def _inproj(x, layer, mod, mod_row, norm_g, w_p, w_rest, wg_t, gb_row, conv_w, cos128, sin128, rope):
    b, n, d = x.shape
    tm = min(TM_INPROJ, n)
    nt = n // tm
    r8 = tm // SUBLANES
    n8 = n // SUBLANES
    lay = lambda *blk: pl.BlockSpec((None,) + blk, lambda bi, t: (layer,) + (0,) * len(blk))
    tok = lambda w: pl.BlockSpec((None, tm, w), lambda bi, t: (bi, t, 0))
    trn = lambda rows: pl.BlockSpec((None, rows, tm), lambda bi, t: (bi, 0, t))
    in_specs = [tok(d),
                pl.BlockSpec((None, SUBLANES, d), lambda bi, t: (bi, jnp.maximum(t * r8 - 1, 0), 0)),
                pl.BlockSpec((None, SUBLANES, d), lambda bi, t: (bi, jnp.minimum((t + 1) * r8, n8 - 1), 0)),
                pl.BlockSpec((None, None, N_MOD, d), lambda bi, t: (layer, mod_row(bi), 0, 0)),
                lay(3, d), lay(d, C_Q), lay(d, C_END - C_Q), lay(N_GATES, d), lay(N_GATES, 1), lay(3, 2 * D_MLSTM),
                pl.BlockSpec((tm, LANES), lambda bi, t: (t, 0)),
                pl.BlockSpec((tm, LANES), lambda bi, t: (t, 0))]
    chk = pl.BlockSpec((None, tm // MLSTM_L, D_MLSTM, MLSTM_L), lambda bi, t: (bi, t, 0, 0))
    chk_shape = jax.ShapeDtypeStruct((b, n // MLSTM_L, D_MLSTM, MLSTM_L), BF16)
    out_specs = [tok(2 * D_FOURIER), trn(D_ATT), tok(D_KV), trn(D_KV),
                 tok(D_MLSTM), chk, chk, tok(D_MLSTM), trn(N_GATES)]
    shp = lambda w, dt: jax.ShapeDtypeStruct((b, n, w), dt)
    shpt = lambda rows, dt: jax.ShapeDtypeStruct((b, rows, n), dt)
    out_shape = [shp(2 * D_FOURIER, BF16), shpt(D_ATT, BF16), shp(D_KV, BF16), shpt(D_KV, BF16),
                 shp(D_MLSTM, BF16), chk_shape, chk_shape, shp(D_MLSTM, BF16),
                 shpt(N_GATES, F32)]
    return pl.pallas_call(
        functools.partial(_inproj_kernel, rope=rope),
        grid=(b, nt),
        in_specs=in_specs,
        out_specs=out_specs,
        out_shape=out_shape,
        compiler_params=_cparams(("parallel", "parallel")),
        name="inproj",
    )(x, x, x, mod, norm_g, w_p, w_rest, wg_t, gb_row, conv_w, cos128, sin128)


def _dft_kernel(cn_ref, sn_ref, p_ref, y_ref):
    y = _dot(cn_ref[...], p_ref[:, :D_FOURIER]) + _dot(sn_ref[...], p_ref[:, D_FOURIER:])
    y_ref[...] = y.astype(BF16)


def _dft(cn, sn, p):
    b, n, _ = p.shape
    tk = min(TK_DFT, n)
    return pl.pallas_call(
        _dft_kernel,
        grid=(n // tk, b),
        in_specs=[pl.BlockSpec((tk, n), lambda i, bi: (i, 0)),
                  pl.BlockSpec((tk, n), lambda i, bi: (i, 0)),
                  pl.BlockSpec((None, n, 2 * D_FOURIER), lambda i, bi: (bi, 0, 0))],
        out_specs=pl.BlockSpec((None, tk, D_FOURIER), lambda i, bi: (bi, i, 0)),
        out_shape=jax.ShapeDtypeStruct((b, n, D_FOURIER), BF16),
        compiler_params=_cparams(("parallel", "parallel")),
        name="dft",
    )(cn, sn, p)


def _anti_identity(rows, cols, offset):
    r = lax.broadcasted_iota(jnp.int32, (rows, cols), 0)
    c = lax.broadcasted_iota(jnp.int32, (rows, cols), 1)
    return jnp.where(r + c == offset, 1.0, 0.0).astype(BF16)


def _dft_fold(p_ref, pf_ref, mid_ref):
    n = p_ref.shape[0]
    nh = n // 2
    tk = TK_DFT
    flip = _anti_identity(tk, tk, tk - 1)
    nblk = nh // tk
    rev = jnp.concatenate([_dot(flip, p_ref[n - (i + 1) * tk:n - i * tk, :]) for i in range(nblk)], axis=0)
    row = lax.broadcasted_iota(jnp.int32, (nh, 1), 0)
    mirror = jnp.where(row == 0, 0.0, pltpu.roll(rev, 1, axis=0))
    top = p_ref[0:nh, :].astype(F32)
    pf_ref[:, :D_FOURIER] = (top[:, :D_FOURIER] + mirror[:, :D_FOURIER]).astype(BF16)
    pf_ref[:, D_FOURIER:] = (top[:, D_FOURIER:] - mirror[:, D_FOURIER:]).astype(BF16)
    mid_ref[...] = jnp.broadcast_to(p_ref[nh:nh + 1, :D_FOURIER].astype(F32), mid_ref.shape)


def _dft_half_kernel(ca_ref, sa_ref, cb_ref, sb_ref, p_ref, top_ref, bot_ref, cn_scr, sn_scr, pf_scr, mid_scr):
    i = pl.program_id(0)
    bi = pl.program_id(1)
    tk = top_ref.shape[0]
    groups = tk // DFT_K0

    @pl.when(i == 0)
    def _():
        _dft_fold(p_ref, pf_scr.at[bi], mid_scr.at[bi])

    pf_ref = pf_scr.at[bi]
    mid_ref = mid_scr.at[bi]

    @pl.when(bi == 0)
    def _():
        cb = cb_ref[...]
        sb = sb_ref[...]
        for r in range(groups + 1):
            nr = DFT_K0 if r < groups else DFT_TAIL
            ca = ca_ref[pl.ds(i * groups + r, 1), :]
            sa = sa_ref[pl.ds(i * groups + r, 1), :]
            cn_scr[r * DFT_K0:r * DFT_K0 + nr, :] = (ca * cb[:nr] - sa * sb[:nr]).astype(BF16)
            sn_scr[r * DFT_K0:r * DFT_K0 + nr, :] = (sa * cb[:nr] + ca * sb[:nr]).astype(BF16)

    e = _dot(cn_scr[...], pf_ref[:, :D_FOURIER])
    o = _dot(sn_scr[...], pf_ref[:, D_FOURIER:])
    rows = lax.broadcasted_iota(jnp.int32, (tk + DFT_TAIL, 1), 0)
    mid = mid_ref[0:1, :]
    e = e + jnp.where((rows & 1) == 0, mid, -mid)
    top_ref[...] = (e + o)[:tk].astype(BF16)
    bot_ref[...] = _dot(_anti_identity(tk, tk + DFT_TAIL, tk), (e - o).astype(BF16)).astype(BF16)


def _dft_half(tables, p):
    ca, sa, cb, sb = tables
    b, n, _ = p.shape
    nh = n // 2
    tk = TK_DFT
    nt = nh // tk
    whole = lambda a: pl.BlockSpec(a.shape, lambda i, bi: (0, 0))
    return pl.pallas_call(
        _dft_half_kernel,
        grid=(nt, b),
        in_specs=[whole(ca), whole(sa), whole(cb), whole(sb),
                  pl.BlockSpec((None, n, 2 * D_FOURIER), lambda i, bi: (jnp.where(i == 0, bi, b - 1), 0, 0))],
        out_specs=[pl.BlockSpec((None, tk, D_FOURIER), lambda i, bi: (bi, i, 0)),
                   pl.BlockSpec((None, tk, D_FOURIER), lambda i, bi: (bi, nt - 1 - i, 0))],
        out_shape=[jax.ShapeDtypeStruct((b, nh, D_FOURIER), BF16), jax.ShapeDtypeStruct((b, nh, D_FOURIER), BF16)],
        scratch_shapes=[pltpu.VMEM((tk + DFT_TAIL, nh), BF16), pltpu.VMEM((tk + DFT_TAIL, nh), BF16),
                        pltpu.VMEM((b, nh, 2 * D_FOURIER), BF16), pltpu.VMEM((b, SUBLANES, D_FOURIER), F32)],
        compiler_params=_cparams(("arbitrary", "arbitrary")),
        name="dft_half",
    )(ca, sa, cb, sb, p)


def _attn_kernel(sink_ref, qt_ref, *rest, layer, local):
    if local:
        kp_ref, kc_ref, kn_ref, vp_ref, vc_ref, vn_ref, kx_ref, vxt_ref, o_ref = rest
    else:
        kx_ref, vxt_ref, o_ref = rest
    blk = ATT_BLOCK
    group = ATT_HEADS // ATT_KV_HEADS
    gw = group * blk
    cl = kx_ref.shape[0]
    nq = qt_ref.shape[1] // blk
    lane_head = lax.broadcasted_iota(jnp.int32, (1, gw), 1) // blk
    sinks = []
    for kv in range(ATT_KV_HEADS):
        sink = jnp.zeros((1, gw), F32)
        for i in range(group):
            sink = jnp.where(lane_head == i, sink_ref[layer, kv * group + i] * LOG2E, sink)
        sinks.append(sink)
    if local:
        j = pl.program_id(1)
        nb = pl.num_programs(1) * nq
        ks = lax.broadcasted_iota(jnp.int32, (blk, blk), 0)
        qi = lax.broadcasted_iota(jnp.int32, (blk, blk), 1)
        in_p = jnp.abs(ks - blk - qi) <= WINDOW
        in_n = jnp.abs(ks + blk - qi) <= WINDOW

    def key_blocks(i):
        pick = lambda b: kp_ref[...] if b == 0 else kn_ref[...] if b == nq + 1 else kc_ref[(b - 1) * blk:b * blk, :]
        return [kx_ref[...]] + [pick(b) for b in range(i, i + 3)]

    def value_blocks(i, dims):
        pick = lambda b: (vp_ref[dims, :] if b == 0 else vn_ref[dims, :] if b == nq + 1
                          else vc_ref[dims, (b - 1) * blk:b * blk])
        return [vxt_ref[dims, :]] + [pick(b) for b in range(i, i + 3)]

    zq = jnp.zeros((HEAD_DIM, gw), BF16)

    def scores(i, kv):
        qt_blk = qt_ref[:, i * blk:(i + 1) * blk]
        r0 = kv * group * HEAD_DIM
        qblk = jnp.concatenate([qt_blk[r0 + h * HEAD_DIM:r0 + (h + 1) * HEAD_DIM, :] for h in range(group)], axis=1)
        qt_g = jnp.concatenate([qblk, zq] if kv == 0 else [zq, qblk], axis=0)
        if not local:
            return [_dot(kx_ref[...], qt_g)]
        jb = j * nq + i
        bias_p = jnp.where(in_p & (jb > 0), 0.0, NEG)
        bias_n = jnp.where(in_n & (jb < nb - 1), 0.0, NEG)
        s = _dot(jnp.concatenate(key_blocks(i), axis=0), qt_g)
        add = lambda sg, bias: jnp.concatenate([sg[:, h * blk:(h + 1) * blk] + bias for h in range(group)], axis=1)
        return [s[:cl], add(s[cl:cl + blk], bias_p), s[cl + blk:cl + 2 * blk], add(s[cl + 2 * blk:], bias_n)]

    def colmax(segs, kv):
        m = sinks[kv]
        for sg in segs:
            m = jnp.maximum(m, jnp.max(sg, axis=0, keepdims=True))
        return m

    def softmax(segs, kv, m):
        den = jnp.exp2(sinks[kv] - m)
        ps = []
        for sg in segs:
            p = jnp.exp2(sg - m)
            den = den + jnp.sum(p, axis=0, keepdims=True)
            ps.append(p.astype(BF16))
        return (jnp.concatenate(ps, axis=0) if len(ps) > 1 else ps[0]), den

    def values(i, kv, p_all, den):
        dims = slice(kv * HEAD_DIM, (kv + 1) * HEAD_DIM)
        vt_g = jnp.concatenate(value_blocks(i, dims), axis=1) if local else vxt_ref[dims, :]
        ot = _dot(vt_g, p_all) * (1.0 / den)
        for cb in range(group // 2):
            pair = jnp.concatenate([ot[:, (2 * cb) * blk:(2 * cb + 1) * blk],
                                    ot[:, (2 * cb + 1) * blk:(2 * cb + 2) * blk]], axis=0)
            c0 = (kv * (group // 2) + cb) * LANES
            o_ref[i * blk:(i + 1) * blk, c0:c0 + LANES] = jnp.transpose(pair).astype(BF16)

    chains = [(i, kv) for i in range(nq) for kv in range(ATT_KV_HEADS)]
    segs_q, max_q, soft_q = {}, {}, {}
    lag_max, lag_soft, lag_val = ATT_SKEW
    def run_scores(c):
        segs_q[c] = scores(*chains[c])

    def run_max(c):
        max_q[c] = colmax(segs_q[c], chains[c][1])

    def run_soft(c):
        soft_q[c] = softmax(segs_q.pop(c), chains[c][1], max_q.pop(c))

    def run_values(c):
        values(*chains[c], *soft_q.pop(c))

    stages = ((0, run_scores), (lag_max, run_max), (lag_soft, run_soft), (lag_val, run_values))
    for step in range(len(chains) + lag_val):
        for idx in ATT_STAGE_ORDER:
            lag, fn = stages[idx]
            if 0 <= step - lag < len(chains):
                fn(step - lag)


def _attention(sink, layer, qt, k, vt, kx, vxt, local):
    b, _, n = qt.shape
    cl = kx.shape[1]
    blk = ATT_BLOCK
    tq = min(TQ_ATT, n)
    nq = tq // blk
    nb = n // blk
    in_specs = [pl.BlockSpec(memory_space=pltpu.SMEM),
                pl.BlockSpec((None, D_ATT, tq), lambda bi, j: (bi, 0, j))]
    args = [sink, qt]
    if local:
        pj = lambda j: jnp.maximum(j * nq - 1, 0)
        nj = lambda j: jnp.minimum((j + 1) * nq, nb - 1)
        in_specs += [pl.BlockSpec((None, blk, D_KV), lambda bi, j: (bi, pj(j), 0)),
                     pl.BlockSpec((None, tq, D_KV), lambda bi, j: (bi, j, 0)),
                     pl.BlockSpec((None, blk, D_KV), lambda bi, j: (bi, nj(j), 0)),
                     pl.BlockSpec((None, D_KV, blk), lambda bi, j: (bi, 0, pj(j))),
                     pl.BlockSpec((None, D_KV, tq), lambda bi, j: (bi, 0, j)),
                     pl.BlockSpec((None, D_KV, blk), lambda bi, j: (bi, 0, nj(j)))]
        args += [k, k, k, vt, vt, vt]
    in_specs += [pl.BlockSpec((None, cl, D_KV), lambda bi, j: (bi, 0, 0)),
                 pl.BlockSpec((None, D_KV, cl), lambda bi, j: (bi, 0, 0))]
    args += [kx, vxt]
    return pl.pallas_call(
        functools.partial(_attn_kernel, layer=layer, local=local),
        grid=(b, n // tq),
        in_specs=in_specs,
        out_specs=pl.BlockSpec((None, tq, D_ATT), lambda bi, j: (bi, j, 0)),
        out_shape=jax.ShapeDtypeStruct((b, n, D_ATT), BF16),
        compiler_params=_cparams(("parallel", "parallel")),
        name="attention",
    )(*args)


def _mlstm_stages(d, k_ref, qt_ref, vt_ref, gr_ref, h_ref, c_scr, n_scr, m_scr, consts):
    L = MLSTM_L
    H = MLSTM_HEADS
    DH = MLSTM_DIM
    tri, valid_t, lane_h, row_h, blockdiag, nmask = consts
    r0 = d * H
    v = {}

    def stage_in():
        k = k_ref[...]
        qt = qt_ref[...]
        gr = gr_ref[...]
        cum = _dot_split3(gr, tri)
        li = gr[r0:r0 + H, :]
        a = cum[2 * H + r0:2 * H + r0 + H, :]
        a_tot = a[:, L - 1:L] if d == 0 else a[:, 0:1]
        m_in = m_scr[r0:r0 + H, 0:1]
        c_old = c_scr[d]
        n_old = n_scr[d]
        v.update(k=k, qt=qt, li=li, a=a, a_tot=a_tot, m_in=m_in, c_old=c_old, n_old=n_old,
                 g_prev=a + m_in,
                 b_t=jnp.transpose(jnp.concatenate([li - a, jnp.zeros((L - H, L), F32)], axis=0)),
                 inter=_dot(jnp.concatenate([c_old, n_old], axis=0).astype(BF16), qt))

    def stage_scores():
        k = v['k']
        kst = jnp.concatenate([jnp.where(lane_h == hd, k, jnp.zeros_like(k)) for hd in range(H)], axis=0)
        v['s_t'] = _dot(kst, v.pop('qt'))

    def stage_weights(heads):
        p_parts, m_rows, wp_rows, den_rows = (v.setdefault(key, []) for key in ('p_parts', 'm_rows', 'wp_rows', 'den_rows'))
        for hd in heads:
            d_t = jnp.where(valid_t, v['b_t'][:, hd:hd + 1] + v['a'][hd:hd + 1, :], NEG)
            g_prev = v['g_prev'][hd:hd + 1, :]
            m_t = jnp.maximum(g_prev, jnp.max(d_t, axis=0, keepdims=True))
            p_t = v['s_t'][hd * L:(hd + 1) * L, :] * jnp.exp(d_t - m_t)
            w_prev = jnp.exp(g_prev - m_t)
            den = jnp.sum(p_t, axis=0, keepdims=True) + w_prev * v['inter'][D_MLSTM + hd:D_MLSTM + hd + 1, :]
            p_parts.append(p_t.astype(BF16))
            m_rows.append(m_t)
            wp_rows.append(w_prev)
            den_rows.append(den)

    def stage_out():
        vt = vt_ref[...]
        k = v['k']
        vbd = jnp.concatenate([jnp.where(row_h == hd, vt, jnp.zeros_like(vt)) for hd in range(H)], axis=1)
        num_t = _dot(vbd, jnp.concatenate(v['p_parts'], axis=0))
        outs = []
        for hd in range(H):
            inv = 1.0 / jnp.maximum(jnp.abs(v['den_rows'][hd]), jnp.exp(-v['m_rows'][hd]))
            sl = slice(hd * DH, (hd + 1) * DH)
            outs.append((num_t[sl, :] + v['wp_rows'][hd] * v['inter'][sl, :]) * inv)
        h_ref[...] = jnp.transpose(jnp.concatenate(outs, axis=0))

    def stage_state():
        vt = vt_ref[...]
        k = v['k']
        a_tot, m_in = v['a_tot'], v['m_in']
        dl = a_tot - v['a'] + v['li']
        m_new = jnp.maximum(a_tot + m_in, jnp.max(dl, axis=1, keepdims=True))
        w = jnp.exp(dl - m_new)
        decay = jnp.exp(a_tot + m_in - m_new)
        wexp = jnp.concatenate([jnp.broadcast_to(w[hd:hd + 1, :], (DH, L)) for hd in range(H)], axis=0)
        dcol = jnp.concatenate([jnp.broadcast_to(decay[hd:hd + 1, :], (DH, 1)) for hd in range(H)], axis=0)
        upd = _dot((vt.astype(F32) * wexp).astype(BF16), k)
        c_scr[d] = dcol * v['c_old'] + jnp.where(blockdiag, upd, 0.0)
        n_upd = _dot(jnp.concatenate([w, jnp.zeros((H, L), F32)], axis=0).astype(BF16), k)
        decay8 = jnp.concatenate([decay, jnp.zeros((H, 1), F32)], axis=0)
        n_scr[d] = jnp.where(nmask, decay8 * v['n_old'] + n_upd, 0.0)
        m_scr[r0:r0 + H, :] = jnp.broadcast_to(m_new, (H, LANES))

    return stage_in, stage_scores, stage_state, functools.partial(stage_weights, range(H)), stage_out


def _mlstm_kernel(kf_ref, qtf_ref, vtf_ref, grf_ref, kb_ref, qtb_ref, vtb_ref, grb_ref,
                  c0_ref, n0_ref, m0_ref, hf_ref, hb_ref, cfin_ref, nfin_ref, mfin_ref, c_scr, n_scr, m_scr):
    c = pl.program_id(0)
    nc = pl.num_programs(0)
    L = MLSTM_L
    H = MLSTM_HEADS
    nb = kf_ref.shape[0]

    @pl.when(c == 0)
    def _():
        c_scr[...] = c0_ref[...]
        n_scr[...] = n0_ref[...]
        m_scr[...] = m0_ref[...]

    ri = lax.broadcasted_iota(jnp.int32, (L, L), 0)
    ci = lax.broadcasted_iota(jnp.int32, (L, L), 1)
    upper = ci >= ri
    lower = ci <= ri
    triu = jnp.where(upper, 1.0, 0.0).astype(BF16)
    tril = jnp.where(lower, 1.0, 0.0).astype(BF16)
    lane_h = lax.broadcasted_iota(jnp.int32, (1, D_MLSTM), 1) // MLSTM_DIM
    row_h = lax.broadcasted_iota(jnp.int32, (D_MLSTM, 1), 0) // MLSTM_DIM
    blockdiag = row_h == lane_h
    nmask = lax.broadcasted_iota(jnp.int32, (2 * H, 1), 0) == lane_h
    cps = kf_ref.shape[1] // L
    chains = []
    for j in range(cps):
        for bi in range(nb):
            st = (c_scr.at[bi], n_scr.at[bi], m_scr.at[bi])
            for d, refs, consts in ((0, (kf_ref, qtf_ref, vtf_ref, grf_ref, hf_ref), (triu, upper)),
                                    (1, (kb_ref, qtb_ref, vtb_ref, grb_ref, hb_ref), (tril, lower))):
                ck = j if d == 0 else cps - 1 - j
                rows = pl.ds(ck * L, L)
                k_ref, qt_ref, vt_ref, gr_ref, h_ref = refs
                chains.append(_mlstm_stages(d, k_ref.at[bi, rows], qt_ref.at[bi, ck], vt_ref.at[bi, ck],
                                            gr_ref.at[bi, :, rows], h_ref.at[bi, rows], *st,
                                            consts + (lane_h, row_h, blockdiag, nmask)))
    n_stage = len(chains[0])
    stage_order = range(n_stage) if cps == 1 else range(n_stage - 1, -1, -1)
    for step in range(len(chains) + n_stage - 1):
        for stage in stage_order:
            ch = step - stage
            if 0 <= ch < len(chains):
                chains[ch][stage]()

    @pl.when(c == nc - 1)
    def _():
        cfin_ref[...] = c_scr[...]
        nfin_ref[...] = n_scr[...]
        mfin_ref[...] = m_scr[...]


def _mlstm(mk, mqt, mvt, grow, state):
    b, n, _ = mk.shape
    cps = max(c for c in range(1, MLSTM_CHUNKS_PER_STEP + 1) if n % (c * MLSTM_L) == 0)
    L = cps * MLSTM_L
    nc = n // L
    nh = 2 * MLSTM_HEADS
    tok = lambda cm: pl.BlockSpec((b, L, D_MLSTM), lambda c: (0, cm(c), 0))
    tr = lambda rows, cm: pl.BlockSpec((b, rows, L), lambda c: (0, 0, cm(c)))
    chk = lambda cm: pl.BlockSpec((b, cps, D_MLSTM, MLSTM_L), lambda c: (0, cm(c), 0, 0))
    fw = lambda c: c
    bw = lambda c: nc - 1 - c
    cspec = pl.BlockSpec((b, 2, D_MLSTM, D_MLSTM), lambda c: (0, 0, 0, 0))
    nspec = pl.BlockSpec((b, 2, nh, D_MLSTM), lambda c: (0, 0, 0, 0))
    mspec = pl.BlockSpec((b, nh, LANES), lambda c: (0, 0, 0))
    return pl.pallas_call(
        _mlstm_kernel,
        grid=(nc,),
        in_specs=[tok(fw), chk(fw), chk(fw), tr(N_GATES, fw),
                  tok(bw), chk(bw), chk(bw), tr(N_GATES, bw), cspec, nspec, mspec],
        out_specs=[tok(fw), tok(bw), cspec, nspec, mspec],
        out_shape=[jax.ShapeDtypeStruct((b, n, D_MLSTM), F32), jax.ShapeDtypeStruct((b, n, D_MLSTM), F32),
                   jax.ShapeDtypeStruct((b, 2, D_MLSTM, D_MLSTM), F32),
                   jax.ShapeDtypeStruct((b, 2, nh, D_MLSTM), F32),
                   jax.ShapeDtypeStruct((b, nh, LANES), F32)],
        scratch_shapes=[pltpu.VMEM((b, 2, D_MLSTM, D_MLSTM), F32), pltpu.VMEM((b, 2, nh, D_MLSTM), F32),
                        pltpu.VMEM((b, nh, LANES), F32)],
        compiler_params=_cparams(("arbitrary",)),
        name="mlstm",
    )(mk, mqt, mvt, grow, mk, mqt, mvt, grow, *state)


def _rope_tables(n):
    rows = n // GRID_W
    row = jnp.broadcast_to(jnp.arange(rows, dtype=F32)[:, None], (rows, GRID_W)).reshape(n)
    colp = jnp.broadcast_to(jnp.arange(GRID_W, dtype=F32)[None, :], (rows, GRID_W)).reshape(n)
    nf = HEAD_DIM // 4
    inv = ROPE_BASE ** (-jnp.arange(nf, dtype=F32) / nf)
    ar = row[:, None] * inv
    ac = colp[:, None] * inv
    ang = jnp.concatenate([ar, ar, ac, ac], axis=-1)
    cos = jnp.cos(ang)
    sin = jnp.sin(ang)
    sign = jnp.where((jnp.arange(HEAD_DIM) & 16) == 0, -1.0, 1.0).astype(F32)
    reps = LANES // HEAD_DIM
    return jnp.tile(cos, (1, reps)), jnp.tile(sin * sign, (1, reps))


def _dft_factor_tables(n):
    nk1 = n // DFT_K0
    j = jnp.arange(n, dtype=jnp.int32)
    k1 = jnp.arange(nk1, dtype=jnp.int32) * DFT_K0
    k0 = jnp.arange(DFT_K0, dtype=jnp.int32)
    w = 2.0 * math.pi / n
    ang_a = ((k1[:, None] * j[None, :]) % n).astype(F32) * w
    ang_b = ((k0[:, None] * j[None, :]) % n).astype(F32) * w
    return jnp.cos(ang_a), jnp.sin(ang_a), jnp.cos(ang_b), jnp.sin(ang_b)


def _dft_half_tables(n):
    nh = n // 2
    nk1 = -(-(nh // DFT_K0 + 1) // SUBLANES) * SUBLANES
    j = jnp.arange(nh, dtype=jnp.int32)
    k1 = jnp.arange(nk1, dtype=jnp.int32) * DFT_K0
    k0 = jnp.arange(DFT_K0, dtype=jnp.int32)
    w = 2.0 * math.pi / n
    ang_a = ((k1[:, None] * j[None, :]) % n).astype(F32) * w
    ang_b = ((k0[:, None] * j[None, :]) % n).astype(F32) * w
    return jnp.cos(ang_a), jnp.sin(ang_a), jnp.cos(ang_b), jnp.sin(ang_b)


def _dft_tables(n):
    ca, sa, cb, sb = _dft_factor_tables(n)
    cn = (ca[:, None, :] * cb[None] - sa[:, None, :] * sb[None]).reshape(n, n)
    sn = (sa[:, None, :] * cb[None] + ca[:, None, :] * sb[None]).reshape(n, n)
    return cn.astype(BF16), sn.astype(BF16)


def _blockdiag(blocks):
    g, c = blocks.shape[-3], blocks.shape[-1]
    eye = jnp.eye(g, dtype=blocks.dtype)
    out = jnp.einsum('...gce,gh->...gche', blocks, eye)
    return out.reshape(blocks.shape[:-3] + (g * c, g * c))


def _channel_dft_blockdiag():
    e = jnp.arange(FOURIER_CH, dtype=jnp.int32)
    ang = ((e[:, None] * e[None, :]) % FOURIER_CH).astype(F32) * (2.0 * math.pi / FOURIER_CH)
    reps = (FOURIER_GROUPS, 1, 1)
    return _blockdiag(jnp.tile(jnp.cos(ang)[None], reps)), _blockdiag(jnp.tile(-jnp.sin(ang)[None], reps))


def kernel(x, c, ctx, c_ctx, w_ada, b_ada, norm_g, w_ffn_in, w_ffn_out, w_in, w_out, w_fourier, attn_sink,
           conv_qk, b_gate_i, b_gate_f, g_final):
    b, n, d = x.shape
    cl = ctx.shape[1]
    depth = w_ada.shape[0]
    nh = 2 * MLSTM_HEADS

    rows = -(-(b + 1) // SUBLANES) * SUBLANES
    c_all = jnp.zeros((rows, d), F32).at[:b].set(c).at[b].set(c_ctx)
    mod = _modulation(c_all, w_ada, b_ada).reshape(depth, rows, N_MOD, d)
    x_row = lambda bi: bi
    c_row = lambda bi: b

    offs = [0]
    for s in (D_FOURIER, D_ATT, D_KV, D_KV, D_MLSTM, D_MLSTM, D_MLSTM, D_MLSTM, nh, nh):
        offs.append(offs[-1] + s)
    assert offs[1] == D_FOURIER and offs[8] - offs[1] == C_END - C_Q
    w_gate = w_in[:, :, offs[8]:offs[10]]
    cc_bd, sc_bd = _channel_dft_blockdiag()
    w_p, w_p_ctx, w_rest = _fourier_fold(w_in, _blockdiag(w_fourier), cc_bd, sc_bd,
                                         1.0 / math.sqrt(n * FOURIER_CH), 1.0 / math.sqrt(cl * FOURIER_CH))
    wg_t = jnp.swapaxes(w_gate, 1, 2).astype(BF16)
    gb_row = jnp.concatenate([b_gate_i.reshape(depth, nh), b_gate_f.reshape(depth, nh)], axis=-1)[:, :, None]
    wf = (w_ffn_in[0, 0].astype(BF16), w_ffn_out[0, 0].astype(BF16))
    wo = w_out.astype(BF16)

    cos128, sin128 = _rope_tables(n)
    half_spectrum = n % (2 * TK_DFT) == 0 and TK_DFT == TM
    dft_x = _dft_half_tables(n) if half_spectrum else _dft_tables(n)
    cnc, snc = _dft_tables(cl)
    zeros_c = jnp.zeros((cl, LANES), F32)
    state0 = (jnp.zeros((b, 2, D_MLSTM, D_MLSTM), F32), jnp.zeros((b, 2, nh, D_MLSTM), F32),
              jnp.zeros((b, nh, LANES), F32))

    xc = ctx
    for l in range(depth):
        last = l == depth - 1
        x, *wf_next = _ffn(x, l, 0, mod, x_row, norm_g, *wf, cast_next=(w_ffn_in, w_ffn_out, l, 1))
        xc = _ffn(xc, l, 0, mod, c_row, norm_g, *wf)
        wf = wf_next

        px, qtx, kx, vtx, mkx, mqtx, mvtx, ogx, grx = _inproj(
            x, l, mod, x_row, norm_g, w_p, w_rest, wg_t, gb_row, conv_qk, cos128, sin128, True)
        pc, qtc, kc, vtc, mkc, mqtc, mvtc, ogc, grc = _inproj(
            xc, l, mod, c_row, norm_g, w_p_ctx, w_rest, wg_t, gb_row, conv_qk, zeros_c, zeros_c, False)

        hcf, hcb, *state_c = _mlstm(mkc, mqtc, mvtc, grc, state0)
        hxf, hxb, *_ = _mlstm(mkx, mqtx, mvtx, grx, state_c)
        att_x = _attention(attn_sink, l, qtx, kx, vtx, kc, vtc, True)
        yf_x = tuple(_dft_half(dft_x, px)) if half_spectrum else _dft(*dft_x, px)
        mix_x = (yf_x, att_x, hxf, hxb, ogx)
        if last:
            x = _ffn(x, l, 2, mod, x_row, norm_g, *wf, mixer=mix_x, w_o=wo, g_final=g_final)
        else:
            x, *wf_next = _ffn(x, l, 2, mod, x_row, norm_g, *wf, mixer=mix_x, w_o=wo,
                               cast_next=(w_ffn_in, w_ffn_out, l + 1, 0))
            att_c = _attention(attn_sink, l, qtc, None, None, kc, vtc, False)
            yf_c = _dft(cnc, snc, pc)
            xc = _ffn(xc, l, 2, mod, c_row, norm_g, *wf, mixer=(yf_c, att_c, hcf, hcb, ogc), w_o=wo)
            wf = wf_next
    return x
```
